```python
import math
import jax, jax.numpy as jnp
from jax import lax
import numpy as np

D_MODEL = 1024
BATCH = 4
SEQ = 8192
DEPTH = 4
DEC_BATCH = 8
DEC_SEQ = 16
PAST_LEN = 2048

CHUNK = 64
N_META = 16
N_MIXERS = 2
N_A = (DEPTH + 1) // 2
N_B = DEPTH // 2
N_HEADS = 8
HEAD_DIM = D_MODEL // N_HEADS
HD = N_HEADS * HEAD_DIM
IDX_HEADS = 8
IDX_DIM = 64
TOPK_MAX = 256
N_BUCKETS = 32
MAX_DISTANCE = 128
D_FF = 2816
CONV_W = 3
Q_BLOCK = 128
ALPHA = (2 * DEPTH) ** 0.25
BETA = (8 * DEPTH) ** -0.25
LN_EPS = 1e-5
IDX_Q_OFF = 3 * HD
IDX_K_OFF = IDX_Q_OFF + IDX_HEADS * IDX_DIM
IDX_W_OFF = IDX_K_OFF + IDX_DIM
A_IN = IDX_W_OFF + IDX_HEADS
B_IN = 3 * HD

kernel_name = 'dsa_stickbreak_convffn_stream'


def layer_norm(x, g, b):
    xf = x.astype(jnp.float32)
    mu = jnp.mean(xf, -1, keepdims=True)
    var = jnp.mean(jnp.square(xf - mu), -1, keepdims=True)
    y = (xf - mu) * lax.rsqrt(var + LN_EPS) * g.astype(jnp.float32) + b.astype(jnp.float32)
    return y.astype(x.dtype)


def rel_bucket(rel):
    nb = N_BUCKETS // 2
    max_exact = nb // 2
    ret = jnp.where(rel > 0, nb, 0)
    n = jnp.abs(rel)
    nf = jnp.maximum(n, 1).astype(jnp.float32)
    large = max_exact + (jnp.log(nf / max_exact) / math.log(MAX_DISTANCE / max_exact)
                         * (nb - max_exact)).astype(jnp.int32)
    large = jnp.minimum(large, nb - 1)
    return ret + jnp.where(n < max_exact, n, large)


def chunk_id(pos, n_lead):
    return jnp.where(pos < n_lead, -1, (pos - n_lead) // CHUNK)


def map_query_blocks(fn, batched, vectors):
    b, t = batched[0].shape[:2]
    n_blk = -(-t // Q_BLOCK)
    pad = n_blk * Q_BLOCK - t
    def split_b(a):
        a = jnp.pad(a, [(0, 0), (0, pad)] + [(0, 0)] * (a.ndim - 2))
        return jnp.swapaxes(a.reshape((b, n_blk, Q_BLOCK) + a.shape[2:]), 0, 1)
    def split_v(a):
        return jnp.pad(a, (0, pad), mode='edge').reshape(n_blk, Q_BLOCK)
    blocks = tuple(split_b(a) for a in batched) + tuple(split_v(a) for a in vectors)
    out = lax.map(lambda blk: fn(*blk), blocks)
    out = jnp.swapaxes(out, 0, 1)
    return out.reshape((b, n_blk * Q_BLOCK) + out.shape[3:])[:, :t]


def dsa_attend(q, qi, wi, k, v, ki, q_pos, k_pos, q_chunk, k_chunk, rel_bias, topk):
    s = jnp.einsum('bqhd,bld->bqhl', qi, ki).astype(jnp.float32)
    score = jnp.einsum('bqhl,bqh->bql', jax.nn.relu(s), wi.astype(jnp.float32))
    visible = k_chunk[None, :] <= q_chunk[:, None]
    score = jnp.where(visible[None], score, -jnp.inf)
    _, idx = lax.top_k(score, topk)
    gather = jax.vmap(lambda a, i: a[i])
    k_sel = gather(k, idx)
    v_sel = gather(v, idx)
    logits = jnp.einsum('bqhd,bqkhd->bhqk', q, k_sel).astype(jnp.float32) * (HEAD_DIM ** -0.5)
    rel = k_pos[idx] - q_pos[None, :, None]
    bias = rel_bias.astype(jnp.float32)[rel_bucket(rel)]
    logits = logits + jnp.transpose(bias, (0, 3, 1, 2))
    valid = k_chunk[idx] <= q_chunk[None, :, None]
    logits = jnp.where(valid[:, None], logits, -jnp.inf)
    p = jax.nn.softmax(logits, axis=-1).astype(v.dtype)
    return jnp.einsum('bhqk,bqkhd->bqhd', p, v_sel)


def stick_breaking(q, k, v, q_pos, k_pos):
    z = jnp.einsum('bqhd,blhd->bhql', q, k).astype(jnp.float32) * (HEAD_DIM ** -0.5)
    causal = (k_pos[None, :] < q_pos[:, None])[None, None]
    log_1m = jnp.where(causal, jax.nn.log_sigmoid(-z), 0.0)
    between = lax.cumsum(log_1m, axis=3, reverse=True) - log_1m
    a = jnp.where(causal, jnp.exp(jax.nn.log_sigmoid(z) + between), 0.0)
    return jnp.einsum('bhql,blhd->bqhd', a.astype(v.dtype), v)


def project_a(x, w_in):
    b, t, _ = x.shape
    h = x @ w_in
    q = h[..., :HD].reshape(b, t, N_HEADS, HEAD_DIM)
    k = h[..., HD:2 * HD].reshape(b, t, N_HEADS, HEAD_DIM)
    v = h[..., 2 * HD:3 * HD].reshape(b, t, N_HEADS, HEAD_DIM)
    qi = h[..., IDX_Q_OFF:IDX_K_OFF].reshape(b, t, IDX_HEADS, IDX_DIM)
    ki = h[..., IDX_K_OFF:IDX_W_OFF]
    wi = h[..., IDX_W_OFF:] * ((IDX_HEADS * IDX_DIM) ** -0.5)
    return q, k, v, qi, ki, wi


def project_b(x, w_in):
    b, t, _ = x.shape
    h = (x @ w_in).reshape(b, t, 3, N_HEADS, HEAD_DIM)
    return h[:, :, 0], h[:, :, 1], h[:, :, 2]


def mixer_a_prompt(x, w_in, w_out, rel_bias, topk):
    b, t, _ = x.shape
    q, k, v, qi, ki, wi = project_a(x, w_in)
    pos = jnp.arange(t, dtype=jnp.int32)
    chunk = chunk_id(pos, N_META)
    fn = lambda qb, qib, wib, pb, cb: dsa_attend(qb, qib, wib, k, v, ki, pb, pos, cb, chunk, rel_bias, topk)
    o = map_query_blocks(fn, (q, qi, wi), (pos, chunk))
    return o.reshape(b, t, HD) @ w_out, k, v, ki


def mixer_a_sample(x, ck, cv, cki, w_in, w_out, rel_bias, topk):
    b, t, _ = x.shape
    p = ck.shape[1]
    q, k, v, qi, ki, wi = project_a(x, w_in)
    k_all = jnp.concatenate([ck.astype(k.dtype), k], axis=1)
    v_all = jnp.concatenate([cv.astype(v.dtype), v], axis=1)
    ki_all = jnp.concatenate([cki.astype(ki.dtype), ki], axis=1)
    k_pos = jnp.arange(p + t, dtype=jnp.int32)
    q_pos = p + jnp.arange(t, dtype=jnp.int32)
    o = dsa_attend(q, qi, wi, k_all, v_all, ki_all, q_pos, k_pos,
                   chunk_id(q_pos, 0), chunk_id(k_pos, 0), rel_bias, topk)
    return o.reshape(b, t, HD) @ w_out, k, v, ki


def mixer_b_prompt(x, w_in, w_out):
    b, t, _ = x.shape
    q, k, v = project_b(x, w_in)
    pos = jnp.arange(t, dtype=jnp.int32)
    fn = lambda qb, pb: stick_breaking(qb, k, v, pb, pos)
    o = map_query_blocks(fn, (q,), (pos,))
    return o.reshape(b, t, HD) @ w_out, k, v


def mixer_b_sample(x, ck, cv, w_in, w_out):
    b, t, _ = x.shape
    p = ck.shape[1]
    q, k, v = project_b(x, w_in)
    k_all = jnp.concatenate([ck.astype(k.dtype), k], axis=1)
    v_all = jnp.concatenate([cv.astype(v.dtype), v], axis=1)
    k_pos = jnp.arange(p + t, dtype=jnp.int32)
    q_pos = p + jnp.arange(t, dtype=jnp.int32)
    o = stick_breaking(q, k_all, v_all, q_pos, k_pos)
    return o.reshape(b, t, HD) @ w_out, k, v


def conv_ffn(x, left, w_gate, w_up, conv_w, conv_b, w_down):
    t = x.shape[1]
    g = x @ w_gate
    u = x @ w_up
    gp = jnp.concatenate([left.astype(g.dtype), g], axis=1)
    gc = conv_b + conv_w[0] * gp[:, 0:t]
    for i in range(1, CONV_W):
        gc = gc + conv_w[i] * gp[:, i:i + t]
    h = jax.nn.gelu(gc) * u
    return h @ w_down, gp[:, gp.shape[1] - (CONV_W - 1):]


def setup_inputs(seed: int = 0) -> dict:
    key = jax.random.key(seed)
    ks = jax.random.split(key, 24)
    def nrm(k, shape, scale=1.0):
        return jax.random.normal(k, shape, jnp.float32) * scale
    return {
        'x_prompt': nrm(ks[0], (BATCH, SEQ, D_MODEL)),
        'x_sample': nrm(ks[1], (DEC_BATCH, DEC_SEQ, D_MODEL)),
        'cache_a_k': nrm(ks[2], (N_A, DEC_BATCH, PAST_LEN, N_HEADS, HEAD_DIM)),
        'cache_a_v': nrm(ks[3], (N_A, DEC_BATCH, PAST_LEN, N_HEADS, HEAD_DIM)),
        'cache_a_idx_k': nrm(ks[4], (N_A, DEC_BATCH, PAST_LEN, IDX_DIM)),
        'cache_b_k': nrm(ks[5], (N_B, DEC_BATCH, PAST_LEN, N_HEADS, HEAD_DIM)),
        'cache_b_v': nrm(ks[6], (N_B, DEC_BATCH, PAST_LEN, N_HEADS, HEAD_DIM)),
        'state_ffn_conv': nrm(ks[7], (DEPTH, DEC_BATCH, CONV_W - 1, D_FF)),
        'meta_tokens': nrm(ks[8], (N_META, D_MODEL)),
        'rel_bias': nrm(ks[9], (N_BUCKETS, N_HEADS), 0.5),
        'w_a_in': nrm(ks[10], (N_A, D_MODEL, A_IN), D_MODEL ** -0.5),
        'w_a_out': nrm(ks[11], (N_A, HD, D_MODEL), BETA * HD ** -0.5),
        'w_b_in': nrm(ks[12], (N_B, D_MODEL, B_IN), D_MODEL ** -0.5),
        'w_b_out': nrm(ks[13], (N_B, HD, D_MODEL), BETA * HD ** -0.5),
        'ln1_g': 1.0 + nrm(ks[14], (DEPTH, D_MODEL), 0.05),
        'ln1_b': nrm(ks[15], (DEPTH, D_MODEL), 0.02),
        'ln2_g': 1.0 + nrm(ks[16], (DEPTH, D_MODEL), 0.05),
        'ln2_b': nrm(ks[17], (DEPTH, D_MODEL), 0.02),
        'w_ffn_gate': nrm(ks[18], (DEPTH, D_MODEL, D_FF), D_MODEL ** -0.5),
        'w_ffn_up': nrm(ks[19], (DEPTH, D_MODEL, D_FF), D_MODEL ** -0.5),
        'ffn_conv_w': nrm(ks[20], (DEPTH, CONV_W, D_FF), CONV_W ** -0.5),
        'ffn_conv_b': nrm(ks[21], (DEPTH, D_FF), 0.02),
        'w_ffn_down': nrm(ks[22], (DEPTH, D_FF, D_MODEL), BETA * D_FF ** -0.5),
    }


def reference(x_prompt, x_sample, cache_a_k, cache_a_v, cache_a_idx_k, cache_b_k, cache_b_v,
              state_ffn_conv, meta_tokens, rel_bias, w_a_in, w_a_out, w_b_in, w_b_out,
              ln1_g, ln1_b, ln2_g, ln2_b, w_ffn_gate, w_ffn_up, ffn_conv_w, ffn_conv_b, w_ffn_down):
    b = x_prompt.shape[0]
    meta = jnp.broadcast_to(meta_tokens.astype(x_prompt.dtype)[None], (b, N_META, D_MODEL))
    xp = jnp.concatenate([meta, x_prompt], axis=1)
    xs = x_sample
    topk_p = min(TOPK_MAX, x_prompt.shape[1] // 4)
    topk_s = min(TOPK_MAX, (cache_a_k.shape[2] + x_sample.shape[1]) // 4)
    zero_left = jnp.zeros((b, CONV_W - 1, D_FF), xp.dtype)
    a_k_p, a_v_p, a_ik_p, a_k_s, a_v_s, a_ik_s = [], [], [], [], [], []
    b_k_p, b_v_p, b_k_s, b_v_s = [], [], [], []
    conv_p, conv_s = [], []
    for i in range(DEPTH):
        j = i // N_MIXERS
        if i % N_MIXERS == 0:
            mp, kp, vp, kip = mixer_a_prompt(xp, w_a_in[j], w_a_out[j], rel_bias, topk_p)
            ms, kn, vn, kin = mixer_a_sample(xs, cache_a_k[j], cache_a_v[j], cache_a_idx_k[j],
                                             w_a_in[j], w_a_out[j], rel_bias, topk_s)
            a_k_p.append(kp); a_v_p.append(vp); a_ik_p.append(kip)
            a_k_s.append(kn); a_v_s.append(vn); a_ik_s.append(kin)
        else:
            mp, kp, vp = mixer_b_prompt(xp, w_b_in[j], w_b_out[j])
            ms, kn, vn = mixer_b_sample(xs, cache_b_k[j], cache_b_v[j], w_b_in[j], w_b_out[j])
            b_k_p.append(kp); b_v_p.append(vp)
            b_k_s.append(kn); b_v_s.append(vn)
        xp = layer_norm(ALPHA * xp + mp, ln1_g[i], ln1_b[i])
        xs = layer_norm(ALPHA * xs + ms, ln1_g[i], ln1_b[i])
        fp, cp = conv_ffn(xp, zero_left, w_ffn_gate[i], w_ffn_up[i], ffn_conv_w[i], ffn_conv_b[i], w_ffn_down[i])
        fs, cs = conv_ffn(xs, state_ffn_conv[i], w_ffn_gate[i], w_ffn_up[i], ffn_conv_w[i], ffn_conv_b[i], w_ffn_down[i])
        conv_p.append(cp); conv_s.append(cs)
        xp = layer_norm(ALPHA * xp + fp, ln2_g[i], ln2_b[i])
        xs = layer_norm(ALPHA * xs + fs, ln2_g[i], ln2_b[i])
    y_prompt = xp[:, N_META:]
    return (y_prompt, xs,
            jnp.stack(a_k_p), jnp.stack(a_v_p), jnp.stack(a_ik_p),
            jnp.stack(b_k_p), jnp.stack(b_v_p), jnp.stack(conv_p),
            jnp.stack(a_k_s), jnp.stack(a_v_s), jnp.stack(a_ik_s),
            jnp.stack(b_k_s), jnp.stack(b_v_s), jnp.stack(conv_s))
```

```python
import functools
import math

import jax
import jax.numpy as jnp
from jax import lax
from jax.experimental import pallas as pl
from jax.experimental.pallas import tpu as pltpu

D_MODEL = 1024
N_HEADS = 8
HEAD_DIM = 128
HD = N_HEADS * HEAD_DIM
IDX_HEADS = 8
IDX_DIM = 64
IDX_QW = IDX_HEADS * IDX_DIM
CHUNK = 64
CHUNK_SHIFT = 6
N_META = 16
N_MIXERS = 2
TOPK_MAX = 256
N_BUCKETS = 32
MAX_DISTANCE = 128
CONV_W = 3
LN_EPS = 1e-5

LANES = 128
SUBLANES = 8
ROW_TILE = 256
KEY_TILE = 256
FF_CHUNK = 256
SEL_ROWS = 64
VMEM_LIMIT = 56 * 1024 * 1024

NEG_BIG = -1e30
INT_MIN = -2 ** 31
KEY_NEG_INF = -2139095041
EXP_ZERO_BELOW = -104.0

_NT = (((1,), (1,)), ((), ()))


def _round_up(n, m):
    return -(-n // m) * m


def _layer_norm(x, g, b):
    mu = jnp.mean(x, -1, keepdims=True)
    xc = x - mu
    var = jnp.mean(xc * xc, -1, keepdims=True)
    return xc * lax.rsqrt(var + LN_EPS) * g + b


def _proj_a_kernel(x_ref, wq_ref, wk_ref, wv_ref, wqi_ref, wkw_ref, cs_ref,
                   q_ref, kc_ref, kb_ref, vc_ref, vb_ref, qi_ref, kic_ref, kib_ref, kw_ref):
    xb = x_ref[0].astype(jnp.bfloat16)
    q = jnp.dot(xb, wq_ref[...], preferred_element_type=jnp.float32)
    q_ref[0] = (q * (HEAD_DIM ** -0.5)).astype(jnp.bfloat16)
    k = jnp.dot(xb, wk_ref[...], preferred_element_type=jnp.float32)
    kc_ref[0] = k
    kb_ref[0] = k.astype(jnp.bfloat16)
    v = jnp.dot(xb, wv_ref[...], preferred_element_type=jnp.float32)
    vc_ref[0] = v
    vb_ref[0] = v.astype(jnp.bfloat16)
    qi = jnp.dot(xb, wqi_ref[...], preferred_element_type=jnp.float32)
    qi_ref[0] = qi.astype(jnp.bfloat16)
    kw = jnp.dot(xb, wkw_ref[...], preferred_element_type=jnp.float32)
    kic_ref[0] = kw[:, :IDX_DIM]
    kib_ref[0] = kw[:, :IDX_DIM].astype(jnp.bfloat16)
    kw_ref[0] = kw * cs_ref[...]


def _proj_b_kernel(x_ref, wq_ref, wk_ref, wv_ref, q_ref, kc_ref, kb_ref, vc_ref, vb_ref):
    xb = x_ref[0].astype(jnp.bfloat16)
    q = jnp.dot(xb, wq_ref[...], preferred_element_type=jnp.float32)
    q_ref[0] = (q * (HEAD_DIM ** -0.5)).astype(jnp.bfloat16)
    k = jnp.dot(xb, wk_ref[...], preferred_element_type=jnp.float32)
    kc_ref[0] = k
    kb_ref[0] = k.astype(jnp.bfloat16)
    v = jnp.dot(xb, wv_ref[...], preferred_element_type=jnp.float32)
    vc_ref[0] = v
    vb_ref[0] = v.astype(jnp.bfloat16)


def _const_spec(shape):
    return pl.BlockSpec(shape, lambda *_: (0,) * len(shape))


def _project(x, t_valid, tm, weights, is_a):
    b, tp, _ = x.shape
    grid = (b, tp // tm)
    row = lambda w: pl.BlockSpec((1, tm, w), lambda i, j: (i, j, 0))
    bf = jnp.bfloat16
    f32 = jnp.float32
    pad = lambda w, dt: jax.ShapeDtypeStruct((b, tp, w), dt)
    exact = lambda w, dt: jax.ShapeDtypeStruct((b, t_valid, w), dt)
    in_specs = [row(D_MODEL)] + [_const_spec(w.shape) for w in weights]
    if is_a:
        out_shape = [pad(HD, bf), exact(HD, f32), pad(HD, bf), exact(HD, f32), pad(HD, bf),
                     pad(IDX_QW, bf), exact(IDX_DIM, f32), pad(IDX_DIM, bf), pad(LANES, f32)]
        out_specs = [row(HD), row(HD), row(HD), row(HD), row(HD),
                     row(IDX_QW), row(IDX_DIM), row(IDX_DIM), row(LANES)]
        body = _proj_a_kernel
    else:
        out_shape = [pad(HD, bf), exact(HD, f32), pad(HD, bf), exact(HD, f32), pad(HD, bf)]
        out_specs = [row(HD)] * 5
        body = _proj_b_kernel
    return pl.pallas_call(
        body, grid=grid, in_specs=in_specs, out_specs=out_specs, out_shape=out_shape,
        compiler_params=pltpu.CompilerParams(
            dimension_semantics=("parallel", "parallel"), vmem_limit_bytes=VMEM_LIMIT),
        name="proj_a" if is_a else "proj_b",
    )(x, *weights)


def _vis_end(qpos, n_lead, l_valid):
    ve = jnp.where(qpos < n_lead, n_lead,
                   n_lead + ((((qpos - n_lead) >> CHUNK_SHIFT) + 1) << CHUNK_SHIFT))
    return jnp.minimum(ve, l_valid)


def _attn_a_kernel(q_ref, qi_ref, kwq_ref, k_ref, v_ref, kib_ref, bias_ref, o_ref,
                   keys_ref, madd_ref, thr_ref, need_ref, cut_ref,
                   *, tq, tk, q_off, n_lead, l_valid, topk, idx_bits):
    i = pl.program_id(1)
    h = pl.program_id(2)
    qpos0 = q_off + i * tq
    nkb = (_vis_end(qpos0 + tq - 1, n_lead, l_valid) + tk - 1) // tk
    rc = min(tq, SEL_ROWS)
    n_rc = tq // rc

    @pl.when(h == 0)
    def _select():
        qpos = qpos0 + lax.broadcasted_iota(jnp.int32, (tq, 1), 0)
        vis_end = _vis_end(qpos, n_lead, l_valid)
        qi = qi_ref[0]
        wi = kwq_ref[0][:, IDX_DIM:IDX_DIM + IDX_HEADS]

        def score_block(j, carry):
            ds = pl.ds(pl.multiple_of(j * tk, tk), tk)
            kib = kib_ref[0, ds, :]
            sc = jnp.zeros((tq, tk), jnp.float32)
            for ih in range(IDX_HEADS):
                s = lax.dot_general(qi[:, ih * IDX_DIM:(ih + 1) * IDX_DIM], kib, _NT,
                                    preferred_element_type=jnp.float32)
                sc = sc + jnp.maximum(s, 0.0) * wi[:, ih:ih + 1]
            sc = jnp.where(sc == 0.0, 0.0, sc)
            bits = pltpu.bitcast(sc, jnp.int32)
            key = jnp.where(bits < 0, bits ^ 0x7FFFFFFF, bits)
            col = j * tk + lax.broadcasted_iota(jnp.int32, (tq, tk), 1)
            keys_ref[:, ds] = jnp.where(col < vis_end, key, KEY_NEG_INF)
            return carry

        lax.fori_loop(0, nkb, score_block, 0)

        def count(r0, pred):
            def blk(j, cnt):
                kk = keys_ref[pl.ds(r0, rc), pl.ds(pl.multiple_of(j * tk, tk), tk)]
                for c in range(tk // LANES):
                    cnt = cnt + pred(kk[:, c * LANES:(c + 1) * LANES],
                                     j * tk + c * LANES).astype(jnp.int32)
                return cnt
            cnt = lax.fori_loop(0, nkb, blk, jnp.zeros((rc, LANES), jnp.int32))
            return jnp.broadcast_to(jnp.sum(cnt, axis=1, keepdims=True), (rc, LANES))

        lane = lax.broadcasted_iota(jnp.int32, (rc, LANES), 1)
        any_excess = jnp.zeros((), jnp.int32)
        for r in range(n_rc):
            r0 = r * rc

            def bit_step(b, ut):
                cand = (ut | (jnp.int32(1) << (31 - b))) ^ INT_MIN
                cnt = count(r0, lambda kk, c0: kk >= cand)
                return jnp.where(cnt >= topk, cand ^ INT_MIN, ut)

            ut = lax.fori_loop(0, 32, bit_step, jnp.zeros((rc, LANES), jnp.int32))
            thr = ut ^ INT_MIN
            cnt_gt = count(r0, lambda kk, c0: kk > thr)
            cnt_ge = count(r0, lambda kk, c0: kk >= thr)
            thr_ref[pl.ds(r0, rc), :] = thr
            need_ref[pl.ds(r0, rc), :] = topk - cnt_gt
            cut_ref[pl.ds(r0, rc), :] = jnp.full((rc, LANES), 1 << idx_bits, jnp.int32)
            excess = (cnt_ge > topk) & (thr != KEY_NEG_INF)
            any_excess = jnp.maximum(any_excess, jnp.max(excess.astype(jnp.int32)))

        @pl.when(any_excess > 0)
        def _ties():
            for r in range(n_rc):
                r0 = r * rc
                thr = thr_ref[pl.ds(r0, rc), :]
                need = need_ref[pl.ds(r0, rc), :]

                def bit_step(b, cut):
                    cand = cut | (jnp.int32(1) << (idx_bits - 1 - b))
                    cnt = count(r0, lambda kk, c0: (kk == thr) & (c0 + lane < cand))
                    return jnp.where(cnt <= need, cand, cut)

                cut_ref[pl.ds(r0, rc), :] = lax.fori_loop(
                    0, idx_bits, bit_step, jnp.zeros((rc, LANES), jnp.int32))

        def mask_block(j, carry):
            ds = pl.ds(pl.multiple_of(j * tk, tk), tk)
            for r in range(n_rc):
                rows = pl.ds(r * rc, rc)
                thr = thr_ref[rows, :]
                cut = cut_ref[rows, :]
                for c in range(tk // LANES):
                    cs = pl.ds(pl.multiple_of(j * tk + c * LANES, LANES), LANES)
                    kk = keys_ref[rows, cs]
                    col = j * tk + c * LANES + lane
                    sel = ((kk > thr) | ((kk == thr) & (col < cut))) & (kk != KEY_NEG_INF)
                    madd_ref[rows, cs] = jnp.where(sel, 0.0, NEG_BIG)
            return carry

        lax.fori_loop(0, nkb, mask_block, 0)

    q = q_ref[0]
    jq = qpos0 // tk

    def body(j, carry):
        m, l, acc = carry
        ds = pl.ds(pl.multiple_of(j * tk, tk), tk)
        s = lax.dot_general(q, k_ref[0, ds, :], _NT, preferred_element_type=jnp.float32)
        d = jnp.clip(j - jq, -2, 1) + 2
        s = s + bias_ref[d, 0] + madd_ref[:, ds]
        m_new = jnp.maximum(m, jnp.max(s, axis=1, keepdims=True))
        alpha = jnp.exp(m - m_new)
        p = jnp.exp(s - m_new)
        l = alpha * l + jnp.sum(p, axis=1, keepdims=True)
        acc = alpha * acc + jnp.dot(p.astype(jnp.bfloat16), v_ref[0, ds, :],
                                    preferred_element_type=jnp.float32)
        return m_new, l, acc

    m0 = jnp.full((tq, 1), NEG_BIG, jnp.float32)
    l0 = jnp.zeros((tq, 1), jnp.float32)
    a0 = jnp.zeros((tq, HEAD_DIM), jnp.float32)
    _, l, acc = lax.fori_loop(0, nkb, body, (m0, l0, a0))
    o_ref[0] = (acc / l).astype(o_ref.dtype)


def _attn_a(q, qi, kw, k, v, kib, bias, *, tq, tk, q_off, n_lead, l_valid, topk):
    b, tqp, _ = q.shape
    lp = k.shape[1]
    idx_bits = lp.bit_length()
    grid = (b, tqp // tq, N_HEADS)
    kern = functools.partial(_attn_a_kernel, tq=tq, tk=tk, q_off=q_off, n_lead=n_lead,
                             l_valid=l_valid, topk=topk, idx_bits=idx_bits)
    return pl.pallas_call(
        kern, grid=grid,
        in_specs=[
            pl.BlockSpec((1, tq, HEAD_DIM), lambda bi, i, h: (bi, i, h)),
            pl.BlockSpec((1, tq, IDX_QW), lambda bi, i, h: (bi, i, 0)),
            pl.BlockSpec((1, tq, LANES), lambda bi, i, h: (bi, i, 0)),
            pl.BlockSpec((1, lp, HEAD_DIM), lambda bi, i, h: (bi, 0, h)),
            pl.BlockSpec((1, lp, HEAD_DIM), lambda bi, i, h: (bi, 0, h)),
            pl.BlockSpec((1, lp, IDX_DIM), lambda bi, i, h: (bi, 0, 0)),
            pl.BlockSpec((4, 1, tq, tk), lambda bi, i, h: (0, h, 0, 0)),
        ],
        out_specs=pl.BlockSpec((1, tq, HEAD_DIM), lambda bi, i, h: (bi, i, h)),
        out_shape=jax.ShapeDtypeStruct((b, tqp, HD), jnp.bfloat16),
        scratch_shapes=[
            pltpu.VMEM((tq, lp), jnp.int32),
            pltpu.VMEM((tq, lp), jnp.float32),
            pltpu.VMEM((tq, LANES), jnp.int32),
            pltpu.VMEM((tq, LANES), jnp.int32),
            pltpu.VMEM((tq, LANES), jnp.int32),
        ],
        compiler_params=pltpu.CompilerParams(
            dimension_semantics=("parallel", "parallel", "arbitrary"),
            vmem_limit_bytes=VMEM_LIMIT),
        name="dsa_attention",
    )(q, qi, kw, k, v, kib, bias)


def _rel_bucket(rel):
    nb = N_BUCKETS // 2
    max_exact = nb // 2
    ret = jnp.where(rel > 0, nb, 0)
    n = jnp.abs(rel)
    nf = jnp.maximum(n, 1).astype(jnp.float32)
    large = max_exact + (jnp.log(nf / max_exact) / math.log(MAX_DISTANCE / max_exact)
                         * (nb - max_exact)).astype(jnp.int32)
    large = jnp.minimum(large, nb - 1)
    return ret + jnp.where(n < max_exact, n, large)


def _bias_tiles(rel_bias, tq, tk):
    d = jnp.arange(-2, 2, dtype=jnp.int32)[:, None, None]
    r = jnp.arange(tq, dtype=jnp.int32)[None, :, None]
    c = jnp.arange(tk, dtype=jnp.int32)[None, None, :]
    tiles = rel_bias.astype(jnp.float32)[_rel_bucket(d * tk + c - r)]
    return jnp.transpose(tiles, (0, 3, 1, 2))


def _attn_b_kernel(q_ref, k_ref, v_ref, tri_ref, o_ref, *, tq, tk, q_off):
    i = pl.program_id(2)
    qpos0 = q_off + i * tq
    q = q_ref[0]
    row = qpos0 + lax.broadcasted_iota(jnp.int32, (tq, tk), 0)
    lane = lax.broadcasted_iota(jnp.int32, (tq, tk), 1)
    tri = tri_ref[...]

    def cond(carry):
        j, run, _ = carry
        return (j >= 0) & (jnp.max(run) > EXP_ZERO_BELOW)

    def body(carry):
        j, run, acc = carry
        ds = pl.ds(pl.multiple_of(j * tk, tk), tk)
        z = lax.dot_general(q, k_ref[0, ds, :], _NT, preferred_element_type=jnp.float32)
        causal = (j * tk + lane) < row
        tail = jnp.log1p(jnp.exp(-jnp.abs(z)))
        log_1m = jnp.where(causal, -jnp.maximum(z, 0.0) - tail, 0.0)
        hi = log_1m.astype(jnp.bfloat16)
        lo = (log_1m - hi.astype(jnp.float32)).astype(jnp.bfloat16)
        within = (jnp.dot(hi, tri, preferred_element_type=jnp.float32)
                  + jnp.dot(lo, tri, preferred_element_type=jnp.float32))
        a = jnp.where(causal, jnp.exp(jnp.minimum(z, 0.0) - tail + within + run), 0.0)
        acc = acc + jnp.dot(a.astype(jnp.bfloat16), v_ref[0, ds, :],
                            preferred_element_type=jnp.float32)
        run = run + within[:, 0:1] + log_1m[:, 0:1]
        return j - 1, run, acc

    j0 = (qpos0 + tq - 2) // tk
    run0 = jnp.zeros((tq, 1), jnp.float32)
    acc0 = jnp.zeros((tq, HEAD_DIM), jnp.float32)
    _, _, acc = lax.while_loop(cond, body, (j0, run0, acc0))
    o_ref[0] = acc.astype(o_ref.dtype)


def _attn_b(q, k, v, *, tq, tk, q_off):
    b, tqp, _ = q.shape
    lp = k.shape[1]
    tri = (jnp.arange(tk)[:, None] > jnp.arange(tk)[None, :]).astype(jnp.bfloat16)
    kern = functools.partial(_attn_b_kernel, tq=tq, tk=tk, q_off=q_off)
    return pl.pallas_call(
        kern, grid=(b, N_HEADS, tqp // tq),
        in_specs=[
            pl.BlockSpec((1, tq, HEAD_DIM), lambda bi, h, i: (bi, i, h)),
            pl.BlockSpec((1, lp, HEAD_DIM), lambda bi, h, i: (bi, 0, h)),
            pl.BlockSpec((1, lp, HEAD_DIM), lambda bi, h, i: (bi, 0, h)),
            _const_spec((tk, tk)),
        ],
        out_specs=pl.BlockSpec((1, tq, HEAD_DIM), lambda bi, h, i: (bi, i, h)),
        out_shape=jax.ShapeDtypeStruct((b, tqp, HD), jnp.bfloat16),
        compiler_params=pltpu.CompilerParams(
            dimension_semantics=("parallel", "parallel", "parallel"),
            vmem_limit_bytes=VMEM_LIMIT),
        name="stick_breaking_attention",
    )(q, k, v, tri)


def _post_kernel(x_ref, o_ref, left_ref, wo_ref, g1_ref, b1_ref, wg_ref, wu_ref, cw_ref, cb_ref,
                 wd_ref, g2_ref, b2_ref, y_ref, conv_ref, carry_ref,
                 *, tm, d_ff, alpha, t_last, r_last):
    t = pl.program_id(1)

    @pl.when(t == 0)
    def _init():
        carry_ref[...] = left_ref[0]

    mix = jnp.dot(o_ref[0], wo_ref[...], preferred_element_type=jnp.float32)
    x1 = _layer_norm(alpha * x_ref[0] + mix, g1_ref[...], b1_ref[...])
    xb = x1.astype(jnp.bfloat16)
    row = lax.broadcasted_iota(jnp.int32, (tm, FF_CHUNK), 0)
    acc = jnp.zeros((tm, D_MODEL), jnp.float32)
    for c in range(d_ff // FF_CHUNK):
        cs = slice(c * FF_CHUNK, (c + 1) * FF_CHUNK)
        g = jnp.dot(xb, wg_ref[:, cs], preferred_element_type=jnp.float32)
        u = jnp.dot(xb, wu_ref[:, cs], preferred_element_type=jnp.float32)
        prev = carry_ref[:, cs]
        g1 = jnp.where(row == 0, prev[7:8], pltpu.roll(g, 1, 0))
        g2 = jnp.where(row == 0, prev[6:7], jnp.where(row == 1, prev[7:8], pltpu.roll(g, 2, 0)))
        gc = cb_ref[:, cs] + cw_ref[0:1, cs] * g2
        gc = gc + cw_ref[1:2, cs] * g1
        gc = gc + cw_ref[2:3, cs] * g
        hid = jax.nn.gelu(gc) * u
        acc = acc + jnp.dot(hid.astype(jnp.bfloat16), wd_ref[cs, :],
                            preferred_element_type=jnp.float32)
        carry_ref[:, cs] = g[tm - SUBLANES:tm]

        @pl.when(t == t_last)
        def _emit():
            conv_ref[0, :, cs] = g[r_last:r_last + SUBLANES]

    y_ref[0] = _layer_norm(alpha * x1 + acc, g2_ref[...], b2_ref[...])


def _post(x, o, left8, t_valid, tm, w, alpha):
    b, tp, _ = x.shape
    d_ff = w["wg"].shape[1]
    assert d_ff % FF_CHUNK == 0 and t_valid % SUBLANES == 0 and t_valid >= SUBLANES
    t_last = (t_valid - 1) // tm
    r_last = (t_valid - SUBLANES) % tm
    row = lambda width: pl.BlockSpec((1, tm, width), lambda i, j: (i, j, 0))
    vec = lambda a: a.reshape(1, -1).astype(jnp.float32)
    consts = [w["wo"], vec(w["g1"]), vec(w["b1"]), w["wg"], w["wu"], w["cw"].astype(jnp.float32),
              vec(w["cb"]), w["wd"], vec(w["g2"]), vec(w["b2"])]
    kern = functools.partial(_post_kernel, tm=tm, d_ff=d_ff, alpha=alpha,
                             t_last=t_last, r_last=r_last)
    return pl.pallas_call(
        kern, grid=(b, tp // tm),
        in_specs=[row(D_MODEL), row(HD), pl.BlockSpec((1, SUBLANES, d_ff), lambda i, j: (i, 0, 0))]
                 + [_const_spec(c.shape) for c in consts],
        out_specs=[row(D_MODEL), pl.BlockSpec((1, SUBLANES, d_ff), lambda i, j: (i, 0, 0))],
        out_shape=[jax.ShapeDtypeStruct((b, tp, D_MODEL), jnp.float32),
                   jax.ShapeDtypeStruct((b, SUBLANES, d_ff), jnp.float32)],
        scratch_shapes=[pltpu.VMEM((SUBLANES, d_ff), jnp.float32)],
        compiler_params=pltpu.CompilerParams(
            dimension_semantics=("parallel", "arbitrary"), vmem_limit_bytes=VMEM_LIMIT),
        name="outproj_ln_convffn_ln",
    )(x, o, left8, *consts)


def _split_a(w_in):
    bf = jnp.bfloat16
    off_qi = 3 * HD
    off_ki = off_qi + IDX_QW
    off_wi = off_ki + IDX_DIM
    wkw = jnp.pad(w_in[:, off_ki:], ((0, 0), (0, LANES - IDX_DIM - IDX_HEADS)))
    col_scale = jnp.concatenate([
        jnp.ones((IDX_DIM,), jnp.float32),
        jnp.full((IDX_HEADS,), IDX_QW ** -0.5, jnp.float32),
        jnp.zeros((LANES - IDX_DIM - IDX_HEADS,), jnp.float32)]).reshape(1, LANES)
    return [w_in[:, :HD].astype(bf), w_in[:, HD:2 * HD].astype(bf), w_in[:, 2 * HD:3 * HD].astype(bf),
            w_in[:, off_qi:off_ki].astype(bf), wkw.astype(bf), col_scale]


def _split_b(w_in):
    bf = jnp.bfloat16
    return [w_in[:, :HD].astype(bf), w_in[:, HD:2 * HD].astype(bf), w_in[:, 2 * HD:].astype(bf)]


def _with_cache(cache, new_bf, lp):
    b, p = cache.shape[:2]
    flat = cache.reshape(b, p, -1).astype(jnp.bfloat16)
    t = new_bf.shape[1]
    return jnp.pad(jnp.concatenate([flat, new_bf], axis=1), ((0, 0), (0, lp - p - t), (0, 0)))


def kernel(x_prompt, x_sample, cache_a_k, cache_a_v, cache_a_idx_k, cache_b_k, cache_b_v,
           state_ffn_conv, meta_tokens, rel_bias, w_a_in, w_a_out, w_b_in, w_b_out,
           ln1_g, ln1_b, ln2_g, ln2_b, w_ffn_gate, w_ffn_up, ffn_conv_w, ffn_conv_b, w_ffn_down):
    bp, seq, _ = x_prompt.shape
    bs, ts, _ = x_sample.shape
    depth = ln1_g.shape[0]
    d_ff = w_ffn_gate.shape[2]
    past = cache_a_k.shape[2]
    alpha = (2 * depth) ** 0.25
    t_p = N_META + seq
    tp_pad = _round_up(t_p, ROW_TILE)
    ls = past + ts
    ls_pad = _round_up(ls, KEY_TILE)
    assert past % KEY_TILE == 0 and ts % SUBLANES == 0 and ts <= KEY_TILE
    topk_p = min(TOPK_MAX, seq // 4)
    topk_s = min(TOPK_MAX, ls // 4)
    bf = jnp.bfloat16

    meta = jnp.broadcast_to(meta_tokens.astype(x_prompt.dtype)[None], (bp, N_META, D_MODEL))
    xp = jnp.pad(jnp.concatenate([meta, x_prompt], axis=1), ((0, 0), (0, tp_pad - t_p), (0, 0)))
    xs = x_sample
    bias_p = _bias_tiles(rel_bias, ROW_TILE, KEY_TILE)
    bias_s = _bias_tiles(rel_bias, ts, KEY_TILE)
    left_p = jnp.zeros((bp, SUBLANES, d_ff), jnp.float32)

    heads = lambda a: a.reshape(a.shape[0], a.shape[1], N_HEADS, HEAD_DIM)
    outs = {k: [] for k in ("akp", "avp", "aip", "aks", "avs", "ais",
                            "bkp", "bvp", "bks", "bvs", "cp", "cs")}
    for i in range(depth):
        j = i // N_MIXERS
        if i % N_MIXERS == 0:
            wts = _split_a(w_a_in[j])
            q, kc, kb, vc, vb, qi, kic, kib, kw = _project(xp, t_p, ROW_TILE, wts, True)
            op = _attn_a(q, qi, kw, kb, vb, kib, bias_p, tq=ROW_TILE, tk=KEY_TILE, q_off=0,
                         n_lead=N_META, l_valid=t_p, topk=topk_p)
            outs["akp"].append(heads(kc)); outs["avp"].append(heads(vc)); outs["aip"].append(kic)
            q, kc, kb, vc, vb, qi, kic, kib, kw = _project(xs, ts, ts, wts, True)
            os_ = _attn_a(q, qi, kw, _with_cache(cache_a_k[j], kb, ls_pad),
                          _with_cache(cache_a_v[j], vb, ls_pad),
                          _with_cache(cache_a_idx_k[j], kib, ls_pad), bias_s,
                          tq=ts, tk=KEY_TILE, q_off=past, n_lead=0, l_valid=ls, topk=topk_s)
            outs["aks"].append(heads(kc)); outs["avs"].append(heads(vc)); outs["ais"].append(kic)
            w_out = w_a_out[j]
        else:
            wts = _split_b(w_b_in[j])
            q, kc, kb, vc, vb = _project(xp, t_p, ROW_TILE, wts, False)
            op = _attn_b(q, kb, vb, tq=ROW_TILE, tk=KEY_TILE, q_off=0)
            outs["bkp"].append(heads(kc)); outs["bvp"].append(heads(vc))
            q, kc, kb, vc, vb = _project(xs, ts, ts, wts, False)
            os_ = _attn_b(q, _with_cache(cache_b_k[j], kb, ls_pad),
                          _with_cache(cache_b_v[j], vb, ls_pad), tq=ts, tk=KEY_TILE, q_off=past)
            outs["bks"].append(heads(kc)); outs["bvs"].append(heads(vc))
            w_out = w_b_out[j]
        w = dict(wo=w_out.astype(bf), g1=ln1_g[i], b1=ln1_b[i], wg=w_ffn_gate[i].astype(bf),
                 wu=w_ffn_up[i].astype(bf), cw=ffn_conv_w[i], cb=ffn_conv_b[i],
                 wd=w_ffn_down[i].astype(bf), g2=ln2_g[i], b2=ln2_b[i])
        xp, conv_p = _post(xp, op, left_p, t_p, ROW_TILE, w, alpha)
        left_s = jnp.pad(state_ffn_conv[i].astype(jnp.float32),
                         ((0, 0), (SUBLANES - (CONV_W - 1), 0), (0, 0)))
        xs, conv_s = _post(xs, os_, left_s, ts, ts, w, alpha)
        outs["cp"].append(conv_p[:, SUBLANES - (CONV_W - 1):])
        outs["cs"].append(conv_s[:, SUBLANES - (CONV_W - 1):])

    st = jnp.stack
    return (xp[:, N_META:t_p], xs,
            st(outs["akp"]), st(outs["avp"]), st(outs["aip"]),
            st(outs["bkp"]), st(outs["bvp"]), st(outs["cp"]),
            st(outs["aks"]), st(outs["avs"]), st(outs["ais"]),
            st(outs["bks"]), st(outs["bvs"]), st(outs["cs"]))
```

```python
import functools
import math

import jax
import jax.numpy as jnp
from jax import lax
from jax.experimental import pallas as pl
from jax.experimental.pallas import tpu as pltpu

D_MODEL = 1024
N_HEADS = 8
HEAD_DIM = 128
HD = N_HEADS * HEAD_DIM
IDX_HEADS = 8
IDX_DIM = 64
IDX_QW = IDX_HEADS * IDX_DIM
CHUNK = 64
CHUNK_SHIFT = 6
N_META = 16
N_MIXERS = 2
TOPK_MAX = 256
N_BUCKETS = 32
MAX_DISTANCE = 128
CONV_W = 3
LN_EPS = 1e-5

LANES = 128
SUBLANES = 8
ROW_TILE = 256
KEY_TILE = 256
FF_CHUNK = 256
ATTN_GROUP = 4
HEADS_PER_STEP = 2
SAMPLE_TILE = 128
VMEM_LIMIT = 56 * 1024 * 1024

NEG_BIG = -1e30
INT_MIN = -2 ** 31
KEY_NEG_INF = -2139095041
EXP_ZERO_BELOW = -104.0

_NT = (((1,), (1,)), ((), ()))


def _round_up(n, m):
    return -(-n // m) * m


def _layer_norm(x, g, b):
    mu = jnp.mean(x, -1, keepdims=True)
    xc = x - mu
    var = jnp.mean(xc * xc, -1, keepdims=True)
    return xc * lax.rsqrt(var + LN_EPS) * g + b


def _proj_a_kernel(x_ref, wqt_ref, wk_ref, wv_ref, wvt_ref, wqit_ref, wkw_ref, wkwt_ref, rs_ref,
                   qt_ref, kc_ref, kb_ref, vc_ref, vt_ref, qit_ref, kic_ref, kib_ref, kwt_ref):
    xb = x_ref[0].astype(jnp.bfloat16)

    def cols(wt_ref):
        return lax.dot_general(wt_ref[...], xb, _NT, preferred_element_type=jnp.float32)

    qt_ref[0] = (cols(wqt_ref) * (HEAD_DIM ** -0.5)).astype(jnp.bfloat16)
    k = jnp.dot(xb, wk_ref[...], preferred_element_type=jnp.float32)
    kc_ref[0] = k
    kb_ref[0] = k.astype(jnp.bfloat16)
    vc_ref[0] = jnp.dot(xb, wv_ref[...], preferred_element_type=jnp.float32)
    vt_ref[0] = cols(wvt_ref).astype(jnp.bfloat16)
    qit_ref[0] = cols(wqit_ref).astype(jnp.bfloat16)
    kw = jnp.dot(xb, wkw_ref[...], preferred_element_type=jnp.float32)
    kic_ref[0] = kw[:, :IDX_DIM]
    kib_ref[0] = kw[:, :IDX_DIM].astype(jnp.bfloat16)
    kwt_ref[0] = cols(wkwt_ref) * rs_ref[...]


def _proj_b_kernel(x_ref, wq_ref, wk_ref, wv_ref, q_ref, kc_ref, kb_ref, vc_ref, vb_ref):
    xb = x_ref[0].astype(jnp.bfloat16)
    q = jnp.dot(xb, wq_ref[...], preferred_element_type=jnp.float32)
    q_ref[0] = (q * (HEAD_DIM ** -0.5)).astype(jnp.bfloat16)
    k = jnp.dot(xb, wk_ref[...], preferred_element_type=jnp.float32)
    kc_ref[0] = k
    kb_ref[0] = k.astype(jnp.bfloat16)
    v = jnp.dot(xb, wv_ref[...], preferred_element_type=jnp.float32)
    vc_ref[0] = v
    vb_ref[0] = v.astype(jnp.bfloat16)


def _const_spec(shape):
    return pl.BlockSpec(shape, lambda *_: (0,) * len(shape))


def _project(x, t_valid, tm, weights, is_a):
    b, tp, _ = x.shape
    grid = (b, tp // tm)
    row = lambda w: pl.BlockSpec((1, tm, w), lambda i, j: (i, j, 0))
    bf = jnp.bfloat16
    f32 = jnp.float32
    pad = lambda w, dt: jax.ShapeDtypeStruct((b, tp, w), dt)
    exact = lambda w, dt: jax.ShapeDtypeStruct((b, t_valid, w), dt)
    in_specs = [row(D_MODEL)] + [_const_spec(w.shape) for w in weights]
    if is_a:
        col = lambda w: pl.BlockSpec((1, w, tm), lambda i, j: (i, 0, j))
        padt = lambda w, dt: jax.ShapeDtypeStruct((b, w, tp), dt)
        out_shape = [padt(HD, bf), exact(HD, f32), pad(HD, bf), exact(HD, f32), padt(HD, bf),
                     padt(IDX_QW, bf), exact(IDX_DIM, f32), pad(IDX_DIM, bf), padt(LANES, f32)]
        out_specs = [col(HD), row(HD), row(HD), row(HD), col(HD),
                     col(IDX_QW), row(IDX_DIM), row(IDX_DIM), col(LANES)]
        body = _proj_a_kernel
    else:
        out_shape = [pad(HD, bf), exact(HD, f32), pad(HD, bf), exact(HD, f32), pad(HD, bf)]
        out_specs = [row(HD)] * 5
        body = _proj_b_kernel
    return pl.pallas_call(
        body, grid=grid, in_specs=in_specs, out_specs=out_specs, out_shape=out_shape,
        compiler_params=pltpu.CompilerParams(
            dimension_semantics=("parallel", "parallel"), vmem_limit_bytes=VMEM_LIMIT),
        name="proj_a" if is_a else "proj_b",
    )(x, *weights)


def _vis_end(qpos, n_lead, l_valid):
    ve = jnp.where(qpos < n_lead, n_lead,
                   n_lead + ((((qpos - n_lead) >> CHUNK_SHIFT) + 1) << CHUNK_SHIFT))
    return jnp.minimum(ve, l_valid)


def _attn_a_kernel(qt_ref, qit_ref, kwt_ref, k_ref, vt_ref, kib_ref, bias_ref, o_ref,
                   keys_ref, madd_ref, thr_ref, need_ref, cut_ref, s0_ref, s1_ref,
                   *, tq, tk, q_off, n_lead, l_valid, topk, idx_bits):
    i = pl.program_id(1)
    h = pl.program_id(2)
    qpos0 = q_off + i * tq
    nkb = (_vis_end(qpos0 + tq - 1, n_lead, l_valid) + tk - 1) // tk
    g = tk // SUBLANES
    wide = ATTN_GROUP * tk
    n_trips = (nkb + ATTN_GROUP - 1) // ATTN_GROUP

    def blk_rows(j):
        return pl.ds(pl.multiple_of(j * g, g), g)

    @pl.when(h == 0)
    def _select():
        qpos = qpos0 + lax.broadcasted_iota(jnp.int32, (1, tq), 1)
        vis_end = _vis_end(qpos, n_lead, l_valid)

        def score_block(j, carry):
            kib = kib_ref[0, pl.ds(pl.multiple_of(j * tk, tk), tk), :]
            sc = jnp.zeros((tk, tq), jnp.float32)
            for ih in range(IDX_HEADS):
                s = jnp.dot(kib, qit_ref[0, ih * IDX_DIM:(ih + 1) * IDX_DIM, :],
                            preferred_element_type=jnp.float32)
                sc = sc + jnp.maximum(s, 0.0) * kwt_ref[0, IDX_DIM + ih:IDX_DIM + ih + 1, :]
            sc = jnp.where(sc == 0.0, 0.0, sc)
            bits = pltpu.bitcast(sc, jnp.int32)
            key = jnp.where(bits < 0, bits ^ 0x7FFFFFFF, bits)
            pos = j * tk + lax.broadcasted_iota(jnp.int32, (tk, tq), 0)
            key = jnp.where(pos < vis_end, key, KEY_NEG_INF)
            keys_ref[blk_rows(j)] = key.reshape(g, SUBLANES, tq)
            return carry

        lax.fori_loop(0, nkb, score_block, 0)

        def key_pos(j):
            shape = (g, SUBLANES, tq)
            return (j * tk + lax.broadcasted_iota(jnp.int32, shape, 0) * SUBLANES
                    + lax.broadcasted_iota(jnp.int32, shape, 1))

        def count(pred):
            def blk(j, cnt):
                hit = pred(keys_ref[blk_rows(j)], j).astype(jnp.int32)
                part = jnp.sum(hit.reshape(4, g // 4, SUBLANES, tq), axis=1)
                return cnt + jnp.sum(part, axis=0)
            cnt = lax.fori_loop(0, nkb, blk, jnp.zeros((SUBLANES, tq), jnp.int32))
            return jnp.broadcast_to(jnp.sum(cnt, axis=0, keepdims=True), (SUBLANES, tq))

        def thr_step(b, ut):
            cand = (ut | (jnp.int32(1) << (31 - b))) ^ INT_MIN
            cnt = count(lambda kk, j: kk >= cand)
            return jnp.where(cnt >= topk, cand ^ INT_MIN, ut)

        thr = lax.fori_loop(0, 32, thr_step, jnp.zeros((SUBLANES, tq), jnp.int32)) ^ INT_MIN
        cnt_gt = count(lambda kk, j: kk > thr)
        cnt_ge = count(lambda kk, j: kk >= thr)
        thr_ref[...] = thr
        need_ref[...] = topk - cnt_gt
        cut_ref[...] = jnp.full((SUBLANES, tq), 1 << idx_bits, jnp.int32)
        excess = (cnt_ge > topk) & (thr != KEY_NEG_INF)

        @pl.when(jnp.max(excess.astype(jnp.int32)) > 0)
        def _ties():
            need = need_ref[...]

            def cut_step(b, cut):
                cand = cut | (jnp.int32(1) << (idx_bits - 1 - b))
                cnt = count(lambda kk, j: (kk == thr) & (key_pos(j) < cand))
                return jnp.where(cnt <= need, cand, cut)

            cut_ref[...] = lax.fori_loop(0, idx_bits, cut_step,
                                         jnp.zeros((SUBLANES, tq), jnp.int32))

        def mask_block(j, carry):
            kk = keys_ref[blk_rows(j)]
            thr_v = thr_ref[...]
            sel = (((kk > thr_v) | ((kk == thr_v) & (key_pos(j) < cut_ref[...])))
                   & (kk != KEY_NEG_INF))
            madd_ref[blk_rows(j)] = jnp.where(sel, 0.0, NEG_BIG)
            return carry

        lax.fori_loop(0, nkb, mask_block, 0)

        def mask_tail(j, carry):
            madd_ref[blk_rows(j)] = jnp.full((g, SUBLANES, tq), NEG_BIG, jnp.float32)
            return carry

        lax.fori_loop(nkb, n_trips * ATTN_GROUP, mask_tail, 0)

    jq = qpos0 // tk
    s_refs = (s0_ref, s1_ref)
    heads = [slice(hh * HEAD_DIM, (hh + 1) * HEAD_DIM) for hh in range(HEADS_PER_STEP)]

    def keys_of(t):
        return pl.ds(pl.multiple_of(t * wide, wide), wide)

    def scores(hh, t):
        s_refs[hh][...] = jnp.dot(k_ref[0, keys_of(t), heads[hh]], qt_ref[0, heads[hh], :],
                                  preferred_element_type=jnp.float32)

    def softmax_pv(hh, t, carry):
        m, l, acc = carry
        parts = []
        for u in range(ATTN_GROUP):
            j = t * ATTN_GROUP + u
            d = jnp.clip(j - jq, -2, 1) + 2
            parts.append(s_refs[hh][u * tk:(u + 1) * tk] + bias_ref[d, hh]
                         + madd_ref[blk_rows(j)].reshape(tk, tq))
        m_new = m
        for part in parts:
            m_new = jnp.maximum(m_new, jnp.max(part, axis=0, keepdims=True))
        alpha = jnp.exp(m - m_new)
        p = [jnp.exp(part - m_new) for part in parts]
        l = alpha * l + sum(jnp.sum(pu, axis=0, keepdims=True) for pu in p)
        pb = jnp.concatenate([pu.astype(jnp.bfloat16) for pu in p], axis=0)
        acc = alpha * acc + jnp.dot(vt_ref[0, heads[hh], keys_of(t)], pb,
                                    preferred_element_type=jnp.float32)
        return m_new, l, acc

    def trip(t, carries):
        scores(1, t)
        c0 = softmax_pv(0, t, carries[0])
        scores(0, jnp.minimum(t + 1, n_trips - 1))
        c1 = softmax_pv(1, t, carries[1])
        return c0, c1

    scores(0, 0)
    carries = tuple((jnp.full((1, tq), NEG_BIG, jnp.float32), jnp.zeros((1, tq), jnp.float32),
                     jnp.zeros((HEAD_DIM, tq), jnp.float32)) for _ in range(2))
    carries = lax.fori_loop(0, n_trips, trip, carries)
    for hh, (_, l, acc) in enumerate(carries):
        o_ref[0, :, heads[hh]] = (acc / l).T.astype(o_ref.dtype)


def _attn_a(qt, qit, kwt, k, vt, kib, bias, *, tq, tk, q_off, n_lead, l_valid, topk):
    b, _, tqp = qt.shape
    lp = k.shape[1]
    assert lp % (ATTN_GROUP * tk) == 0 and tk % (4 * SUBLANES) == 0 and tk >= topk
    assert tqp % tq == 0 and q_off % tk == 0 and HEADS_PER_STEP == 2
    idx_bits = lp.bit_length()
    grid = (b, tqp // tq, N_HEADS // HEADS_PER_STEP)
    hw = HEADS_PER_STEP * HEAD_DIM
    kern = functools.partial(_attn_a_kernel, tq=tq, tk=tk, q_off=q_off, n_lead=n_lead,
                             l_valid=l_valid, topk=topk, idx_bits=idx_bits)
    return pl.pallas_call(
        kern, grid=grid,
        in_specs=[
            pl.BlockSpec((1, hw, tq), lambda bi, i, h: (bi, h, i)),
            pl.BlockSpec((1, IDX_QW, tq), lambda bi, i, h: (bi, 0, i)),
            pl.BlockSpec((1, LANES, tq), lambda bi, i, h: (bi, 0, i)),
            pl.BlockSpec((1, lp, hw), lambda bi, i, h: (bi, 0, h)),
            pl.BlockSpec((1, hw, lp), lambda bi, i, h: (bi, h, 0)),
            pl.BlockSpec((1, lp, IDX_DIM), lambda bi, i, h: (bi, 0, 0)),
            pl.BlockSpec((4, HEADS_PER_STEP, tk, tq), lambda bi, i, h: (0, h, 0, 0)),
        ],
        out_specs=pl.BlockSpec((1, tq, hw), lambda bi, i, h: (bi, i, h)),
        out_shape=jax.ShapeDtypeStruct((b, tqp, HD), jnp.bfloat16),
        scratch_shapes=[
            pltpu.VMEM((lp // SUBLANES, SUBLANES, tq), jnp.int32),
            pltpu.VMEM((lp // SUBLANES, SUBLANES, tq), jnp.float32),
            pltpu.VMEM((SUBLANES, tq), jnp.int32),
            pltpu.VMEM((SUBLANES, tq), jnp.int32),
            pltpu.VMEM((SUBLANES, tq), jnp.int32),
            pltpu.VMEM((ATTN_GROUP * tk, tq), jnp.float32),
            pltpu.VMEM((ATTN_GROUP * tk, tq), jnp.float32),
        ],
        compiler_params=pltpu.CompilerParams(
            dimension_semantics=("parallel", "parallel", "arbitrary"),
            vmem_limit_bytes=VMEM_LIMIT),
        name="dsa_attention",
    )(qt, qit, kwt, k, vt, kib, bias)


def _rel_bucket(rel):
    nb = N_BUCKETS // 2
    max_exact = nb // 2
    ret = jnp.where(rel > 0, nb, 0)
    n = jnp.abs(rel)
    nf = jnp.maximum(n, 1).astype(jnp.float32)
    large = max_exact + (jnp.log(nf / max_exact) / math.log(MAX_DISTANCE / max_exact)
                         * (nb - max_exact)).astype(jnp.int32)
    large = jnp.minimum(large, nb - 1)
    return ret + jnp.where(n < max_exact, n, large)


def _bias_tiles(rel_bias, tq, tk):
    d = jnp.arange(-2, 2, dtype=jnp.int32)[:, None, None]
    c = jnp.arange(tk, dtype=jnp.int32)[None, :, None]
    r = jnp.arange(tq, dtype=jnp.int32)[None, None, :]
    tiles = rel_bias.astype(jnp.float32)[_rel_bucket(d * tk + c - r)]
    return jnp.transpose(tiles, (0, 3, 1, 2))


def _attn_b_kernel(q_ref, k_ref, v_ref, tri_ref, o_ref, *, tq, tk, q_off):
    i = pl.program_id(2)
    qpos0 = q_off + i * tq
    q = q_ref[0]
    row = qpos0 + lax.broadcasted_iota(jnp.int32, (tq, tk), 0)
    lane = lax.broadcasted_iota(jnp.int32, (tq, tk), 1)
    tri = tri_ref[...]

    def cond(carry):
        j, run, _ = carry
        return (j >= 0) & (jnp.max(run) > EXP_ZERO_BELOW)

    def body(carry):
        j, run, acc = carry
        ds = pl.ds(pl.multiple_of(j * tk, tk), tk)
        z = lax.dot_general(q, k_ref[0, ds, :], _NT, preferred_element_type=jnp.float32)
        causal = (j * tk + lane) < row
        tail = jnp.log1p(jnp.exp(-jnp.abs(z)))
        log_1m = jnp.where(causal, -jnp.maximum(z, 0.0) - tail, 0.0)
        hi = log_1m.astype(jnp.bfloat16)
        lo = (log_1m - hi.astype(jnp.float32)).astype(jnp.bfloat16)
        within = (jnp.dot(hi, tri, preferred_element_type=jnp.float32)
                  + jnp.dot(lo, tri, preferred_element_type=jnp.float32))
        a = jnp.where(causal, jnp.exp(jnp.minimum(z, 0.0) - tail + within + run), 0.0)
        acc = acc + jnp.dot(a.astype(jnp.bfloat16), v_ref[0, ds, :],
                            preferred_element_type=jnp.float32)
        run = run + within[:, 0:1] + log_1m[:, 0:1]
        return j - 1, run, acc

    j0 = (qpos0 + tq - 2) // tk
    run0 = jnp.zeros((tq, 1), jnp.float32)
    acc0 = jnp.zeros((tq, HEAD_DIM), jnp.float32)
    _, _, acc = lax.while_loop(cond, body, (j0, run0, acc0))
    o_ref[0] = acc.astype(o_ref.dtype)


def _attn_b(q, k, v, *, tq, tk, q_off):
    b, tqp, _ = q.shape
    lp = k.shape[1]
    tri = (jnp.arange(tk)[:, None] > jnp.arange(tk)[None, :]).astype(jnp.bfloat16)
    kern = functools.partial(_attn_b_kernel, tq=tq, tk=tk, q_off=q_off)
    return pl.pallas_call(
        kern, grid=(b, N_HEADS, tqp // tq),
        in_specs=[
            pl.BlockSpec((1, tq, HEAD_DIM), lambda bi, h, i: (bi, i, h)),
            pl.BlockSpec((1, lp, HEAD_DIM), lambda bi, h, i: (bi, 0, h)),
            pl.BlockSpec((1, lp, HEAD_DIM), lambda bi, h, i: (bi, 0, h)),
            _const_spec((tk, tk)),
        ],
        out_specs=pl.BlockSpec((1, tq, HEAD_DIM), lambda bi, h, i: (bi, i, h)),
        out_shape=jax.ShapeDtypeStruct((b, tqp, HD), jnp.bfloat16),
        compiler_params=pltpu.CompilerParams(
            dimension_semantics=("parallel", "parallel", "parallel"),
            vmem_limit_bytes=VMEM_LIMIT),
        name="stick_breaking_attention",
    )(q, k, v, tri)


def _post_kernel(x_ref, o_ref, left_ref, wo_ref, g1_ref, b1_ref, wg_ref, wu_ref, cw_ref, cb_ref,
                 wd_ref, g2_ref, b2_ref, y_ref, conv_ref, carry_ref,
                 *, tm, d_ff, alpha, t_last, r_last):
    t = pl.program_id(1)

    @pl.when(t == 0)
    def _init():
        carry_ref[...] = left_ref[0]

    mix = jnp.dot(o_ref[0], wo_ref[...], preferred_element_type=jnp.float32)
    x1 = _layer_norm(alpha * x_ref[0] + mix, g1_ref[...], b1_ref[...])
    xb = x1.astype(jnp.bfloat16)
    row = lax.broadcasted_iota(jnp.int32, (tm, FF_CHUNK), 0)
    acc = jnp.zeros((tm, D_MODEL), jnp.float32)
    for c in range(d_ff // FF_CHUNK):
        cs = slice(c * FF_CHUNK, (c + 1) * FF_CHUNK)
        g = jnp.dot(xb, wg_ref[:, cs], preferred_element_type=jnp.float32)
        u = jnp.dot(xb, wu_ref[:, cs], preferred_element_type=jnp.float32)
        prev = carry_ref[:, cs]
        g1 = jnp.where(row == 0, prev[7:8], pltpu.roll(g, 1, 0))
        g2 = jnp.where(row == 0, prev[6:7], jnp.where(row == 1, prev[7:8], pltpu.roll(g, 2, 0)))
        gc = cb_ref[:, cs] + cw_ref[0:1, cs] * g2
        gc = gc + cw_ref[1:2, cs] * g1
        gc = gc + cw_ref[2:3, cs] * g
        hid = jax.nn.gelu(gc) * u
        acc = acc + jnp.dot(hid.astype(jnp.bfloat16), wd_ref[cs, :],
                            preferred_element_type=jnp.float32)
        carry_ref[:, cs] = g[tm - SUBLANES:tm]

        @pl.when(t == t_last)
        def _emit():
            conv_ref[0, :, cs] = g[r_last:r_last + SUBLANES]

    y_ref[0] = _layer_norm(alpha * x1 + acc, g2_ref[...], b2_ref[...])


def _post(x, o, left8, t_valid, tm, w, alpha):
    b, tp, _ = x.shape
    d_ff = w["wg"].shape[1]
    assert d_ff % FF_CHUNK == 0 and t_valid % SUBLANES == 0 and t_valid >= SUBLANES
    t_last = (t_valid - 1) // tm
    r_last = (t_valid - SUBLANES) % tm
    row = lambda width: pl.BlockSpec((1, tm, width), lambda i, j: (i, j, 0))
    vec = lambda a: a.reshape(1, -1).astype(jnp.float32)
    consts = [w["wo"], vec(w["g1"]), vec(w["b1"]), w["wg"], w["wu"], w["cw"].astype(jnp.float32),
              vec(w["cb"]), w["wd"], vec(w["g2"]), vec(w["b2"])]
    kern = functools.partial(_post_kernel, tm=tm, d_ff=d_ff, alpha=alpha,
                             t_last=t_last, r_last=r_last)
    return pl.pallas_call(
        kern, grid=(b, tp // tm),
        in_specs=[row(D_MODEL), row(HD), pl.BlockSpec((1, SUBLANES, d_ff), lambda i, j: (i, 0, 0))]
                 + [_const_spec(c.shape) for c in consts],
        out_specs=[row(D_MODEL), pl.BlockSpec((1, SUBLANES, d_ff), lambda i, j: (i, 0, 0))],
        out_shape=[jax.ShapeDtypeStruct((b, tp, D_MODEL), jnp.float32),
                   jax.ShapeDtypeStruct((b, SUBLANES, d_ff), jnp.float32)],
        scratch_shapes=[pltpu.VMEM((SUBLANES, d_ff), jnp.float32)],
        compiler_params=pltpu.CompilerParams(
            dimension_semantics=("parallel", "arbitrary"), vmem_limit_bytes=VMEM_LIMIT),
        name="outproj_ln_convffn_ln",
    )(x, o, left8, *consts)


def _split_a(w_in):
    bf = jnp.bfloat16
    off_qi = 3 * HD
    off_ki = off_qi + IDX_QW
    wq, wk, wv = w_in[:, :HD], w_in[:, HD:2 * HD], w_in[:, 2 * HD:3 * HD]
    wqi = w_in[:, off_qi:off_ki]
    wkw = jnp.pad(w_in[:, off_ki:], ((0, 0), (0, LANES - IDX_DIM - IDX_HEADS)))
    row_scale = jnp.concatenate([
        jnp.ones((IDX_DIM,), jnp.float32),
        jnp.full((IDX_HEADS,), IDX_QW ** -0.5, jnp.float32),
        jnp.zeros((LANES - IDX_DIM - IDX_HEADS,), jnp.float32)]).reshape(LANES, 1)
    return [wq.T.astype(bf), wk.astype(bf), wv.astype(bf), wv.T.astype(bf), wqi.T.astype(bf),
            wkw.astype(bf), wkw.T.astype(bf), row_scale]


def _split_b(w_in):
    bf = jnp.bfloat16
    return [w_in[:, :HD].astype(bf), w_in[:, HD:2 * HD].astype(bf), w_in[:, 2 * HD:].astype(bf)]


def _with_cache(cache, new_bf, lp):
    b, p = cache.shape[:2]
    flat = cache.reshape(b, p, -1).astype(jnp.bfloat16)
    t = new_bf.shape[1]
    return jnp.pad(jnp.concatenate([flat, new_bf], axis=1), ((0, 0), (0, lp - p - t), (0, 0)))


def kernel(x_prompt, x_sample, cache_a_k, cache_a_v, cache_a_idx_k, cache_b_k, cache_b_v,
           state_ffn_conv, meta_tokens, rel_bias, w_a_in, w_a_out, w_b_in, w_b_out,
           ln1_g, ln1_b, ln2_g, ln2_b, w_ffn_gate, w_ffn_up, ffn_conv_w, ffn_conv_b, w_ffn_down):
    bp, seq, _ = x_prompt.shape
    bs, ts, _ = x_sample.shape
    depth = ln1_g.shape[0]
    d_ff = w_ffn_gate.shape[2]
    past = cache_a_k.shape[2]
    alpha = (2 * depth) ** 0.25
    t_p = N_META + seq
    tp_pad = _round_up(t_p, ROW_TILE)
    ls = past + ts
    ls_pad = _round_up(ls, ATTN_GROUP * KEY_TILE)
    assert past % KEY_TILE == 0 and ts % SUBLANES == 0 and ts <= SAMPLE_TILE
    topk_p = min(TOPK_MAX, seq // 4)
    topk_s = min(TOPK_MAX, ls // 4)
    bf = jnp.bfloat16

    meta = jnp.broadcast_to(meta_tokens.astype(x_prompt.dtype)[None], (bp, N_META, D_MODEL))
    xp = jnp.pad(jnp.concatenate([meta, x_prompt], axis=1), ((0, 0), (0, tp_pad - t_p), (0, 0)))
    xs = x_sample
    bias_p = _bias_tiles(rel_bias, ROW_TILE, KEY_TILE)
    bias_s = _bias_tiles(rel_bias, SAMPLE_TILE, KEY_TILE)
    left_p = jnp.zeros((bp, SUBLANES, d_ff), jnp.float32)

    heads = lambda a: a.reshape(a.shape[0], a.shape[1], N_HEADS, HEAD_DIM)
    outs = {k: [] for k in ("akp", "avp", "aip", "aks", "avs", "ais",
                            "bkp", "bvp", "bks", "bvs", "cp", "cs")}
    for i in range(depth):
        j = i // N_MIXERS
        if i % N_MIXERS == 0:
            wts = _split_a(w_a_in[j])
            qt, kc, kb, vc, vt, qit, kic, kib, kwt = _project(xp, t_p, ROW_TILE, wts, True)
            key_pad = _round_up(tp_pad, ATTN_GROUP * KEY_TILE) - tp_pad
            op = _attn_a(qt, qit, kwt, jnp.pad(kb, ((0, 0), (0, key_pad), (0, 0))),
                         jnp.pad(vt, ((0, 0), (0, 0), (0, key_pad))),
                         jnp.pad(kib, ((0, 0), (0, key_pad), (0, 0))), bias_p,
                         tq=ROW_TILE, tk=KEY_TILE, q_off=0, n_lead=N_META, l_valid=t_p, topk=topk_p)
            outs["akp"].append(heads(kc)); outs["avp"].append(heads(vc)); outs["aip"].append(kic)
            xs_pad = jnp.pad(xs, ((0, 0), (0, SAMPLE_TILE - ts), (0, 0)))
            qt, kc, kb, vc, vt, qit, kic, kib, kwt = _project(xs_pad, ts, SAMPLE_TILE, wts, True)
            cache_vt = jnp.swapaxes(cache_a_v[j].reshape(bs, past, HD).astype(bf), 1, 2)
            vt_all = jnp.pad(jnp.concatenate([cache_vt, vt[:, :, :ts]], axis=2),
                             ((0, 0), (0, 0), (0, ls_pad - ls)))
            os_ = _attn_a(qt, qit, kwt, _with_cache(cache_a_k[j], kb[:, :ts], ls_pad), vt_all,
                          _with_cache(cache_a_idx_k[j], kib[:, :ts], ls_pad), bias_s,
                          tq=SAMPLE_TILE, tk=KEY_TILE, q_off=past, n_lead=0, l_valid=ls,
                          topk=topk_s)[:, :ts]
            outs["aks"].append(heads(kc)); outs["avs"].append(heads(vc)); outs["ais"].append(kic)
            w_out = w_a_out[j]
        else:
            wts = _split_b(w_b_in[j])
            q, kc, kb, vc, vb = _project(xp, t_p, ROW_TILE, wts, False)
            op = _attn_b(q, kb, vb, tq=ROW_TILE, tk=KEY_TILE, q_off=0)
            outs["bkp"].append(heads(kc)); outs["bvp"].append(heads(vc))
            q, kc, kb, vc, vb = _project(xs, ts, ts, wts, False)
            os_ = _attn_b(q, _with_cache(cache_b_k[j], kb, ls_pad),
                          _with_cache(cache_b_v[j], vb, ls_pad), tq=ts, tk=KEY_TILE, q_off=past)
            outs["bks"].append(heads(kc)); outs["bvs"].append(heads(vc))
            w_out = w_b_out[j]
        w = dict(wo=w_out.astype(bf), g1=ln1_g[i], b1=ln1_b[i], wg=w_ffn_gate[i].astype(bf),
                 wu=w_ffn_up[i].astype(bf), cw=ffn_conv_w[i], cb=ffn_conv_b[i],
                 wd=w_ffn_down[i].astype(bf), g2=ln2_g[i], b2=ln2_b[i])
        xp, conv_p = _post(xp, op, left_p, t_p, ROW_TILE, w, alpha)
        left_s = jnp.pad(state_ffn_conv[i].astype(jnp.float32),
                         ((0, 0), (SUBLANES - (CONV_W - 1), 0), (0, 0)))
        xs, conv_s = _post(xs, os_, left_s, ts, ts, w, alpha)
        outs["cp"].append(conv_p[:, SUBLANES - (CONV_W - 1):])
        outs["cs"].append(conv_s[:, SUBLANES - (CONV_W - 1):])

    st = jnp.stack
    return (xp[:, N_META:t_p], xs,
            st(outs["akp"]), st(outs["avp"]), st(outs["aip"]),
            st(outs["bkp"]), st(outs["bvp"]), st(outs["cp"]),
            st(outs["aks"]), st(outs["avs"]), st(outs["ais"]),
            st(outs["bks"]), st(outs["bvs"]), st(outs["cs"]))
```

```python
import functools
import math

import jax
import jax.numpy as jnp
from jax import lax
from jax.experimental import pallas as pl
from jax.experimental.pallas import tpu as pltpu

D_MODEL = 1024
N_HEADS = 8
HEAD_DIM = 128
HD = N_HEADS * HEAD_DIM
IDX_HEADS = 8
IDX_DIM = 64
IDX_QW = IDX_HEADS * IDX_DIM
CHUNK = 64
CHUNK_SHIFT = 6
N_META = 16
N_MIXERS = 2
TOPK_MAX = 256
N_BUCKETS = 32
MAX_DISTANCE = 128
CONV_W = 3
LN_EPS = 1e-5

LANES = 128
SUBLANES = 8
ROW_TILE = 256
KEY_TILE = 256
FF_CHUNK = 256
ATTN_GROUP = 4
HEADS_PER_STEP = 2
SAMPLE_TILE = 128
VMEM_LIMIT = 56 * 1024 * 1024

NEG_BIG = -1e30
INT_MIN = -2 ** 31
KEY_NEG_INF = -2139095041
EXP_ZERO_BELOW = -104.0

_NT = (((1,), (1,)), ((), ()))


def _round_up(n, m):
    return -(-n // m) * m


def _layer_norm(x, g, b):
    mu = jnp.mean(x, -1, keepdims=True)
    xc = x - mu
    var = jnp.mean(xc * xc, -1, keepdims=True)
    return xc * lax.rsqrt(var + LN_EPS) * g + b


def _proj_a_kernel(x_ref, wqt_ref, wk_ref, wv_ref, wvt_ref, wqit_ref, wkw_ref, wkwt_ref, rs_ref,
                   qt_ref, kc_ref, kb_ref, vc_ref, vt_ref, qit_ref, kic_ref, kib_ref, kwt_ref):
    xb = x_ref[0].astype(jnp.bfloat16)

    def cols(wt_ref):
        return lax.dot_general(wt_ref[...], xb, _NT, preferred_element_type=jnp.float32)

    qt_ref[0] = (cols(wqt_ref) * (HEAD_DIM ** -0.5)).astype(jnp.bfloat16)
    k = jnp.dot(xb, wk_ref[...], preferred_element_type=jnp.float32)
    kc_ref[0] = k
    kb_ref[0] = k.astype(jnp.bfloat16)
    vc_ref[0] = jnp.dot(xb, wv_ref[...], preferred_element_type=jnp.float32)
    vt_ref[0] = cols(wvt_ref).astype(jnp.bfloat16)
    qit_ref[0] = cols(wqit_ref).astype(jnp.bfloat16)
    kw = jnp.dot(xb, wkw_ref[...], preferred_element_type=jnp.float32)
    kic_ref[0] = kw[:, :IDX_DIM]
    kib_ref[0] = kw[:, :IDX_DIM].astype(jnp.bfloat16)
    kwt_ref[0] = cols(wkwt_ref) * rs_ref[...]


def _proj_b_kernel(x_ref, wq_ref, wk_ref, wv_ref, q_ref, kc_ref, kb_ref, vc_ref, vb_ref):
    xb = x_ref[0].astype(jnp.bfloat16)
    q = jnp.dot(xb, wq_ref[...], preferred_element_type=jnp.float32)
    q_ref[0] = (q * (HEAD_DIM ** -0.5)).astype(jnp.bfloat16)
    k = jnp.dot(xb, wk_ref[...], preferred_element_type=jnp.float32)
    kc_ref[0] = k
    kb_ref[0] = k.astype(jnp.bfloat16)
    v = jnp.dot(xb, wv_ref[...], preferred_element_type=jnp.float32)
    vc_ref[0] = v
    vb_ref[0] = v.astype(jnp.bfloat16)


def _const_spec(shape):
    return pl.BlockSpec(shape, lambda *_: (0,) * len(shape))


def _project(x, t_valid, tm, weights, is_a):
    b, tp, _ = x.shape
    grid = (b, tp // tm)
    row = lambda w: pl.BlockSpec((1, tm, w), lambda i, j: (i, j, 0))
    bf = jnp.bfloat16
    f32 = jnp.float32
    pad = lambda w, dt: jax.ShapeDtypeStruct((b, tp, w), dt)
    exact = lambda w, dt: jax.ShapeDtypeStruct((b, t_valid, w), dt)
    in_specs = [row(D_MODEL)] + [_const_spec(w.shape) for w in weights]
    if is_a:
        col = lambda w: pl.BlockSpec((1, w, tm), lambda i, j: (i, 0, j))
        padt = lambda w, dt: jax.ShapeDtypeStruct((b, w, tp), dt)
        out_shape = [padt(HD, bf), exact(HD, f32), pad(HD, bf), exact(HD, f32), padt(HD, bf),
                     padt(IDX_QW, bf), exact(IDX_DIM, f32), pad(IDX_DIM, bf), padt(LANES, f32)]
        out_specs = [col(HD), row(HD), row(HD), row(HD), col(HD),
                     col(IDX_QW), row(IDX_DIM), row(IDX_DIM), col(LANES)]
        body = _proj_a_kernel
    else:
        out_shape = [pad(HD, bf), exact(HD, f32), pad(HD, bf), exact(HD, f32), pad(HD, bf)]
        out_specs = [row(HD)] * 5
        body = _proj_b_kernel
    return pl.pallas_call(
        body, grid=grid, in_specs=in_specs, out_specs=out_specs, out_shape=out_shape,
        compiler_params=pltpu.CompilerParams(
            dimension_semantics=("parallel", "parallel"), vmem_limit_bytes=VMEM_LIMIT),
        name="proj_a" if is_a else "proj_b",
    )(x, *weights)


def _vis_end(qpos, n_lead, l_valid):
    ve = jnp.where(qpos < n_lead, n_lead,
                   n_lead + ((((qpos - n_lead) >> CHUNK_SHIFT) + 1) << CHUNK_SHIFT))
    return jnp.minimum(ve, l_valid)


def _attn_a_kernel(qt_ref, qit_ref, kwt_ref, k_ref, vt_ref, kib_ref, bias_ref, o_ref,
                   keys_ref, madd_ref, thr_ref, need_ref, cut_ref, s0_ref, s1_ref, p1_ref,
                   *, tq, tk, q_off, n_lead, l_valid, topk, idx_bits):
    i = pl.program_id(1)
    h = pl.program_id(2)
    qpos0 = q_off + i * tq
    nkb = (_vis_end(qpos0 + tq - 1, n_lead, l_valid) + tk - 1) // tk
    g = tk // SUBLANES
    wide = ATTN_GROUP * tk
    n_trips = (nkb + ATTN_GROUP - 1) // ATTN_GROUP

    def blk_rows(j):
        return pl.ds(pl.multiple_of(j * g, g), g)

    @pl.when(h == 0)
    def _select():
        qpos = qpos0 + lax.broadcasted_iota(jnp.int32, (1, tq), 1)
        vis_end = _vis_end(qpos, n_lead, l_valid)

        def score_block(j, carry):
            kib = kib_ref[0, pl.ds(pl.multiple_of(j * tk, tk), tk), :]
            sc = jnp.zeros((tk, tq), jnp.float32)
            for ih in range(IDX_HEADS):
                s = jnp.dot(kib, qit_ref[0, ih * IDX_DIM:(ih + 1) * IDX_DIM, :],
                            preferred_element_type=jnp.float32)
                sc = sc + jnp.maximum(s, 0.0) * kwt_ref[0, IDX_DIM + ih:IDX_DIM + ih + 1, :]
            sc = jnp.where(sc == 0.0, 0.0, sc)
            bits = pltpu.bitcast(sc, jnp.int32)
            key = jnp.where(bits < 0, bits ^ 0x7FFFFFFF, bits)
            pos = j * tk + lax.broadcasted_iota(jnp.int32, (tk, tq), 0)
            key = jnp.where(pos < vis_end, key, KEY_NEG_INF)
            keys_ref[blk_rows(j)] = key.reshape(g, SUBLANES, tq)
            return carry

        lax.fori_loop(0, nkb, score_block, 0)

        def key_pos(j):
            shape = (g, SUBLANES, tq)
            return (j * tk + lax.broadcasted_iota(jnp.int32, shape, 0) * SUBLANES
                    + lax.broadcasted_iota(jnp.int32, shape, 1))

        def count(pred):
            def blk(j, cnt):
                hit = pred(keys_ref[blk_rows(j)], j).astype(jnp.int32)
                part = jnp.sum(hit.reshape(4, g // 4, SUBLANES, tq), axis=1)
                return cnt + jnp.sum(part, axis=0)
            cnt = lax.fori_loop(0, nkb, blk, jnp.zeros((SUBLANES, tq), jnp.int32))
            return jnp.broadcast_to(jnp.sum(cnt, axis=0, keepdims=True), (SUBLANES, tq))

        def thr_step(b, ut):
            cand = (ut | (jnp.int32(1) << (31 - b))) ^ INT_MIN
            cnt = count(lambda kk, j: kk >= cand)
            return jnp.where(cnt >= topk, cand ^ INT_MIN, ut)

        thr = lax.fori_loop(0, 32, thr_step, jnp.zeros((SUBLANES, tq), jnp.int32)) ^ INT_MIN
        cnt_gt = count(lambda kk, j: kk > thr)
        cnt_ge = count(lambda kk, j: kk >= thr)
        thr_ref[...] = thr
        need_ref[...] = topk - cnt_gt
        cut_ref[...] = jnp.full((SUBLANES, tq), 1 << idx_bits, jnp.int32)
        excess = (cnt_ge > topk) & (thr != KEY_NEG_INF)

        @pl.when(jnp.max(excess.astype(jnp.int32)) > 0)
        def _ties():
            need = need_ref[...]

            def cut_step(b, cut):
                cand = cut | (jnp.int32(1) << (idx_bits - 1 - b))
                cnt = count(lambda kk, j: (kk == thr) & (key_pos(j) < cand))
                return jnp.where(cnt <= need, cand, cut)

            cut_ref[...] = lax.fori_loop(0, idx_bits, cut_step,
                                         jnp.zeros((SUBLANES, tq), jnp.int32))

        def mask_block(j, carry):
            kk = keys_ref[blk_rows(j)]
            thr_v = thr_ref[...]
            sel = (((kk > thr_v) | ((kk == thr_v) & (key_pos(j) < cut_ref[...])))
                   & (kk != KEY_NEG_INF))
            madd_ref[blk_rows(j)] = jnp.where(sel, 0.0, NEG_BIG)
            return carry

        lax.fori_loop(0, nkb, mask_block, 0)

        def mask_tail(j, carry):
            madd_ref[blk_rows(j)] = jnp.full((g, SUBLANES, tq), NEG_BIG, jnp.float32)
            return carry

        lax.fori_loop(nkb, n_trips * ATTN_GROUP, mask_tail, 0)

    jq = qpos0 // tk
    s_refs = (s0_ref, s1_ref)
    heads = [slice(hh * HEAD_DIM, (hh + 1) * HEAD_DIM) for hh in range(HEADS_PER_STEP)]

    def keys_of(t):
        return pl.ds(pl.multiple_of(t * wide, wide), wide)

    def scores(hh, t):
        s_refs[hh][...] = jnp.dot(k_ref[0, keys_of(t), heads[hh]], qt_ref[0, heads[hh], :],
                                  preferred_element_type=jnp.float32)

    def softmax(hh, t, m, l):
        parts = []
        for u in range(ATTN_GROUP):
            j = t * ATTN_GROUP + u
            d = jnp.clip(j - jq, -2, 1) + 2
            parts.append(s_refs[hh][u * tk:(u + 1) * tk] + bias_ref[d, hh]
                         + madd_ref[blk_rows(j)].reshape(tk, tq))
        m_new = m
        for part in parts:
            m_new = jnp.maximum(m_new, jnp.max(part, axis=0, keepdims=True))
        alpha = jnp.exp(m - m_new)
        p = [jnp.exp(part - m_new) for part in parts]
        l = alpha * l + sum(jnp.sum(pu, axis=0, keepdims=True) for pu in p)
        pb = jnp.concatenate([pu.astype(jnp.bfloat16) for pu in p], axis=0)
        return m_new, l, alpha, pb

    def weighted_values(hh, t, pb):
        return jnp.dot(vt_ref[0, heads[hh], keys_of(t)], pb,
                       preferred_element_type=jnp.float32)

    def trip(t, carries):
        (m0, l0, acc0), (m1, l1, acc1) = carries
        acc1 = acc1 + weighted_values(1, jnp.maximum(t - 1, 0), p1_ref[...])
        scores(1, t)
        m0, l0, alpha0, pb0 = softmax(0, t, m0, l0)
        acc0 = alpha0 * acc0 + weighted_values(0, t, pb0)
        scores(0, jnp.minimum(t + 1, n_trips - 1))
        m1, l1, alpha1, pb1 = softmax(1, t, m1, l1)
        p1_ref[...] = pb1
        return (m0, l0, acc0), (m1, l1, alpha1 * acc1)

    scores(0, 0)
    p1_ref[...] = jnp.zeros(p1_ref.shape, p1_ref.dtype)
    carries = tuple((jnp.full((1, tq), NEG_BIG, jnp.float32), jnp.zeros((1, tq), jnp.float32),
                     jnp.zeros((HEAD_DIM, tq), jnp.float32)) for _ in range(2))
    (_, l0, acc0), (_, l1, acc1) = lax.fori_loop(0, n_trips, trip, carries)
    acc1 = acc1 + weighted_values(1, n_trips - 1, p1_ref[...])
    o_ref[0, :, heads[0]] = (acc0 / l0).T.astype(o_ref.dtype)
    o_ref[0, :, heads[1]] = (acc1 / l1).T.astype(o_ref.dtype)


def _attn_a(qt, qit, kwt, k, vt, kib, bias, *, tq, tk, q_off, n_lead, l_valid, topk):
    b, _, tqp = qt.shape
    lp = k.shape[1]
    assert lp % (ATTN_GROUP * tk) == 0 and tk % (4 * SUBLANES) == 0 and tk >= topk
    assert tqp % tq == 0 and q_off % tk == 0 and HEADS_PER_STEP == 2
    idx_bits = lp.bit_length()
    grid = (b, tqp // tq, N_HEADS // HEADS_PER_STEP)
    hw = HEADS_PER_STEP * HEAD_DIM
    kern = functools.partial(_attn_a_kernel, tq=tq, tk=tk, q_off=q_off, n_lead=n_lead,
                             l_valid=l_valid, topk=topk, idx_bits=idx_bits)
    return pl.pallas_call(
        kern, grid=grid,
        in_specs=[
            pl.BlockSpec((1, hw, tq), lambda bi, i, h: (bi, h, i)),
            pl.BlockSpec((1, IDX_QW, tq), lambda bi, i, h: (bi, 0, i)),
            pl.BlockSpec((1, LANES, tq), lambda bi, i, h: (bi, 0, i)),
            pl.BlockSpec((1, lp, hw), lambda bi, i, h: (bi, 0, h)),
            pl.BlockSpec((1, hw, lp), lambda bi, i, h: (bi, h, 0)),
            pl.BlockSpec((1, lp, IDX_DIM), lambda bi, i, h: (bi, 0, 0)),
            pl.BlockSpec((4, HEADS_PER_STEP, tk, tq), lambda bi, i, h: (0, h, 0, 0)),
        ],
        out_specs=pl.BlockSpec((1, tq, hw), lambda bi, i, h: (bi, i, h)),
        out_shape=jax.ShapeDtypeStruct((b, tqp, HD), jnp.bfloat16),
        scratch_shapes=[
            pltpu.VMEM((lp // SUBLANES, SUBLANES, tq), jnp.int32),
            pltpu.VMEM((lp // SUBLANES, SUBLANES, tq), jnp.float32),
            pltpu.VMEM((SUBLANES, tq), jnp.int32),
            pltpu.VMEM((SUBLANES, tq), jnp.int32),
            pltpu.VMEM((SUBLANES, tq), jnp.int32),
            pltpu.VMEM((ATTN_GROUP * tk, tq), jnp.float32),
            pltpu.VMEM((ATTN_GROUP * tk, tq), jnp.float32),
            pltpu.VMEM((ATTN_GROUP * tk, tq), jnp.bfloat16),
        ],
        compiler_params=pltpu.CompilerParams(
            dimension_semantics=("parallel", "parallel", "arbitrary"),
            vmem_limit_bytes=VMEM_LIMIT),
        name="dsa_attention",
    )(qt, qit, kwt, k, vt, kib, bias)


def _rel_bucket(rel):
    nb = N_BUCKETS // 2
    max_exact = nb // 2
    ret = jnp.where(rel > 0, nb, 0)
    n = jnp.abs(rel)
    nf = jnp.maximum(n, 1).astype(jnp.float32)
    large = max_exact + (jnp.log(nf / max_exact) / math.log(MAX_DISTANCE / max_exact)
                         * (nb - max_exact)).astype(jnp.int32)
    large = jnp.minimum(large, nb - 1)
    return ret + jnp.where(n < max_exact, n, large)


def _bias_tiles(rel_bias, tq, tk):
    w = tk + tq
    tiles = []
    for d in range(-2, 2):
        rel = d * tk + tk - 1 - jnp.arange(w, dtype=jnp.int32)
        vec = rel_bias.astype(jnp.float32)[_rel_bucket(rel)].T
        vec = jnp.roll(vec, -(tk - 1), axis=1)
        skew = jnp.tile(vec, (1, tk))[:, :tk * (w - 1)].reshape(-1, tk, w - 1)
        tiles.append(skew[:, :, :tq])
    return jnp.stack(tiles)


def _attn_b_kernel(q_ref, k_ref, v_ref, tri_ref, o_ref, *, tq, tk, q_off):
    i = pl.program_id(2)
    qpos0 = q_off + i * tq
    heads = [slice(hh * HEAD_DIM, (hh + 1) * HEAD_DIM) for hh in range(HEADS_PER_STEP)]
    tri = tri_ref[...]

    def block(j, diagonal, runs, accs):
        ds = pl.ds(pl.multiple_of(j * tk, tk), tk)
        if diagonal:
            causal = ((j * tk + lax.broadcasted_iota(jnp.int32, (tq, tk), 1))
                      < (qpos0 + lax.broadcasted_iota(jnp.int32, (tq, tk), 0)))
        zs = [lax.dot_general(q_ref[0, :, hs], k_ref[0, ds, hs], _NT,
                              preferred_element_type=jnp.float32) for hs in heads]
        tails, log_1ms, withins = [], [], []
        for z in zs:
            tail = jnp.log(1.0 + jnp.exp(-jnp.abs(z)))
            log_1m = -jnp.maximum(z, 0.0) - tail
            if diagonal:
                log_1m = jnp.where(causal, log_1m, 0.0)
            hi = log_1m.astype(jnp.bfloat16)
            lo = (log_1m - hi.astype(jnp.float32)).astype(jnp.bfloat16)
            withins.append(jnp.dot(hi, tri, preferred_element_type=jnp.float32)
                           + jnp.dot(lo, tri, preferred_element_type=jnp.float32))
            tails.append(tail)
            log_1ms.append(log_1m)
        new_runs, new_accs = [], []
        for hh, hs in enumerate(heads):
            a = jnp.exp(jnp.minimum(zs[hh], 0.0) - tails[hh] + withins[hh] + runs[hh])
            if diagonal:
                a = jnp.where(causal, a, 0.0)
            new_accs.append(accs[hh] + jnp.dot(a.astype(jnp.bfloat16), v_ref[0, ds, hs],
                                               preferred_element_type=jnp.float32))
            new_runs.append(runs[hh] + withins[hh][:, 0:1] + log_1ms[hh][:, 0:1])
        return tuple(new_runs), tuple(new_accs)

    def cond(carry):
        j, runs, _ = carry
        return (j >= 0) & (jnp.max(jnp.maximum(*runs)) > EXP_ZERO_BELOW)

    def body(carry):
        j, runs, accs = carry
        runs, accs = block(j, False, runs, accs)
        return j - 1, runs, accs

    j0 = (qpos0 + tq - 2) // tk
    runs = tuple(jnp.zeros((tq, 1), jnp.float32) for _ in heads)
    accs = tuple(jnp.zeros((tq, HEAD_DIM), jnp.float32) for _ in heads)
    runs, accs = block(j0, True, runs, accs)
    _, _, accs = lax.while_loop(cond, body, (j0 - 1, runs, accs))
    for hs, acc in zip(heads, accs):
        o_ref[0, :, hs] = acc.astype(o_ref.dtype)


def _attn_b(q, k, v, *, tq, tk, q_off):
    b, tqp, _ = q.shape
    lp = k.shape[1]
    tri = (jnp.arange(tk)[:, None] > jnp.arange(tk)[None, :]).astype(jnp.bfloat16)
    assert q_off % tk == 0 and tq <= tk and (tq == tk or tqp == tq)
    hw = HEADS_PER_STEP * HEAD_DIM
    kern = functools.partial(_attn_b_kernel, tq=tq, tk=tk, q_off=q_off)
    return pl.pallas_call(
        kern, grid=(b, N_HEADS // HEADS_PER_STEP, tqp // tq),
        in_specs=[
            pl.BlockSpec((1, tq, hw), lambda bi, h, i: (bi, i, h)),
            pl.BlockSpec((1, lp, hw), lambda bi, h, i: (bi, 0, h)),
            pl.BlockSpec((1, lp, hw), lambda bi, h, i: (bi, 0, h)),
            _const_spec((tk, tk)),
        ],
        out_specs=pl.BlockSpec((1, tq, hw), lambda bi, h, i: (bi, i, h)),
        out_shape=jax.ShapeDtypeStruct((b, tqp, HD), jnp.bfloat16),
        compiler_params=pltpu.CompilerParams(
            dimension_semantics=("parallel", "parallel", "parallel"),
            vmem_limit_bytes=VMEM_LIMIT),
        name="stick_breaking_attention",
    )(q, k, v, tri)


def _post_kernel(x_ref, o_ref, left_ref, wo_ref, g1_ref, b1_ref, wg_ref, wu_ref, cw_ref, cb_ref,
                 wd_ref, g2_ref, b2_ref, y_ref, conv_ref, carry_ref,
                 *, tm, d_ff, alpha, t_last, r_last):
    t = pl.program_id(1)

    @pl.when(t == 0)
    def _init():
        carry_ref[...] = left_ref[0]

    mix = jnp.dot(o_ref[0], wo_ref[...], preferred_element_type=jnp.float32)
    x1 = _layer_norm(alpha * x_ref[0] + mix, g1_ref[...], b1_ref[...])
    xb = x1.astype(jnp.bfloat16)
    row = lax.broadcasted_iota(jnp.int32, (tm, FF_CHUNK), 0)
    acc = jnp.zeros((tm, D_MODEL), jnp.float32)
    n_chunks = d_ff // FF_CHUNK

    def gate_up(c):
        cs = slice(c * FF_CHUNK, (c + 1) * FF_CHUNK)
        return (jnp.dot(xb, wg_ref[:, cs], preferred_element_type=jnp.float32),
                jnp.dot(xb, wu_ref[:, cs], preferred_element_type=jnp.float32))

    nxt = gate_up(0)
    for c in range(n_chunks):
        cs = slice(c * FF_CHUNK, (c + 1) * FF_CHUNK)
        g, u = nxt
        if c + 1 < n_chunks:
            nxt = gate_up(c + 1)
        prev = carry_ref[:, cs]
        g1 = jnp.where(row == 0, prev[7:8], pltpu.roll(g, 1, 0))
        g2 = jnp.where(row == 0, prev[6:7], jnp.where(row == 1, prev[7:8], pltpu.roll(g, 2, 0)))
        gc = cb_ref[:, cs] + cw_ref[0:1, cs] * g2
        gc = gc + cw_ref[1:2, cs] * g1
        gc = gc + cw_ref[2:3, cs] * g
        hid = jax.nn.gelu(gc) * u
        acc = acc + jnp.dot(hid.astype(jnp.bfloat16), wd_ref[cs, :],
                            preferred_element_type=jnp.float32)
        carry_ref[:, cs] = g[tm - SUBLANES:tm]

        @pl.when(t == t_last)
        def _emit():
            conv_ref[0, :, cs] = g[r_last:r_last + SUBLANES]

    y_ref[0] = _layer_norm(alpha * x1 + acc, g2_ref[...], b2_ref[...])


def _post(x, o, left8, t_valid, tm, w, alpha):
    b, tp, _ = x.shape
    d_ff = w["wg"].shape[1]
    assert d_ff % FF_CHUNK == 0 and t_valid % SUBLANES == 0 and t_valid >= SUBLANES
    t_last = (t_valid - 1) // tm
    r_last = (t_valid - SUBLANES) % tm
    row = lambda width: pl.BlockSpec((1, tm, width), lambda i, j: (i, j, 0))
    vec = lambda a: a.reshape(1, -1).astype(jnp.float32)
    consts = [w["wo"], vec(w["g1"]), vec(w["b1"]), w["wg"], w["wu"], w["cw"].astype(jnp.float32),
              vec(w["cb"]), w["wd"], vec(w["g2"]), vec(w["b2"])]
    kern = functools.partial(_post_kernel, tm=tm, d_ff=d_ff, alpha=alpha,
                             t_last=t_last, r_last=r_last)
    return pl.pallas_call(
        kern, grid=(b, tp // tm),
        in_specs=[row(D_MODEL), row(HD), pl.BlockSpec((1, SUBLANES, d_ff), lambda i, j: (i, 0, 0))]
                 + [_const_spec(c.shape) for c in consts],
        out_specs=[row(D_MODEL), pl.BlockSpec((1, SUBLANES, d_ff), lambda i, j: (i, 0, 0))],
        out_shape=[jax.ShapeDtypeStruct((b, tp, D_MODEL), jnp.float32),
                   jax.ShapeDtypeStruct((b, SUBLANES, d_ff), jnp.float32)],
        scratch_shapes=[pltpu.VMEM((SUBLANES, d_ff), jnp.float32)],
        compiler_params=pltpu.CompilerParams(
            dimension_semantics=("parallel", "arbitrary"), vmem_limit_bytes=VMEM_LIMIT),
        name="outproj_ln_convffn_ln",
    )(x, o, left8, *consts)


def _split_a(w_in):
    bf = jnp.bfloat16
    off_qi = 3 * HD
    off_ki = off_qi + IDX_QW
    wq, wk, wv = w_in[:, :HD], w_in[:, HD:2 * HD], w_in[:, 2 * HD:3 * HD]
    wqi = w_in[:, off_qi:off_ki]
    wkw = jnp.pad(w_in[:, off_ki:], ((0, 0), (0, LANES - IDX_DIM - IDX_HEADS)))
    row_scale = jnp.concatenate([
        jnp.ones((IDX_DIM,), jnp.float32),
        jnp.full((IDX_HEADS,), IDX_QW ** -0.5, jnp.float32),
        jnp.zeros((LANES - IDX_DIM - IDX_HEADS,), jnp.float32)]).reshape(LANES, 1)
    return [wq.T.astype(bf), wk.astype(bf), wv.astype(bf), wv.T.astype(bf), wqi.T.astype(bf),
            wkw.astype(bf), wkw.T.astype(bf), row_scale]


def _split_b(w_in):
    bf = jnp.bfloat16
    return [w_in[:, :HD].astype(bf), w_in[:, HD:2 * HD].astype(bf), w_in[:, 2 * HD:].astype(bf)]


def _with_cache(cache, new_bf, lp):
    b, p = cache.shape[:2]
    flat = cache.reshape(b, p, -1).astype(jnp.bfloat16)
    t = new_bf.shape[1]
    return jnp.pad(jnp.concatenate([flat, new_bf], axis=1), ((0, 0), (0, lp - p - t), (0, 0)))


def kernel(x_prompt, x_sample, cache_a_k, cache_a_v, cache_a_idx_k, cache_b_k, cache_b_v,
           state_ffn_conv, meta_tokens, rel_bias, w_a_in, w_a_out, w_b_in, w_b_out,
           ln1_g, ln1_b, ln2_g, ln2_b, w_ffn_gate, w_ffn_up, ffn_conv_w, ffn_conv_b, w_ffn_down):
    bp, seq, _ = x_prompt.shape
    bs, ts, _ = x_sample.shape
    depth = ln1_g.shape[0]
    d_ff = w_ffn_gate.shape[2]
    past = cache_a_k.shape[2]
    alpha = (2 * depth) ** 0.25
    t_p = N_META + seq
    tp_pad = _round_up(t_p, ROW_TILE)
    ls = past + ts
    ls_pad = _round_up(ls, ATTN_GROUP * KEY_TILE)
    assert past % KEY_TILE == 0 and ts % SUBLANES == 0 and ts <= SAMPLE_TILE
    topk_p = min(TOPK_MAX, seq // 4)
    topk_s = min(TOPK_MAX, ls // 4)
    bf = jnp.bfloat16

    meta = jnp.broadcast_to(meta_tokens.astype(x_prompt.dtype)[None], (bp, N_META, D_MODEL))
    xp = jnp.pad(jnp.concatenate([meta, x_prompt], axis=1), ((0, 0), (0, tp_pad - t_p), (0, 0)))
    xs = x_sample
    bias_p = _bias_tiles(rel_bias, ROW_TILE, KEY_TILE)
    bias_s = _bias_tiles(rel_bias, SAMPLE_TILE, KEY_TILE)
    left_p = jnp.zeros((bp, SUBLANES, d_ff), jnp.float32)

    heads = lambda a: a.reshape(a.shape[0], a.shape[1], N_HEADS, HEAD_DIM)
    outs = {k: [] for k in ("akp", "avp", "aip", "aks", "avs", "ais",
                            "bkp", "bvp", "bks", "bvs", "cp", "cs")}
    for i in range(depth):
        j = i // N_MIXERS
        if i % N_MIXERS == 0:
            wts = _split_a(w_a_in[j])
            qt, kc, kb, vc, vt, qit, kic, kib, kwt = _project(xp, t_p, ROW_TILE, wts, True)
            key_pad = _round_up(tp_pad, ATTN_GROUP * KEY_TILE) - tp_pad
            op = _attn_a(qt, qit, kwt, jnp.pad(kb, ((0, 0), (0, key_pad), (0, 0))),
                         jnp.pad(vt, ((0, 0), (0, 0), (0, key_pad))),
                         jnp.pad(kib, ((0, 0), (0, key_pad), (0, 0))), bias_p,
                         tq=ROW_TILE, tk=KEY_TILE, q_off=0, n_lead=N_META, l_valid=t_p, topk=topk_p)
            outs["akp"].append(heads(kc)); outs["avp"].append(heads(vc)); outs["aip"].append(kic)
            xs_pad = jnp.pad(xs, ((0, 0), (0, SAMPLE_TILE - ts), (0, 0)))
            qt, kc, kb, vc, vt, qit, kic, kib, kwt = _project(xs_pad, ts, SAMPLE_TILE, wts, True)
            cache_vt = jnp.swapaxes(cache_a_v[j].reshape(bs, past, HD).astype(bf), 1, 2)
            vt_all = jnp.pad(jnp.concatenate([cache_vt, vt[:, :, :ts]], axis=2),
                             ((0, 0), (0, 0), (0, ls_pad - ls)))
            os_ = _attn_a(qt, qit, kwt, _with_cache(cache_a_k[j], kb[:, :ts], ls_pad), vt_all,
                          _with_cache(cache_a_idx_k[j], kib[:, :ts], ls_pad), bias_s,
                          tq=SAMPLE_TILE, tk=KEY_TILE, q_off=past, n_lead=0, l_valid=ls,
                          topk=topk_s)[:, :ts]
            outs["aks"].append(heads(kc)); outs["avs"].append(heads(vc)); outs["ais"].append(kic)
            w_out = w_a_out[j]
        else:
            wts = _split_b(w_b_in[j])
            q, kc, kb, vc, vb = _project(xp, t_p, ROW_TILE, wts, False)
            op = _attn_b(q, kb, vb, tq=ROW_TILE, tk=KEY_TILE, q_off=0)
            outs["bkp"].append(heads(kc)); outs["bvp"].append(heads(vc))
            q, kc, kb, vc, vb = _project(xs, ts, ts, wts, False)
            os_ = _attn_b(q, _with_cache(cache_b_k[j], kb, ls_pad),
                          _with_cache(cache_b_v[j], vb, ls_pad), tq=ts, tk=KEY_TILE, q_off=past)
            outs["bks"].append(heads(kc)); outs["bvs"].append(heads(vc))
            w_out = w_b_out[j]
        w = dict(wo=w_out.astype(bf), g1=ln1_g[i], b1=ln1_b[i], wg=w_ffn_gate[i].astype(bf),
                 wu=w_ffn_up[i].astype(bf), cw=ffn_conv_w[i], cb=ffn_conv_b[i],
                 wd=w_ffn_down[i].astype(bf), g2=ln2_g[i], b2=ln2_b[i])
        xp, conv_p = _post(xp, op, left_p, t_p, ROW_TILE, w, alpha)
        left_s = jnp.pad(state_ffn_conv[i].astype(jnp.float32),
                         ((0, 0), (SUBLANES - (CONV_W - 1), 0), (0, 0)))
        xs, conv_s = _post(xs, os_, left_s, ts, ts, w, alpha)
        outs["cp"].append(conv_p[:, SUBLANES - (CONV_W - 1):])
        outs["cs"].append(conv_s[:, SUBLANES - (CONV_W - 1):])

    st = jnp.stack
    return (xp[:, N_META:t_p], xs,
            st(outs["akp"]), st(outs["avp"]), st(outs["aip"]),
            st(outs["bkp"]), st(outs["bvp"]), st(outs["cp"]),
            st(outs["aks"]), st(outs["avs"]), st(outs["ais"]),
            st(outs["bks"]), st(outs["bvs"]), st(outs["cs"]))
```

```python
import functools
import math

import jax
import jax.numpy as jnp
from jax import lax
from jax.experimental import pallas as pl
from jax.experimental.pallas import tpu as pltpu

D_MODEL = 1024
N_HEADS = 8
HEAD_DIM = 128
HD = N_HEADS * HEAD_DIM
IDX_HEADS = 8
IDX_DIM = 64
IDX_QW = IDX_HEADS * IDX_DIM
CHUNK = 64
CHUNK_SHIFT = 6
N_META = 16
N_MIXERS = 2
TOPK_MAX = 256
N_BUCKETS = 32
MAX_DISTANCE = 128
CONV_W = 3
LN_EPS = 1e-5

LANES = 128
SUBLANES = 8
ROW_TILE = 256
KEY_TILE = 256
FF_CHUNK = 256
ATTN_GROUP = 4
HEADS_PER_STEP = 2
SAMPLE_TILE = 128
VMEM_LIMIT = 56 * 1024 * 1024

NEG_BIG = -1e30
INT_MIN = -2 ** 31
KEY_NEG_INF = -2139095041
EXP_ZERO_BELOW = -104.0

_NT = (((1,), (1,)), ((), ()))


def _round_up(n, m):
    return -(-n // m) * m


def _layer_norm(x, g, b):
    mu = jnp.mean(x, -1, keepdims=True)
    xc = x - mu
    var = jnp.mean(xc * xc, -1, keepdims=True)
    return xc * lax.rsqrt(var + LN_EPS) * g + b


def _proj_a_kernel(x_ref, wqt_ref, wk_ref, wv_ref, wvt_ref, wqit_ref, wkw_ref, wkwt_ref, rs_ref,
                   qt_ref, kc_ref, kb_ref, vc_ref, vt_ref, qit_ref, kic_ref, kib_ref, kwt_ref):
    xb = x_ref[0].astype(jnp.bfloat16)

    def cols(wt_ref):
        return lax.dot_general(wt_ref[...], xb, _NT, preferred_element_type=jnp.float32)

    qt_ref[0] = (cols(wqt_ref) * (HEAD_DIM ** -0.5)).astype(jnp.bfloat16)
    k = jnp.dot(xb, wk_ref[...], preferred_element_type=jnp.float32)
    kc_ref[0] = k
    kb_ref[0] = k.astype(jnp.bfloat16)
    vc_ref[0] = jnp.dot(xb, wv_ref[...], preferred_element_type=jnp.float32)
    vt_ref[0] = cols(wvt_ref).astype(jnp.bfloat16)
    qit_ref[0] = cols(wqit_ref).astype(jnp.bfloat16)
    kw = jnp.dot(xb, wkw_ref[...], preferred_element_type=jnp.float32)
    kic_ref[0] = kw[:, :IDX_DIM]
    kib_ref[0] = kw[:, :IDX_DIM].astype(jnp.bfloat16)
    kwt_ref[0] = cols(wkwt_ref) * rs_ref[...]


def _proj_b_kernel(x_ref, wq_ref, wk_ref, wv_ref, q_ref, kc_ref, kb_ref, vc_ref, vb_ref):
    xb = x_ref[0].astype(jnp.bfloat16)
    q = jnp.dot(xb, wq_ref[...], preferred_element_type=jnp.float32)
    q_ref[0] = (q * (HEAD_DIM ** -0.5)).astype(jnp.bfloat16)
    k = jnp.dot(xb, wk_ref[...], preferred_element_type=jnp.float32)
    kc_ref[0] = k
    kb_ref[0] = k.astype(jnp.bfloat16)
    v = jnp.dot(xb, wv_ref[...], preferred_element_type=jnp.float32)
    vc_ref[0] = v
    vb_ref[0] = v.astype(jnp.bfloat16)


def _const_spec(shape):
    return pl.BlockSpec(shape, lambda *_: (0,) * len(shape))


def _project(x, t_valid, tm, weights, is_a):
    b, tp, _ = x.shape
    grid = (b, tp // tm)
    row = lambda w: pl.BlockSpec((1, tm, w), lambda i, j: (i, j, 0))
    bf = jnp.bfloat16
    f32 = jnp.float32
    pad = lambda w, dt: jax.ShapeDtypeStruct((b, tp, w), dt)
    exact = lambda w, dt: jax.ShapeDtypeStruct((b, t_valid, w), dt)
    in_specs = [row(D_MODEL)] + [_const_spec(w.shape) for w in weights]
    if is_a:
        col = lambda w: pl.BlockSpec((1, w, tm), lambda i, j: (i, 0, j))
        padt = lambda w, dt: jax.ShapeDtypeStruct((b, w, tp), dt)
        out_shape = [padt(HD, bf), exact(HD, f32), pad(HD, bf), exact(HD, f32), padt(HD, bf),
                     padt(IDX_QW, bf), exact(IDX_DIM, f32), pad(IDX_DIM, bf), padt(LANES, f32)]
        out_specs = [col(HD), row(HD), row(HD), row(HD), col(HD),
                     col(IDX_QW), row(IDX_DIM), row(IDX_DIM), col(LANES)]
        body = _proj_a_kernel
    else:
        out_shape = [pad(HD, bf), exact(HD, f32), pad(HD, bf), exact(HD, f32), pad(HD, bf)]
        out_specs = [row(HD)] * 5
        body = _proj_b_kernel
    return pl.pallas_call(
        body, grid=grid, in_specs=in_specs, out_specs=out_specs, out_shape=out_shape,
        compiler_params=pltpu.CompilerParams(
            dimension_semantics=("parallel", "parallel"), vmem_limit_bytes=VMEM_LIMIT),
        name="proj_a" if is_a else "proj_b",
    )(x, *weights)


def _vis_end(qpos, n_lead, l_valid):
    ve = jnp.where(qpos < n_lead, n_lead,
                   n_lead + ((((qpos - n_lead) >> CHUNK_SHIFT) + 1) << CHUNK_SHIFT))
    return jnp.minimum(ve, l_valid)


def _key_blocks(i, tq, tk, q_off, n_lead, l_valid):
    return (_vis_end(q_off + i * tq + tq - 1, n_lead, l_valid) + tk - 1) // tk


def _dsa_select_kernel(qit_ref, kwt_ref, kib_ref, madd_ref, keys_ref, thr_ref, need_ref, cut_ref,
                       *, tq, tk, q_off, n_lead, l_valid, topk, idx_bits):
    i = pl.program_id(1)
    qpos0 = q_off + i * tq
    nkb = _key_blocks(i, tq, tk, q_off, n_lead, l_valid)
    g = tk // SUBLANES

    def blk_rows(j):
        return pl.ds(pl.multiple_of(j * g, g), g)

    def blk_keys(j):
        return pl.ds(pl.multiple_of(j * tk, tk), tk)

    qpos = qpos0 + lax.broadcasted_iota(jnp.int32, (1, tq), 1)
    vis_end = _vis_end(qpos, n_lead, l_valid)

    def score_block(j, carry):
        kib = kib_ref[0, blk_keys(j), :]
        sc = jnp.zeros((tk, tq), jnp.float32)
        for ih in range(IDX_HEADS):
            s = jnp.dot(kib, qit_ref[0, ih * IDX_DIM:(ih + 1) * IDX_DIM, :],
                        preferred_element_type=jnp.float32)
            sc = sc + jnp.maximum(s, 0.0) * kwt_ref[0, IDX_DIM + ih:IDX_DIM + ih + 1, :]
        sc = jnp.where(sc == 0.0, 0.0, sc)
        bits = pltpu.bitcast(sc, jnp.int32)
        key = jnp.where(bits < 0, bits ^ 0x7FFFFFFF, bits)
        pos = j * tk + lax.broadcasted_iota(jnp.int32, (tk, tq), 0)
        key = jnp.where(pos < vis_end, key, KEY_NEG_INF)
        keys_ref[blk_rows(j)] = key.reshape(g, SUBLANES, tq)
        return carry

    lax.fori_loop(0, nkb, score_block, 0)

    def key_pos(j):
        shape = (g, SUBLANES, tq)
        return (j * tk + lax.broadcasted_iota(jnp.int32, shape, 0) * SUBLANES
                + lax.broadcasted_iota(jnp.int32, shape, 1))

    def count(pred):
        def blk(j, cnt):
            hit = pred(keys_ref[blk_rows(j)], j).astype(jnp.int32)
            part = jnp.sum(hit.reshape(4, g // 4, SUBLANES, tq), axis=1)
            return cnt + jnp.sum(part, axis=0)
        cnt = lax.fori_loop(0, nkb, blk, jnp.zeros((SUBLANES, tq), jnp.int32))
        return jnp.broadcast_to(jnp.sum(cnt, axis=0, keepdims=True), (SUBLANES, tq))

    def thr_step(b, ut):
        cand = (ut | (jnp.int32(1) << (31 - b))) ^ INT_MIN
        cnt = count(lambda kk, j: kk >= cand)
        return jnp.where(cnt >= topk, cand ^ INT_MIN, ut)

    thr = lax.fori_loop(0, 32, thr_step, jnp.zeros((SUBLANES, tq), jnp.int32)) ^ INT_MIN
    cnt_gt = count(lambda kk, j: kk > thr)
    cnt_ge = count(lambda kk, j: kk >= thr)
    thr_ref[...] = thr
    need_ref[...] = topk - cnt_gt
    cut_ref[...] = jnp.full((SUBLANES, tq), 1 << idx_bits, jnp.int32)
    excess = (cnt_ge > topk) & (thr != KEY_NEG_INF)

    @pl.when(jnp.max(excess.astype(jnp.int32)) > 0)
    def _ties():
        need = need_ref[...]

        def cut_step(b, cut):
            cand = cut | (jnp.int32(1) << (idx_bits - 1 - b))
            cnt = count(lambda kk, j: (kk == thr) & (key_pos(j) < cand))
            return jnp.where(cnt <= need, cand, cut)

        cut_ref[...] = lax.fori_loop(0, idx_bits, cut_step, jnp.zeros((SUBLANES, tq), jnp.int32))

    def mask_block(j, carry):
        kk = keys_ref[blk_rows(j)]
        thr_v = thr_ref[...]
        sel = (((kk > thr_v) | ((kk == thr_v) & (key_pos(j) < cut_ref[...])))
               & (kk != KEY_NEG_INF))
        madd_ref[0, 0, blk_keys(j), :] = (
            jnp.where(sel, 0.0, NEG_BIG).reshape(tk, tq).astype(madd_ref.dtype))
        return carry

    lax.fori_loop(0, nkb, mask_block, 0)

    def mask_tail(j, carry):
        madd_ref[0, 0, blk_keys(j), :] = jnp.full((tk, tq), NEG_BIG, madd_ref.dtype)
        return carry

    lax.fori_loop(nkb, madd_ref.shape[2] // tk, mask_tail, 0)


def _dsa_attend_kernel(qt_ref, k_ref, vt_ref, madd_ref, bias_ref, o_ref, s0_ref, s1_ref, p1_ref,
                       *, tq, tk, q_off, n_lead, l_valid):
    i = pl.program_id(2)
    qpos0 = q_off + i * tq
    nkb = _key_blocks(i, tq, tk, q_off, n_lead, l_valid)
    wide = ATTN_GROUP * tk
    n_trips = (nkb + ATTN_GROUP - 1) // ATTN_GROUP
    jq = qpos0 // tk
    s_refs = (s0_ref, s1_ref)
    heads = [slice(hh * HEAD_DIM, (hh + 1) * HEAD_DIM) for hh in range(HEADS_PER_STEP)]

    def keys_of(t):
        return pl.ds(pl.multiple_of(t * wide, wide), wide)

    def scores(hh, t):
        s_refs[hh][...] = jnp.dot(k_ref[0, keys_of(t), heads[hh]], qt_ref[0, heads[hh], :],
                                  preferred_element_type=jnp.float32)

    def softmax(hh, t, m, l):
        parts = []
        for u in range(ATTN_GROUP):
            j = t * ATTN_GROUP + u
            d = jnp.clip(j - jq, -2, 1) + 2
            madd = madd_ref[0, 0, pl.ds(pl.multiple_of(j * tk, tk), tk), :]
            parts.append(s_refs[hh][u * tk:(u + 1) * tk] + bias_ref[d, hh]
                         + madd.astype(jnp.float32))
        m_new = m
        for part in parts:
            m_new = jnp.maximum(m_new, jnp.max(part, axis=0, keepdims=True))
        alpha = jnp.exp(m - m_new)
        p = [jnp.exp(part - m_new) for part in parts]
        l = alpha * l + sum(jnp.sum(pu, axis=0, keepdims=True) for pu in p)
        pb = jnp.concatenate([pu.astype(jnp.bfloat16) for pu in p], axis=0)
        return m_new, l, alpha, pb

    def weighted_values(hh, t, pb):
        return jnp.dot(vt_ref[0, heads[hh], keys_of(t)], pb,
                       preferred_element_type=jnp.float32)

    def trip(t, carries):
        (m0, l0, acc0), (m1, l1, acc1) = carries
        acc1 = acc1 + weighted_values(1, jnp.maximum(t - 1, 0), p1_ref[...])
        scores(1, t)
        m0, l0, alpha0, pb0 = softmax(0, t, m0, l0)
        acc0 = alpha0 * acc0 + weighted_values(0, t, pb0)
        scores(0, jnp.minimum(t + 1, n_trips - 1))
        m1, l1, alpha1, pb1 = softmax(1, t, m1, l1)
        p1_ref[...] = pb1
        return (m0, l0, acc0), (m1, l1, alpha1 * acc1)

    scores(0, 0)
    p1_ref[...] = jnp.zeros(p1_ref.shape, p1_ref.dtype)
    carries = tuple((jnp.full((1, tq), NEG_BIG, jnp.float32), jnp.zeros((1, tq), jnp.float32),
                     jnp.zeros((HEAD_DIM, tq), jnp.float32)) for _ in range(2))
    (_, l0, acc0), (_, l1, acc1) = lax.fori_loop(0, n_trips, trip, carries)
    acc1 = acc1 + weighted_values(1, n_trips - 1, p1_ref[...])
    o_ref[0, :, heads[0]] = (acc0 / l0).T.astype(o_ref.dtype)
    o_ref[0, :, heads[1]] = (acc1 / l1).T.astype(o_ref.dtype)


def _attn_a(qt, qit, kwt, k, vt, kib, bias, *, tq, tk, q_off, n_lead, l_valid, topk):
    b, _, tqp = qt.shape
    lp = k.shape[1]
    assert lp % (ATTN_GROUP * tk) == 0 and tk % (4 * SUBLANES) == 0 and tk >= topk
    assert tqp % tq == 0 and q_off % tk == 0 and HEADS_PER_STEP == 2
    n_q = tqp // tq
    hw = HEADS_PER_STEP * HEAD_DIM
    geom = dict(tq=tq, tk=tk, q_off=q_off, n_lead=n_lead, l_valid=l_valid)
    madd = pl.pallas_call(
        functools.partial(_dsa_select_kernel, topk=topk, idx_bits=lp.bit_length(), **geom),
        grid=(b, n_q),
        in_specs=[
            pl.BlockSpec((1, IDX_QW, tq), lambda bi, i: (bi, 0, i)),
            pl.BlockSpec((1, LANES, tq), lambda bi, i: (bi, 0, i)),
            pl.BlockSpec((1, lp, IDX_DIM), lambda bi, i: (bi, 0, 0)),
        ],
        out_specs=pl.BlockSpec((1, 1, lp, tq), lambda bi, i: (bi, i, 0, 0)),
        out_shape=jax.ShapeDtypeStruct((b, n_q, lp, tq), jnp.bfloat16),
        scratch_shapes=[
            pltpu.VMEM((lp // SUBLANES, SUBLANES, tq), jnp.int32),
            pltpu.VMEM((SUBLANES, tq), jnp.int32),
            pltpu.VMEM((SUBLANES, tq), jnp.int32),
            pltpu.VMEM((SUBLANES, tq), jnp.int32),
        ],
        compiler_params=pltpu.CompilerParams(
            dimension_semantics=("parallel", "parallel"), vmem_limit_bytes=VMEM_LIMIT),
        name="dsa_select",
    )(qit, kwt, kib)
    return pl.pallas_call(
        functools.partial(_dsa_attend_kernel, **geom),
        grid=(b, N_HEADS // HEADS_PER_STEP, n_q),
        in_specs=[
            pl.BlockSpec((1, hw, tq), lambda bi, h, i: (bi, h, i)),
            pl.BlockSpec((1, lp, hw), lambda bi, h, i: (bi, 0, h)),
            pl.BlockSpec((1, hw, lp), lambda bi, h, i: (bi, h, 0)),
            pl.BlockSpec((1, 1, lp, tq), lambda bi, h, i: (bi, i, 0, 0)),
            pl.BlockSpec((4, HEADS_PER_STEP, tk, tq), lambda bi, h, i: (0, h, 0, 0)),
        ],
        out_specs=pl.BlockSpec((1, tq, hw), lambda bi, h, i: (bi, i, h)),
        out_shape=jax.ShapeDtypeStruct((b, tqp, HD), jnp.bfloat16),
        scratch_shapes=[
            pltpu.VMEM((ATTN_GROUP * tk, tq), jnp.float32),
            pltpu.VMEM((ATTN_GROUP * tk, tq), jnp.float32),
            pltpu.VMEM((ATTN_GROUP * tk, tq), jnp.bfloat16),
        ],
        compiler_params=pltpu.CompilerParams(
            dimension_semantics=("parallel", "parallel", "parallel"),
            vmem_limit_bytes=VMEM_LIMIT),
        name="dsa_attend",
    )(qt, k, vt, madd, bias)


def _rel_bucket(rel):
    nb = N_BUCKETS // 2
    max_exact = nb // 2
    ret = jnp.where(rel > 0, nb, 0)
    n = jnp.abs(rel)
    nf = jnp.maximum(n, 1).astype(jnp.float32)
    large = max_exact + (jnp.log(nf / max_exact) / math.log(MAX_DISTANCE / max_exact)
                         * (nb - max_exact)).astype(jnp.int32)
    large = jnp.minimum(large, nb - 1)
    return ret + jnp.where(n < max_exact, n, large)


def _bias_tiles(rel_bias, tq, tk):
    w = tk + tq
    tiles = []
    for d in range(-2, 2):
        rel = d * tk + tk - 1 - jnp.arange(w, dtype=jnp.int32)
        vec = rel_bias.astype(jnp.float32)[_rel_bucket(rel)].T
        vec = jnp.roll(vec, -(tk - 1), axis=1)
        skew = jnp.tile(vec, (1, tk))[:, :tk * (w - 1)].reshape(-1, tk, w - 1)
        tiles.append(skew[:, :, :tq])
    return jnp.stack(tiles)


def _attn_b_kernel(q_ref, k_ref, v_ref, tri_ref, o_ref, *, tq, tk, q_off):
    i = pl.program_id(2)
    qpos0 = q_off + i * tq
    heads = [slice(hh * HEAD_DIM, (hh + 1) * HEAD_DIM) for hh in range(HEADS_PER_STEP)]
    tri = tri_ref[...]

    def block(j, diagonal, runs, accs):
        ds = pl.ds(pl.multiple_of(j * tk, tk), tk)
        if diagonal:
            causal = ((j * tk + lax.broadcasted_iota(jnp.int32, (tq, tk), 1))
                      < (qpos0 + lax.broadcasted_iota(jnp.int32, (tq, tk), 0)))
        zs = [lax.dot_general(q_ref[0, :, hs], k_ref[0, ds, hs], _NT,
                              preferred_element_type=jnp.float32) for hs in heads]
        tails, log_1ms, withins = [], [], []
        for z in zs:
            tail = jnp.log(1.0 + jnp.exp(-jnp.abs(z)))
            log_1m = -jnp.maximum(z, 0.0) - tail
            if diagonal:
                log_1m = jnp.where(causal, log_1m, 0.0)
            hi = log_1m.astype(jnp.bfloat16)
            lo = (log_1m - hi.astype(jnp.float32)).astype(jnp.bfloat16)
            withins.append(jnp.dot(hi, tri, preferred_element_type=jnp.float32)
                           + jnp.dot(lo, tri, preferred_element_type=jnp.float32))
            tails.append(tail)
            log_1ms.append(log_1m)
        new_runs, new_accs = [], []
        for hh, hs in enumerate(heads):
            a = jnp.exp(jnp.minimum(zs[hh], 0.0) - tails[hh] + withins[hh] + runs[hh])
            if diagonal:
                a = jnp.where(causal, a, 0.0)
            new_accs.append(accs[hh] + jnp.dot(a.astype(jnp.bfloat16), v_ref[0, ds, hs],
                                               preferred_element_type=jnp.float32))
            new_runs.append(runs[hh] + withins[hh][:, 0:1] + log_1ms[hh][:, 0:1])
        return tuple(new_runs), tuple(new_accs)

    def cond(carry):
        j, runs, _ = carry
        return (j >= 0) & (jnp.max(jnp.maximum(*runs)) > EXP_ZERO_BELOW)

    def body(carry):
        j, runs, accs = carry
        runs, accs = block(j, False, runs, accs)
        return j - 1, runs, accs

    j0 = (qpos0 + tq - 2) // tk
    runs = tuple(jnp.zeros((tq, 1), jnp.float32) for _ in heads)
    accs = tuple(jnp.zeros((tq, HEAD_DIM), jnp.float32) for _ in heads)
    runs, accs = block(j0, True, runs, accs)
    _, _, accs = lax.while_loop(cond, body, (j0 - 1, runs, accs))
    for hs, acc in zip(heads, accs):
        o_ref[0, :, hs] = acc.astype(o_ref.dtype)


def _attn_b(q, k, v, *, tq, tk, q_off):
    b, tqp, _ = q.shape
    lp = k.shape[1]
    tri = (jnp.arange(tk)[:, None] > jnp.arange(tk)[None, :]).astype(jnp.bfloat16)
    assert q_off % tk == 0 and tq <= tk and (tq == tk or tqp == tq)
    hw = HEADS_PER_STEP * HEAD_DIM
    kern = functools.partial(_attn_b_kernel, tq=tq, tk=tk, q_off=q_off)
    return pl.pallas_call(
        kern, grid=(b, N_HEADS // HEADS_PER_STEP, tqp // tq),
        in_specs=[
            pl.BlockSpec((1, tq, hw), lambda bi, h, i: (bi, i, h)),
            pl.BlockSpec((1, lp, hw), lambda bi, h, i: (bi, 0, h)),
            pl.BlockSpec((1, lp, hw), lambda bi, h, i: (bi, 0, h)),
            _const_spec((tk, tk)),
        ],
        out_specs=pl.BlockSpec((1, tq, hw), lambda bi, h, i: (bi, i, h)),
        out_shape=jax.ShapeDtypeStruct((b, tqp, HD), jnp.bfloat16),
        compiler_params=pltpu.CompilerParams(
            dimension_semantics=("parallel", "parallel", "parallel"),
            vmem_limit_bytes=VMEM_LIMIT),
        name="stick_breaking_attention",
    )(q, k, v, tri)


def _post_kernel(x_ref, o_ref, left_ref, wo_ref, g1_ref, b1_ref, wg_ref, wu_ref, cw_ref, cb_ref,
                 wd_ref, g2_ref, b2_ref, y_ref, conv_ref, carry_ref,
                 *, tm, d_ff, alpha, t_last, r_last):
    t = pl.program_id(1)

    @pl.when(t == 0)
    def _init():
        carry_ref[...] = left_ref[0]

    mix = jnp.dot(o_ref[0], wo_ref[...], preferred_element_type=jnp.float32)
    x1 = _layer_norm(alpha * x_ref[0] + mix, g1_ref[...], b1_ref[...])
    xb = x1.astype(jnp.bfloat16)
    row = lax.broadcasted_iota(jnp.int32, (tm, FF_CHUNK), 0)
    acc = jnp.zeros((tm, D_MODEL), jnp.float32)
    n_chunks = d_ff // FF_CHUNK

    def gate_up(c):
        cs = slice(c * FF_CHUNK, (c + 1) * FF_CHUNK)
        return (jnp.dot(xb, wg_ref[:, cs], preferred_element_type=jnp.float32),
                jnp.dot(xb, wu_ref[:, cs], preferred_element_type=jnp.float32))

    nxt = gate_up(0)
    for c in range(n_chunks):
        cs = slice(c * FF_CHUNK, (c + 1) * FF_CHUNK)
        g, u = nxt
        if c + 1 < n_chunks:
            nxt = gate_up(c + 1)
        prev = carry_ref[:, cs]
        g1 = jnp.where(row == 0, prev[7:8], pltpu.roll(g, 1, 0))
        g2 = jnp.where(row == 0, prev[6:7], jnp.where(row == 1, prev[7:8], pltpu.roll(g, 2, 0)))
        gc = cb_ref[:, cs] + cw_ref[0:1, cs] * g2
        gc = gc + cw_ref[1:2, cs] * g1
        gc = gc + cw_ref[2:3, cs] * g
        hid = jax.nn.gelu(gc) * u
        acc = acc + jnp.dot(hid.astype(jnp.bfloat16), wd_ref[cs, :],
                            preferred_element_type=jnp.float32)
        carry_ref[:, cs] = g[tm - SUBLANES:tm]

        @pl.when(t == t_last)
        def _emit():
            conv_ref[0, :, cs] = g[r_last:r_last + SUBLANES]

    y_ref[0] = _layer_norm(alpha * x1 + acc, g2_ref[...], b2_ref[...])


def _post(x, o, left8, t_valid, tm, w, alpha):
    b, tp, _ = x.shape
    d_ff = w["wg"].shape[1]
    assert d_ff % FF_CHUNK == 0 and t_valid % SUBLANES == 0 and t_valid >= SUBLANES
    t_last = (t_valid - 1) // tm
    r_last = (t_valid - SUBLANES) % tm
    row = lambda width: pl.BlockSpec((1, tm, width), lambda i, j: (i, j, 0))
    vec = lambda a: a.reshape(1, -1).astype(jnp.float32)
    consts = [w["wo"], vec(w["g1"]), vec(w["b1"]), w["wg"], w["wu"], w["cw"].astype(jnp.float32),
              vec(w["cb"]), w["wd"], vec(w["g2"]), vec(w["b2"])]
    kern = functools.partial(_post_kernel, tm=tm, d_ff=d_ff, alpha=alpha,
                             t_last=t_last, r_last=r_last)
    return pl.pallas_call(
        kern, grid=(b, tp // tm),
        in_specs=[row(D_MODEL), row(HD), pl.BlockSpec((1, SUBLANES, d_ff), lambda i, j: (i, 0, 0))]
                 + [_const_spec(c.shape) for c in consts],
        out_specs=[row(D_MODEL), pl.BlockSpec((1, SUBLANES, d_ff), lambda i, j: (i, 0, 0))],
        out_shape=[jax.ShapeDtypeStruct((b, tp, D_MODEL), jnp.float32),
                   jax.ShapeDtypeStruct((b, SUBLANES, d_ff), jnp.float32)],
        scratch_shapes=[pltpu.VMEM((SUBLANES, d_ff), jnp.float32)],
        compiler_params=pltpu.CompilerParams(
            dimension_semantics=("parallel", "arbitrary"), vmem_limit_bytes=VMEM_LIMIT),
        name="outproj_ln_convffn_ln",
    )(x, o, left8, *consts)


def _split_a(w_in):
    bf = jnp.bfloat16
    off_qi = 3 * HD
    off_ki = off_qi + IDX_QW
    wq, wk, wv = w_in[:, :HD], w_in[:, HD:2 * HD], w_in[:, 2 * HD:3 * HD]
    wqi = w_in[:, off_qi:off_ki]
    wkw = jnp.pad(w_in[:, off_ki:], ((0, 0), (0, LANES - IDX_DIM - IDX_HEADS)))
    row_scale = jnp.concatenate([
        jnp.ones((IDX_DIM,), jnp.float32),
        jnp.full((IDX_HEADS,), IDX_QW ** -0.5, jnp.float32),
        jnp.zeros((LANES - IDX_DIM - IDX_HEADS,), jnp.float32)]).reshape(LANES, 1)
    return [wq.T.astype(bf), wk.astype(bf), wv.astype(bf), wv.T.astype(bf), wqi.T.astype(bf),
            wkw.astype(bf), wkw.T.astype(bf), row_scale]


def _split_b(w_in):
    bf = jnp.bfloat16
    return [w_in[:, :HD].astype(bf), w_in[:, HD:2 * HD].astype(bf), w_in[:, 2 * HD:].astype(bf)]


def _with_cache(cache, new_bf, lp):
    b, p = cache.shape[:2]
    flat = cache.reshape(b, p, -1).astype(jnp.bfloat16)
    t = new_bf.shape[1]
    return jnp.pad(jnp.concatenate([flat, new_bf], axis=1), ((0, 0), (0, lp - p - t), (0, 0)))


def kernel(x_prompt, x_sample, cache_a_k, cache_a_v, cache_a_idx_k, cache_b_k, cache_b_v,
           state_ffn_conv, meta_tokens, rel_bias, w_a_in, w_a_out, w_b_in, w_b_out,
           ln1_g, ln1_b, ln2_g, ln2_b, w_ffn_gate, w_ffn_up, ffn_conv_w, ffn_conv_b, w_ffn_down):
    bp, seq, _ = x_prompt.shape
    bs, ts, _ = x_sample.shape
    depth = ln1_g.shape[0]
    d_ff = w_ffn_gate.shape[2]
    past = cache_a_k.shape[2]
    alpha = (2 * depth) ** 0.25
    t_p = N_META + seq
    tp_pad = _round_up(t_p, ROW_TILE)
    ls = past + ts
    ls_pad = _round_up(ls, ATTN_GROUP * KEY_TILE)
    assert past % KEY_TILE == 0 and ts % SUBLANES == 0 and ts <= SAMPLE_TILE
    topk_p = min(TOPK_MAX, seq // 4)
    topk_s = min(TOPK_MAX, ls // 4)
    bf = jnp.bfloat16

    meta = jnp.broadcast_to(meta_tokens.astype(x_prompt.dtype)[None], (bp, N_META, D_MODEL))
    xp = jnp.pad(jnp.concatenate([meta, x_prompt], axis=1), ((0, 0), (0, tp_pad - t_p), (0, 0)))
    xs = x_sample
    bias_p = _bias_tiles(rel_bias, ROW_TILE, KEY_TILE)
    bias_s = _bias_tiles(rel_bias, SAMPLE_TILE, KEY_TILE)
    left_p = jnp.zeros((bp, SUBLANES, d_ff), jnp.float32)

    heads = lambda a: a.reshape(a.shape[0], a.shape[1], N_HEADS, HEAD_DIM)
    outs = {k: [] for k in ("akp", "avp", "aip", "aks", "avs", "ais",
                            "bkp", "bvp", "bks", "bvs", "cp", "cs")}
    for i in range(depth):
        j = i // N_MIXERS
        if i % N_MIXERS == 0:
            wts = _split_a(w_a_in[j])
            qt, kc, kb, vc, vt, qit, kic, kib, kwt = _project(xp, t_p, ROW_TILE, wts, True)
            key_pad = _round_up(tp_pad, ATTN_GROUP * KEY_TILE) - tp_pad
            op = _attn_a(qt, qit, kwt, jnp.pad(kb, ((0, 0), (0, key_pad), (0, 0))),
                         jnp.pad(vt, ((0, 0), (0, 0), (0, key_pad))),
                         jnp.pad(kib, ((0, 0), (0, key_pad), (0, 0))), bias_p,
                         tq=ROW_TILE, tk=KEY_TILE, q_off=0, n_lead=N_META, l_valid=t_p, topk=topk_p)
            outs["akp"].append(heads(kc)); outs["avp"].append(heads(vc)); outs["aip"].append(kic)
            xs_pad = jnp.pad(xs, ((0, 0), (0, SAMPLE_TILE - ts), (0, 0)))
            qt, kc, kb, vc, vt, qit, kic, kib, kwt = _project(xs_pad, ts, SAMPLE_TILE, wts, True)
            cache_vt = jnp.swapaxes(cache_a_v[j].reshape(bs, past, HD).astype(bf), 1, 2)
            vt_all = jnp.pad(jnp.concatenate([cache_vt, vt[:, :, :ts]], axis=2),
                             ((0, 0), (0, 0), (0, ls_pad - ls)))
            os_ = _attn_a(qt, qit, kwt, _with_cache(cache_a_k[j], kb[:, :ts], ls_pad), vt_all,
                          _with_cache(cache_a_idx_k[j], kib[:, :ts], ls_pad), bias_s,
                          tq=SAMPLE_TILE, tk=KEY_TILE, q_off=past, n_lead=0, l_valid=ls,
                          topk=topk_s)[:, :ts]
            outs["aks"].append(heads(kc)); outs["avs"].append(heads(vc)); outs["ais"].append(kic)
            w_out = w_a_out[j]
        else:
            wts = _split_b(w_b_in[j])
            q, kc, kb, vc, vb = _project(xp, t_p, ROW_TILE, wts, False)
            op = _attn_b(q, kb, vb, tq=ROW_TILE, tk=KEY_TILE, q_off=0)
            outs["bkp"].append(heads(kc)); outs["bvp"].append(heads(vc))
            q, kc, kb, vc, vb = _project(xs, ts, ts, wts, False)
            os_ = _attn_b(q, _with_cache(cache_b_k[j], kb, ls_pad),
                          _with_cache(cache_b_v[j], vb, ls_pad), tq=ts, tk=KEY_TILE, q_off=past)
            outs["bks"].append(heads(kc)); outs["bvs"].append(heads(vc))
            w_out = w_b_out[j]
        w = dict(wo=w_out.astype(bf), g1=ln1_g[i], b1=ln1_b[i], wg=w_ffn_gate[i].astype(bf),
                 wu=w_ffn_up[i].astype(bf), cw=ffn_conv_w[i], cb=ffn_conv_b[i],
                 wd=w_ffn_down[i].astype(bf), g2=ln2_g[i], b2=ln2_b[i])
        xp, conv_p = _post(xp, op, left_p, t_p, ROW_TILE, w, alpha)
        left_s = jnp.pad(state_ffn_conv[i].astype(jnp.float32),
                         ((0, 0), (SUBLANES - (CONV_W - 1), 0), (0, 0)))
        xs, conv_s = _post(xs, os_, left_s, ts, ts, w, alpha)
        outs["cp"].append(conv_p[:, SUBLANES - (CONV_W - 1):])
        outs["cs"].append(conv_s[:, SUBLANES - (CONV_W - 1):])

    st = jnp.stack
    return (xp[:, N_META:t_p], xs,
            st(outs["akp"]), st(outs["avp"]), st(outs["aip"]),
            st(outs["bkp"]), st(outs["bvp"]), st(outs["cp"]),
            st(outs["aks"]), st(outs["avs"]), st(outs["ais"]),
            st(outs["bks"]), st(outs["bvs"]), st(outs["cs"]))
```

```python
import functools
import math

import jax
import jax.numpy as jnp
from jax import lax
from jax.experimental import pallas as pl
from jax.experimental.pallas import tpu as pltpu

D_MODEL = 1024
N_HEADS = 8
HEAD_DIM = 128
HD = N_HEADS * HEAD_DIM
IDX_HEADS = 8
IDX_DIM = 64
IDX_QW = IDX_HEADS * IDX_DIM
CHUNK = 64
CHUNK_SHIFT = 6
N_META = 16
N_MIXERS = 2
TOPK_MAX = 256
N_BUCKETS = 32
MAX_DISTANCE = 128
CONV_W = 3
LN_EPS = 1e-5

LANES = 128
SUBLANES = 8
ROW_TILE = 256
KEY_TILE = 256
FF_CHUNK = 256
ATTN_GROUP = 4
HEADS_PER_STEP = 2
SAMPLE_TILE = 128
VMEM_LIMIT = 56 * 1024 * 1024

NEG_BIG = -1e30
INT_MIN = -2 ** 31
KEY_NEG_INF = -2139095041
HALF_BITS = 16
HALF_BIAS = 1 << (HALF_BITS - 1)
HALF_MASK = (1 << HALF_BITS) - 1
PACKED_ROWS = 16
EXP_ZERO_BELOW = -104.0

_NT = (((1,), (1,)), ((), ()))


def _round_up(n, m):
    return -(-n // m) * m


def _layer_norm(x, g, b):
    mu = jnp.mean(x, -1, keepdims=True)
    xc = x - mu
    var = jnp.mean(xc * xc, -1, keepdims=True)
    return xc * lax.rsqrt(var + LN_EPS) * g + b


def _proj_a_kernel(x_ref, wqt_ref, wk_ref, wv_ref, wvt_ref, wqit_ref, wkw_ref, wkwt_ref, rs_ref,
                   qt_ref, kc_ref, kb_ref, vc_ref, vt_ref, qit_ref, kic_ref, kib_ref, kwt_ref):
    xb = x_ref[0].astype(jnp.bfloat16)

    def cols(wt_ref):
        return lax.dot_general(wt_ref[...], xb, _NT, preferred_element_type=jnp.float32)

    qt_ref[0] = (cols(wqt_ref) * (HEAD_DIM ** -0.5)).astype(jnp.bfloat16)
    k = jnp.dot(xb, wk_ref[...], preferred_element_type=jnp.float32)
    kc_ref[0] = k
    kb_ref[0] = k.astype(jnp.bfloat16)
    vc_ref[0] = jnp.dot(xb, wv_ref[...], preferred_element_type=jnp.float32)
    vt_ref[0] = cols(wvt_ref).astype(jnp.bfloat16)
    qit_ref[0] = cols(wqit_ref).astype(jnp.bfloat16)
    kw = jnp.dot(xb, wkw_ref[...], preferred_element_type=jnp.float32)
    kic_ref[0] = kw[:, :IDX_DIM]
    kib_ref[0] = kw[:, :IDX_DIM].astype(jnp.bfloat16)
    kwt_ref[0] = cols(wkwt_ref) * rs_ref[...]


def _proj_b_kernel(x_ref, wq_ref, wk_ref, wv_ref, q_ref, kc_ref, kb_ref, vc_ref, vb_ref):
    xb = x_ref[0].astype(jnp.bfloat16)
    q = jnp.dot(xb, wq_ref[...], preferred_element_type=jnp.float32)
    q_ref[0] = (q * (HEAD_DIM ** -0.5)).astype(jnp.bfloat16)
    k = jnp.dot(xb, wk_ref[...], preferred_element_type=jnp.float32)
    kc_ref[0] = k
    kb_ref[0] = k.astype(jnp.bfloat16)
    v = jnp.dot(xb, wv_ref[...], preferred_element_type=jnp.float32)
    vc_ref[0] = v
    vb_ref[0] = v.astype(jnp.bfloat16)


def _const_spec(shape):
    return pl.BlockSpec(shape, lambda *_: (0,) * len(shape))


def _project(x, t_valid, tm, weights, is_a):
    b, tp, _ = x.shape
    grid = (b, tp // tm)
    row = lambda w: pl.BlockSpec((1, tm, w), lambda i, j: (i, j, 0))
    bf = jnp.bfloat16
    f32 = jnp.float32
    pad = lambda w, dt: jax.ShapeDtypeStruct((b, tp, w), dt)
    exact = lambda w, dt: jax.ShapeDtypeStruct((b, t_valid, w), dt)
    in_specs = [row(D_MODEL)] + [_const_spec(w.shape) for w in weights]
    if is_a:
        col = lambda w: pl.BlockSpec((1, w, tm), lambda i, j: (i, 0, j))
        padt = lambda w, dt: jax.ShapeDtypeStruct((b, w, tp), dt)
        out_shape = [padt(HD, bf), exact(HD, f32), pad(HD, bf), exact(HD, f32), padt(HD, bf),
                     padt(IDX_QW, bf), exact(IDX_DIM, f32), pad(IDX_DIM, bf), padt(LANES, f32)]
        out_specs = [col(HD), row(HD), row(HD), row(HD), col(HD),
                     col(IDX_QW), row(IDX_DIM), row(IDX_DIM), col(LANES)]
        body = _proj_a_kernel
    else:
        out_shape = [pad(HD, bf), exact(HD, f32), pad(HD, bf), exact(HD, f32), pad(HD, bf)]
        out_specs = [row(HD)] * 5
        body = _proj_b_kernel
    return pl.pallas_call(
        body, grid=grid, in_specs=in_specs, out_specs=out_specs, out_shape=out_shape,
        compiler_params=pltpu.CompilerParams(
            dimension_semantics=("parallel", "parallel"), vmem_limit_bytes=VMEM_LIMIT),
        name="proj_a" if is_a else "proj_b",
    )(x, *weights)


def _vis_end(qpos, n_lead, l_valid):
    ve = jnp.where(qpos < n_lead, n_lead,
                   n_lead + ((((qpos - n_lead) >> CHUNK_SHIFT) + 1) << CHUNK_SHIFT))
    return jnp.minimum(ve, l_valid)


def _key_blocks(i, tq, tk, q_off, n_lead, l_valid):
    return (_vis_end(q_off + i * tq + tq - 1, n_lead, l_valid) + tk - 1) // tk


def _dsa_select_kernel(qit_ref, kwt_ref, kib_ref, madd_ref,
                       keys_ref, hi_ref, lo_ref, thr_ref, need_ref, cut_ref,
                       *, tq, tk, q_off, n_lead, l_valid, topk, idx_bits):
    i = pl.program_id(1)
    qpos0 = q_off + i * tq
    nkb = _key_blocks(i, tq, tk, q_off, n_lead, l_valid)
    g = tk // SUBLANES

    def blk_rows(j):
        return pl.ds(pl.multiple_of(j * g, g), g)

    def blk_keys(j):
        return pl.ds(pl.multiple_of(j * tk, tk), tk)

    g16 = tk // PACKED_ROWS

    def blk_halves(j):
        return pl.ds(pl.multiple_of(j * g16, g16), g16)

    qpos = qpos0 + lax.broadcasted_iota(jnp.int32, (1, tq), 1)
    vis_end = _vis_end(qpos, n_lead, l_valid)

    def score_block(j, carry):
        kib = kib_ref[0, blk_keys(j), :]
        sc = jnp.zeros((tk, tq), jnp.float32)
        for ih in range(IDX_HEADS):
            s = jnp.dot(kib, qit_ref[0, ih * IDX_DIM:(ih + 1) * IDX_DIM, :],
                        preferred_element_type=jnp.float32)
            sc = sc + jnp.maximum(s, 0.0) * kwt_ref[0, IDX_DIM + ih:IDX_DIM + ih + 1, :]
        sc = jnp.where(sc == 0.0, 0.0, sc)
        bits = pltpu.bitcast(sc, jnp.int32)
        key = jnp.where(bits < 0, bits ^ 0x7FFFFFFF, bits)
        pos = j * tk + lax.broadcasted_iota(jnp.int32, (tk, tq), 0)
        key = jnp.where(pos < vis_end, key, KEY_NEG_INF)
        keys_ref[blk_rows(j)] = key.reshape(g, SUBLANES, tq)
        hi_ref[blk_halves(j)] = (key >> HALF_BITS).astype(jnp.int16).reshape(g16, PACKED_ROWS, tq)
        return carry

    lax.fori_loop(0, nkb, score_block, 0)

    def key_pos(j):
        shape = (g, SUBLANES, tq)
        return (j * tk + lax.broadcasted_iota(jnp.int32, shape, 0) * SUBLANES
                + lax.broadcasted_iota(jnp.int32, shape, 1))

    def count(pred):
        def blk(j, cnt):
            hit = pred(keys_ref[blk_rows(j)], j).astype(jnp.int32)
            part = jnp.sum(hit.reshape(4, g // 4, SUBLANES, tq), axis=1)
            return cnt + jnp.sum(part, axis=0)
        cnt = lax.fori_loop(0, nkb, blk, jnp.zeros((SUBLANES, tq), jnp.int32))
        return jnp.broadcast_to(jnp.sum(cnt, axis=0, keepdims=True), (SUBLANES, tq))

    def count_half(half_ref, cand):
        cand16 = jnp.broadcast_to(cand, (PACKED_ROWS, tq)).astype(jnp.int16)

        def blk(j, cnt):
            hit = jnp.where(half_ref[blk_halves(j)] >= cand16, jnp.bfloat16(1), jnp.bfloat16(0))
            parts = [hit[a] for a in range(g16)]
            while len(parts) > 1:
                parts = [parts[a] + parts[a + 1] for a in range(0, len(parts), 2)]
            return cnt + parts[0].astype(jnp.float32)
        cnt = lax.fori_loop(0, nkb, blk, jnp.zeros((PACKED_ROWS, tq), jnp.float32))
        return jnp.sum(cnt, axis=0, keepdims=True).astype(jnp.int32)

    def kth_largest_half(half_ref, k):
        def step(b, ut):
            cand = ut | (jnp.int32(1) << (HALF_BITS - 1 - b))
            cnt = count_half(half_ref, cand - HALF_BIAS)
            return jnp.where(cnt >= k, cand, ut)
        return lax.fori_loop(0, HALF_BITS, step, jnp.zeros((1, tq), jnp.int32)) - HALF_BIAS

    thr_hi = kth_largest_half(hi_ref, topk)
    above = jnp.where(thr_hi == HALF_BIAS - 1, 0,
                      count_half(hi_ref, jnp.minimum(thr_hi + 1, HALF_BIAS - 1)))

    def low_halves(j, carry):
        kk = keys_ref[blk_rows(j)].reshape(tk, tq)
        lo = jnp.where((kk >> HALF_BITS) == thr_hi, (kk & HALF_MASK) - HALF_BIAS, -HALF_BIAS)
        lo_ref[blk_halves(j)] = lo.astype(jnp.int16).reshape(g16, PACKED_ROWS, tq)
        return carry

    lax.fori_loop(0, nkb, low_halves, 0)
    thr_lo = kth_largest_half(lo_ref, topk - above)
    thr = jnp.broadcast_to((thr_hi << HALF_BITS) | (thr_lo + HALF_BIAS), (SUBLANES, tq))
    cnt_gt = count(lambda kk, j: kk > thr)
    cnt_ge = count(lambda kk, j: kk >= thr)
    thr_ref[...] = thr
    need_ref[...] = topk - cnt_gt
    cut_ref[...] = jnp.full((SUBLANES, tq), 1 << idx_bits, jnp.int32)
    excess = (cnt_ge > topk) & (thr != KEY_NEG_INF)

    @pl.when(jnp.max(excess.astype(jnp.int32)) > 0)
    def _ties():
        need = need_ref[...]

        def cut_step(b, cut):
            cand = cut | (jnp.int32(1) << (idx_bits - 1 - b))
            cnt = count(lambda kk, j: (kk == thr) & (key_pos(j) < cand))
            return jnp.where(cnt <= need, cand, cut)

        cut_ref[...] = lax.fori_loop(0, idx_bits, cut_step, jnp.zeros((SUBLANES, tq), jnp.int32))

    def mask_block(j, carry):
        kk = keys_ref[blk_rows(j)]
        thr_v = thr_ref[...]
        sel = (((kk > thr_v) | ((kk == thr_v) & (key_pos(j) < cut_ref[...])))
               & (kk != KEY_NEG_INF))
        madd_ref[0, 0, blk_keys(j), :] = (
            jnp.where(sel, 0.0, NEG_BIG).reshape(tk, tq).astype(madd_ref.dtype))
        return carry

    lax.fori_loop(0, nkb, mask_block, 0)

    def mask_tail(j, carry):
        madd_ref[0, 0, blk_keys(j), :] = jnp.full((tk, tq), NEG_BIG, madd_ref.dtype)
        return carry

    lax.fori_loop(nkb, madd_ref.shape[2] // tk, mask_tail, 0)


def _dsa_attend_kernel(qt_ref, k_ref, vt_ref, madd_ref, bias_ref, o_ref, s0_ref, s1_ref, p1_ref,
                       *, tq, tk, q_off, n_lead, l_valid):
    i = pl.program_id(2)
    qpos0 = q_off + i * tq
    nkb = _key_blocks(i, tq, tk, q_off, n_lead, l_valid)
    wide = ATTN_GROUP * tk
    n_trips = (nkb + ATTN_GROUP - 1) // ATTN_GROUP
    jq = qpos0 // tk
    s_refs = (s0_ref, s1_ref)
    heads = [slice(hh * HEAD_DIM, (hh + 1) * HEAD_DIM) for hh in range(HEADS_PER_STEP)]

    def keys_of(t):
        return pl.ds(pl.multiple_of(t * wide, wide), wide)

    def scores(hh, t):
        s_refs[hh][...] = jnp.dot(k_ref[0, keys_of(t), heads[hh]], qt_ref[0, heads[hh], :],
                                  preferred_element_type=jnp.float32)

    def softmax(hh, t, m, l):
        parts = []
        for u in range(ATTN_GROUP):
            j = t * ATTN_GROUP + u
            d = jnp.clip(j - jq, -2, 1) + 2
            madd = madd_ref[0, 0, pl.ds(pl.multiple_of(j * tk, tk), tk), :]
            parts.append(s_refs[hh][u * tk:(u + 1) * tk] + bias_ref[d, hh]
                         + madd.astype(jnp.float32))
        m_new = m
        for part in parts:
            m_new = jnp.maximum(m_new, jnp.max(part, axis=0, keepdims=True))
        alpha = jnp.exp(m - m_new)
        p = [jnp.exp(part - m_new) for part in parts]
        l = alpha * l + sum(jnp.sum(pu, axis=0, keepdims=True) for pu in p)
        pb = jnp.concatenate([pu.astype(jnp.bfloat16) for pu in p], axis=0)
        return m_new, l, alpha, pb

    def weighted_values(hh, t, pb):
        return jnp.dot(vt_ref[0, heads[hh], keys_of(t)], pb,
                       preferred_element_type=jnp.float32)

    def trip(t, carries):
        (m0, l0, acc0), (m1, l1, acc1) = carries
        acc1 = acc1 + weighted_values(1, jnp.maximum(t - 1, 0), p1_ref[...])
        scores(1, t)
        m0, l0, alpha0, pb0 = softmax(0, t, m0, l0)
        acc0 = alpha0 * acc0 + weighted_values(0, t, pb0)
        scores(0, jnp.minimum(t + 1, n_trips - 1))
        m1, l1, alpha1, pb1 = softmax(1, t, m1, l1)
        p1_ref[...] = pb1
        return (m0, l0, acc0), (m1, l1, alpha1 * acc1)

    scores(0, 0)
    p1_ref[...] = jnp.zeros(p1_ref.shape, p1_ref.dtype)
    carries = tuple((jnp.full((1, tq), NEG_BIG, jnp.float32), jnp.zeros((1, tq), jnp.float32),
                     jnp.zeros((HEAD_DIM, tq), jnp.float32)) for _ in range(2))
    (_, l0, acc0), (_, l1, acc1) = lax.fori_loop(0, n_trips, trip, carries)
    acc1 = acc1 + weighted_values(1, n_trips - 1, p1_ref[...])
    o_ref[0, :, heads[0]] = (acc0 / l0).T.astype(o_ref.dtype)
    o_ref[0, :, heads[1]] = (acc1 / l1).T.astype(o_ref.dtype)


def _attn_a(qt, qit, kwt, k, vt, kib, bias, *, tq, tk, q_off, n_lead, l_valid, topk):
    b, _, tqp = qt.shape
    lp = k.shape[1]
    assert lp % (ATTN_GROUP * tk) == 0 and tk % (4 * SUBLANES) == 0 and tk >= topk
    assert tqp % tq == 0 and q_off % tk == 0 and HEADS_PER_STEP == 2
    n_q = tqp // tq
    hw = HEADS_PER_STEP * HEAD_DIM
    geom = dict(tq=tq, tk=tk, q_off=q_off, n_lead=n_lead, l_valid=l_valid)
    madd = pl.pallas_call(
        functools.partial(_dsa_select_kernel, topk=topk, idx_bits=lp.bit_length(), **geom),
        grid=(b, n_q),
        in_specs=[
            pl.BlockSpec((1, IDX_QW, tq), lambda bi, i: (bi, 0, i)),
            pl.BlockSpec((1, LANES, tq), lambda bi, i: (bi, 0, i)),
            pl.BlockSpec((1, lp, IDX_DIM), lambda bi, i: (bi, 0, 0)),
        ],
        out_specs=pl.BlockSpec((1, 1, lp, tq), lambda bi, i: (bi, i, 0, 0)),
        out_shape=jax.ShapeDtypeStruct((b, n_q, lp, tq), jnp.bfloat16),
        scratch_shapes=[
            pltpu.VMEM((lp // SUBLANES, SUBLANES, tq), jnp.int32),
            pltpu.VMEM((lp // PACKED_ROWS, PACKED_ROWS, tq), jnp.int16),
            pltpu.VMEM((lp // PACKED_ROWS, PACKED_ROWS, tq), jnp.int16),
            pltpu.VMEM((SUBLANES, tq), jnp.int32),
            pltpu.VMEM((SUBLANES, tq), jnp.int32),
            pltpu.VMEM((SUBLANES, tq), jnp.int32),
        ],
        compiler_params=pltpu.CompilerParams(
            dimension_semantics=("parallel", "parallel"), vmem_limit_bytes=VMEM_LIMIT),
        name="dsa_select",
    )(qit, kwt, kib)
    return pl.pallas_call(
        functools.partial(_dsa_attend_kernel, **geom),
        grid=(b, N_HEADS // HEADS_PER_STEP, n_q),
        in_specs=[
            pl.BlockSpec((1, hw, tq), lambda bi, h, i: (bi, h, i)),
            pl.BlockSpec((1, lp, hw), lambda bi, h, i: (bi, 0, h)),
            pl.BlockSpec((1, hw, lp), lambda bi, h, i: (bi, h, 0)),
            pl.BlockSpec((1, 1, lp, tq), lambda bi, h, i: (bi, i, 0, 0)),
            pl.BlockSpec((4, HEADS_PER_STEP, tk, tq), lambda bi, h, i: (0, h, 0, 0)),
        ],
        out_specs=pl.BlockSpec((1, tq, hw), lambda bi, h, i: (bi, i, h)),
        out_shape=jax.ShapeDtypeStruct((b, tqp, HD), jnp.bfloat16),
        scratch_shapes=[
            pltpu.VMEM((ATTN_GROUP * tk, tq), jnp.float32),
            pltpu.VMEM((ATTN_GROUP * tk, tq), jnp.float32),
            pltpu.VMEM((ATTN_GROUP * tk, tq), jnp.bfloat16),
        ],
        compiler_params=pltpu.CompilerParams(
            dimension_semantics=("parallel", "parallel", "parallel"),
            vmem_limit_bytes=VMEM_LIMIT),
        name="dsa_attend",
    )(qt, k, vt, madd, bias)


def _rel_bucket(rel):
    nb = N_BUCKETS // 2
    max_exact = nb // 2
    ret = jnp.where(rel > 0, nb, 0)
    n = jnp.abs(rel)
    nf = jnp.maximum(n, 1).astype(jnp.float32)
    large = max_exact + (jnp.log(nf / max_exact) / math.log(MAX_DISTANCE / max_exact)
                         * (nb - max_exact)).astype(jnp.int32)
    large = jnp.minimum(large, nb - 1)
    return ret + jnp.where(n < max_exact, n, large)


def _bias_tiles(rel_bias, tq, tk):
    w = tk + tq
    tiles = []
    for d in range(-2, 2):
        rel = d * tk + tk - 1 - jnp.arange(w, dtype=jnp.int32)
        vec = rel_bias.astype(jnp.float32)[_rel_bucket(rel)].T
        vec = jnp.roll(vec, -(tk - 1), axis=1)
        skew = jnp.tile(vec, (1, tk))[:, :tk * (w - 1)].reshape(-1, tk, w - 1)
        tiles.append(skew[:, :, :tq])
    return jnp.stack(tiles)


def _attn_b_kernel(q_ref, k_ref, v_ref, tri_ref, o_ref, *, tq, tk, q_off):
    i = pl.program_id(2)
    qpos0 = q_off + i * tq
    heads = [slice(hh * HEAD_DIM, (hh + 1) * HEAD_DIM) for hh in range(HEADS_PER_STEP)]
    tri = tri_ref[...]

    def block(j, diagonal, runs, accs):
        ds = pl.ds(pl.multiple_of(j * tk, tk), tk)
        if diagonal:
            causal = ((j * tk + lax.broadcasted_iota(jnp.int32, (tq, tk), 1))
                      < (qpos0 + lax.broadcasted_iota(jnp.int32, (tq, tk), 0)))
        zs = [lax.dot_general(q_ref[0, :, hs], k_ref[0, ds, hs], _NT,
                              preferred_element_type=jnp.float32) for hs in heads]
        tails, log_1ms, withins = [], [], []
        for z in zs:
            tail = jnp.log(1.0 + jnp.exp(-jnp.abs(z)))
            log_1m = -jnp.maximum(z, 0.0) - tail
            if diagonal:
                log_1m = jnp.where(causal, log_1m, 0.0)
            hi = log_1m.astype(jnp.bfloat16)
            lo = (log_1m - hi.astype(jnp.float32)).astype(jnp.bfloat16)
            withins.append(jnp.dot(hi, tri, preferred_element_type=jnp.float32)
                           + jnp.dot(lo, tri, preferred_element_type=jnp.float32))
            tails.append(tail)
            log_1ms.append(log_1m)
        new_runs, new_accs = [], []
        for hh, hs in enumerate(heads):
            a = jnp.exp(jnp.minimum(zs[hh], 0.0) - tails[hh] + withins[hh] + runs[hh])
            if diagonal:
                a = jnp.where(causal, a, 0.0)
            new_accs.append(accs[hh] + jnp.dot(a.astype(jnp.bfloat16), v_ref[0, ds, hs],
                                               preferred_element_type=jnp.float32))
            new_runs.append(runs[hh] + withins[hh][:, 0:1] + log_1ms[hh][:, 0:1])
        return tuple(new_runs), tuple(new_accs)

    def cond(carry):
        j, runs, _ = carry
        return (j >= 0) & (jnp.max(jnp.maximum(*runs)) > EXP_ZERO_BELOW)

    def body(carry):
        j, runs, accs = carry
        runs, accs = block(j, False, runs, accs)
        return j - 1, runs, accs

    j0 = (qpos0 + tq - 2) // tk
    runs = tuple(jnp.zeros((tq, 1), jnp.float32) for _ in heads)
    accs = tuple(jnp.zeros((tq, HEAD_DIM), jnp.float32) for _ in heads)
    runs, accs = block(j0, True, runs, accs)
    _, _, accs = lax.while_loop(cond, body, (j0 - 1, runs, accs))
    for hs, acc in zip(heads, accs):
        o_ref[0, :, hs] = acc.astype(o_ref.dtype)


def _attn_b(q, k, v, *, tq, tk, q_off):
    b, tqp, _ = q.shape
    lp = k.shape[1]
    tri = (jnp.arange(tk)[:, None] > jnp.arange(tk)[None, :]).astype(jnp.bfloat16)
    assert q_off % tk == 0 and tq <= tk and (tq == tk or tqp == tq)
    hw = HEADS_PER_STEP * HEAD_DIM
    kern = functools.partial(_attn_b_kernel, tq=tq, tk=tk, q_off=q_off)
    return pl.pallas_call(
        kern, grid=(b, N_HEADS // HEADS_PER_STEP, tqp // tq),
        in_specs=[
            pl.BlockSpec((1, tq, hw), lambda bi, h, i: (bi, i, h)),
            pl.BlockSpec((1, lp, hw), lambda bi, h, i: (bi, 0, h)),
            pl.BlockSpec((1, lp, hw), lambda bi, h, i: (bi, 0, h)),
            _const_spec((tk, tk)),
        ],
        out_specs=pl.BlockSpec((1, tq, hw), lambda bi, h, i: (bi, i, h)),
        out_shape=jax.ShapeDtypeStruct((b, tqp, HD), jnp.bfloat16),
        compiler_params=pltpu.CompilerParams(
            dimension_semantics=("parallel", "parallel", "parallel"),
            vmem_limit_bytes=VMEM_LIMIT),
        name="stick_breaking_attention",
    )(q, k, v, tri)


def _post_kernel(x_ref, o_ref, left_ref, wo_ref, g1_ref, b1_ref, wg_ref, wu_ref, cw_ref, cb_ref,
                 wd_ref, g2_ref, b2_ref, y_ref, conv_ref, carry_ref,
                 *, tm, d_ff, alpha, r_last):
    t = pl.program_id(1)

    @pl.when(t == 0)
    def _init():
        carry_ref[...] = left_ref[0]

    mix = jnp.dot(o_ref[0], wo_ref[...], preferred_element_type=jnp.float32)
    x1 = _layer_norm(alpha * x_ref[0] + mix, g1_ref[...], b1_ref[...])
    xb = x1.astype(jnp.bfloat16)
    row = lax.broadcasted_iota(jnp.int32, (tm, FF_CHUNK), 0)
    acc = jnp.zeros((tm, D_MODEL), jnp.float32)
    n_chunks = d_ff // FF_CHUNK

    def gate_up(c):
        cs = slice(c * FF_CHUNK, (c + 1) * FF_CHUNK)
        return (jnp.dot(xb, wg_ref[:, cs], preferred_element_type=jnp.float32),
                jnp.dot(xb, wu_ref[:, cs], preferred_element_type=jnp.float32))

    nxt = gate_up(0)
    for c in range(n_chunks):
        cs = slice(c * FF_CHUNK, (c + 1) * FF_CHUNK)
        g, u = nxt
        if c + 1 < n_chunks:
            nxt = gate_up(c + 1)
        prev = carry_ref[:, cs]
        g1 = jnp.where(row == 0, prev[7:8], pltpu.roll(g, 1, 0))
        g2 = jnp.where(row == 0, prev[6:7], jnp.where(row == 1, prev[7:8], pltpu.roll(g, 2, 0)))
        gc = cb_ref[:, cs] + cw_ref[0:1, cs] * g2
        gc = gc + cw_ref[1:2, cs] * g1
        gc = gc + cw_ref[2:3, cs] * g
        hid = jax.nn.gelu(gc) * u
        acc = acc + jnp.dot(hid.astype(jnp.bfloat16), wd_ref[cs, :],
                            preferred_element_type=jnp.float32)
        carry_ref[:, cs] = g[tm - SUBLANES:tm]
        conv_ref[0, :, cs] = g[r_last:r_last + SUBLANES]

    y_ref[0] = _layer_norm(alpha * x1 + acc, g2_ref[...], b2_ref[...])


def _post(x, o, left8, t_valid, tm, w, alpha):
    b, tp, _ = x.shape
    d_ff = w["wg"].shape[1]
    assert d_ff % FF_CHUNK == 0 and t_valid % SUBLANES == 0 and t_valid >= SUBLANES
    assert (t_valid - 1) // tm == tp // tm - 1
    r_last = (t_valid - SUBLANES) % tm
    row = lambda width: pl.BlockSpec((1, tm, width), lambda i, j: (i, j, 0))
    vec = lambda a: a.reshape(1, -1).astype(jnp.float32)
    consts = [w["wo"], vec(w["g1"]), vec(w["b1"]), w["wg"], w["wu"], w["cw"].astype(jnp.float32),
              vec(w["cb"]), w["wd"], vec(w["g2"]), vec(w["b2"])]
    kern = functools.partial(_post_kernel, tm=tm, d_ff=d_ff, alpha=alpha, r_last=r_last)
    return pl.pallas_call(
        kern, grid=(b, tp // tm),
        in_specs=[row(D_MODEL), row(HD), pl.BlockSpec((1, SUBLANES, d_ff), lambda i, j: (i, 0, 0))]
                 + [_const_spec(c.shape) for c in consts],
        out_specs=[row(D_MODEL), pl.BlockSpec((1, SUBLANES, d_ff), lambda i, j: (i, 0, 0))],
        out_shape=[jax.ShapeDtypeStruct((b, tp, D_MODEL), jnp.float32),
                   jax.ShapeDtypeStruct((b, SUBLANES, d_ff), jnp.float32)],
        scratch_shapes=[pltpu.VMEM((SUBLANES, d_ff), jnp.float32)],
        compiler_params=pltpu.CompilerParams(
            dimension_semantics=("parallel", "arbitrary"), vmem_limit_bytes=VMEM_LIMIT),
        name="outproj_ln_convffn_ln",
    )(x, o, left8, *consts)


def _split_a(w_in):
    bf = jnp.bfloat16
    off_qi = 3 * HD
    off_ki = off_qi + IDX_QW
    wq, wk, wv = w_in[:, :HD], w_in[:, HD:2 * HD], w_in[:, 2 * HD:3 * HD]
    wqi = w_in[:, off_qi:off_ki]
    wkw = jnp.pad(w_in[:, off_ki:], ((0, 0), (0, LANES - IDX_DIM - IDX_HEADS)))
    row_scale = jnp.concatenate([
        jnp.ones((IDX_DIM,), jnp.float32),
        jnp.full((IDX_HEADS,), IDX_QW ** -0.5, jnp.float32),
        jnp.zeros((LANES - IDX_DIM - IDX_HEADS,), jnp.float32)]).reshape(LANES, 1)
    return [wq.T.astype(bf), wk.astype(bf), wv.astype(bf), wv.T.astype(bf), wqi.T.astype(bf),
            wkw.astype(bf), wkw.T.astype(bf), row_scale]


def _split_b(w_in):
    bf = jnp.bfloat16
    return [w_in[:, :HD].astype(bf), w_in[:, HD:2 * HD].astype(bf), w_in[:, 2 * HD:].astype(bf)]


def _with_cache(cache, new_bf, lp):
    b, p = cache.shape[:2]
    flat = cache.reshape(b, p, -1).astype(jnp.bfloat16)
    t = new_bf.shape[1]
    return jnp.pad(jnp.concatenate([flat, new_bf], axis=1), ((0, 0), (0, lp - p - t), (0, 0)))


def kernel(x_prompt, x_sample, cache_a_k, cache_a_v, cache_a_idx_k, cache_b_k, cache_b_v,
           state_ffn_conv, meta_tokens, rel_bias, w_a_in, w_a_out, w_b_in, w_b_out,
           ln1_g, ln1_b, ln2_g, ln2_b, w_ffn_gate, w_ffn_up, ffn_conv_w, ffn_conv_b, w_ffn_down):
    bp, seq, _ = x_prompt.shape
    bs, ts, _ = x_sample.shape
    depth = ln1_g.shape[0]
    d_ff = w_ffn_gate.shape[2]
    past = cache_a_k.shape[2]
    alpha = (2 * depth) ** 0.25
    t_p = N_META + seq
    tp_pad = _round_up(t_p, ROW_TILE)
    ls = past + ts
    ls_pad = _round_up(ls, ATTN_GROUP * KEY_TILE)
    assert past % KEY_TILE == 0 and ts % SUBLANES == 0 and ts <= SAMPLE_TILE
    topk_p = min(TOPK_MAX, seq // 4)
    topk_s = min(TOPK_MAX, ls // 4)
    bf = jnp.bfloat16

    meta = jnp.broadcast_to(meta_tokens.astype(x_prompt.dtype)[None], (bp, N_META, D_MODEL))
    xp = jnp.pad(jnp.concatenate([meta, x_prompt], axis=1), ((0, 0), (0, tp_pad - t_p), (0, 0)))
    xs = x_sample
    bias_p = _bias_tiles(rel_bias, ROW_TILE, KEY_TILE)
    bias_s = _bias_tiles(rel_bias, SAMPLE_TILE, KEY_TILE)
    left_p = jnp.zeros((bp, SUBLANES, d_ff), jnp.float32)

    heads = lambda a: a.reshape(a.shape[0], a.shape[1], N_HEADS, HEAD_DIM)
    outs = {k: [] for k in ("akp", "avp", "aip", "aks", "avs", "ais",
                            "bkp", "bvp", "bks", "bvs", "cp", "cs")}
    for i in range(depth):
        j = i // N_MIXERS
        if i % N_MIXERS == 0:
            wts = _split_a(w_a_in[j])
            qt, kc, kb, vc, vt, qit, kic, kib, kwt = _project(xp, t_p, ROW_TILE, wts, True)
            key_pad = _round_up(tp_pad, ATTN_GROUP * KEY_TILE) - tp_pad
            op = _attn_a(qt, qit, kwt, jnp.pad(kb, ((0, 0), (0, key_pad), (0, 0))),
                         jnp.pad(vt, ((0, 0), (0, 0), (0, key_pad))),
                         jnp.pad(kib, ((0, 0), (0, key_pad), (0, 0))), bias_p,
                         tq=ROW_TILE, tk=KEY_TILE, q_off=0, n_lead=N_META, l_valid=t_p, topk=topk_p)
            outs["akp"].append(heads(kc)); outs["avp"].append(heads(vc)); outs["aip"].append(kic)
            xs_pad = jnp.pad(xs, ((0, 0), (0, SAMPLE_TILE - ts), (0, 0)))
            qt, kc, kb, vc, vt, qit, kic, kib, kwt = _project(xs_pad, ts, SAMPLE_TILE, wts, True)
            cache_vt = jnp.swapaxes(cache_a_v[j].reshape(bs, past, HD).astype(bf), 1, 2)
            vt_all = jnp.pad(jnp.concatenate([cache_vt, vt[:, :, :ts]], axis=2),
                             ((0, 0), (0, 0), (0, ls_pad - ls)))
            os_ = _attn_a(qt, qit, kwt, _with_cache(cache_a_k[j], kb[:, :ts], ls_pad), vt_all,
                          _with_cache(cache_a_idx_k[j], kib[:, :ts], ls_pad), bias_s,
                          tq=SAMPLE_TILE, tk=KEY_TILE, q_off=past, n_lead=0, l_valid=ls,
                          topk=topk_s)[:, :ts]
            outs["aks"].append(heads(kc)); outs["avs"].append(heads(vc)); outs["ais"].append(kic)
            w_out = w_a_out[j]
        else:
            wts = _split_b(w_b_in[j])
            q, kc, kb, vc, vb = _project(xp, t_p, ROW_TILE, wts, False)
            op = _attn_b(q, kb, vb, tq=ROW_TILE, tk=KEY_TILE, q_off=0)
            outs["bkp"].append(heads(kc)); outs["bvp"].append(heads(vc))
            q, kc, kb, vc, vb = _project(xs, ts, ts, wts, False)
            os_ = _attn_b(q, _with_cache(cache_b_k[j], kb, ls_pad),
                          _with_cache(cache_b_v[j], vb, ls_pad), tq=ts, tk=KEY_TILE, q_off=past)
            outs["bks"].append(heads(kc)); outs["bvs"].append(heads(vc))
            w_out = w_b_out[j]
        w = dict(wo=w_out.astype(bf), g1=ln1_g[i], b1=ln1_b[i], wg=w_ffn_gate[i].astype(bf),
                 wu=w_ffn_up[i].astype(bf), cw=ffn_conv_w[i], cb=ffn_conv_b[i],
                 wd=w_ffn_down[i].astype(bf), g2=ln2_g[i], b2=ln2_b[i])
        xp, conv_p = _post(xp, op, left_p, t_p, ROW_TILE, w, alpha)
        left_s = jnp.pad(state_ffn_conv[i].astype(jnp.float32),
                         ((0, 0), (SUBLANES - (CONV_W - 1), 0), (0, 0)))
        xs, conv_s = _post(xs, os_, left_s, ts, ts, w, alpha)
        outs["cp"].append(conv_p[:, SUBLANES - (CONV_W - 1):])
        outs["cs"].append(conv_s[:, SUBLANES - (CONV_W - 1):])

    st = jnp.stack
    return (xp[:, N_META:t_p], xs,
            st(outs["akp"]), st(outs["avp"]), st(outs["aip"]),
            st(outs["bkp"]), st(outs["bvp"]), st(outs["cp"]),
            st(outs["aks"]), st(outs["avs"]), st(outs["ais"]),
            st(outs["bks"]), st(outs["bvs"]), st(outs["cs"]))
```

```python
import functools
import math

import jax
import jax.numpy as jnp
from jax import lax
from jax.experimental import pallas as pl
from jax.experimental.pallas import tpu as pltpu

D_MODEL = 1024
N_HEADS = 8
HEAD_DIM = 128
HD = N_HEADS * HEAD_DIM
IDX_HEADS = 8
IDX_DIM = 64
IDX_QW = IDX_HEADS * IDX_DIM
CHUNK = 64
CHUNK_SHIFT = 6
N_META = 16
N_MIXERS = 2
TOPK_MAX = 256
N_BUCKETS = 32
MAX_DISTANCE = 128
CONV_W = 3
LN_EPS = 1e-5

LANES = 128
SUBLANES = 8
ROW_TILE = 256
KEY_TILE = 256
FF_CHUNK = 256
ATTN_GROUP = 4
HEADS_PER_STEP = 2
SB_HEADS_PER_STEP = 4
SAMPLE_TILE = 128
VMEM_LIMIT = 56 * 1024 * 1024

NEG_BIG = -1e30
INT_MIN = -2 ** 31
KEY_NEG_INF = -2139095041
HALF_BITS = 16
HALF_BIAS = 1 << (HALF_BITS - 1)
HALF_MASK = (1 << HALF_BITS) - 1
PACKED_ROWS = 16
EXP_ZERO_BELOW = -104.0

_NT = (((1,), (1,)), ((), ()))


def _round_up(n, m):
    return -(-n // m) * m


def _layer_norm(x, g, b):
    mu = jnp.mean(x, -1, keepdims=True)
    xc = x - mu
    var = jnp.mean(xc * xc, -1, keepdims=True)
    return xc * lax.rsqrt(var + LN_EPS) * g + b


def _proj_a_kernel(x_ref, wqt_ref, wk_ref, wv_ref, wvt_ref, wqit_ref, wkw_ref, wkwt_ref, rs_ref,
                   qt_ref, kc_ref, kb_ref, vc_ref, vt_ref, qit_ref, kic_ref, kib_ref, kwt_ref):
    xb = x_ref[0].astype(jnp.bfloat16)

    def cols(wt_ref):
        return lax.dot_general(wt_ref[...], xb, _NT, preferred_element_type=jnp.float32)

    qt_ref[0] = (cols(wqt_ref) * (HEAD_DIM ** -0.5)).astype(jnp.bfloat16)
    k = jnp.dot(xb, wk_ref[...], preferred_element_type=jnp.float32)
    kc_ref[0] = k
    kb_ref[0] = k.astype(jnp.bfloat16)
    vc_ref[0] = jnp.dot(xb, wv_ref[...], preferred_element_type=jnp.float32)
    vt_ref[0] = cols(wvt_ref).astype(jnp.bfloat16)
    qit_ref[0] = cols(wqit_ref).astype(jnp.bfloat16)
    kw = jnp.dot(xb, wkw_ref[...], preferred_element_type=jnp.float32)
    kic_ref[0] = kw[:, :IDX_DIM]
    kib_ref[0] = kw[:, :IDX_DIM].astype(jnp.bfloat16)
    kwt_ref[0] = cols(wkwt_ref) * rs_ref[...]


def _proj_b_kernel(x_ref, wq_ref, wk_ref, wv_ref, q_ref, kc_ref, kb_ref, vc_ref, vb_ref):
    xb = x_ref[0].astype(jnp.bfloat16)
    q = jnp.dot(xb, wq_ref[...], preferred_element_type=jnp.float32)
    q_ref[0] = (q * (HEAD_DIM ** -0.5)).astype(jnp.bfloat16)
    k = jnp.dot(xb, wk_ref[...], preferred_element_type=jnp.float32)
    kc_ref[0] = k
    kb_ref[0] = k.astype(jnp.bfloat16)
    v = jnp.dot(xb, wv_ref[...], preferred_element_type=jnp.float32)
    vc_ref[0] = v
    vb_ref[0] = v.astype(jnp.bfloat16)


def _const_spec(shape):
    return pl.BlockSpec(shape, lambda *_: (0,) * len(shape))


def _project(x, t_valid, tm, weights, is_a):
    b, tp, _ = x.shape
    grid = (b, tp // tm)
    row = lambda w: pl.BlockSpec((1, tm, w), lambda i, j: (i, j, 0))
    bf = jnp.bfloat16
    f32 = jnp.float32
    pad = lambda w, dt: jax.ShapeDtypeStruct((b, tp, w), dt)
    exact = lambda w, dt: jax.ShapeDtypeStruct((b, t_valid, w), dt)
    in_specs = [row(D_MODEL)] + [_const_spec(w.shape) for w in weights]
    if is_a:
        col = lambda w: pl.BlockSpec((1, w, tm), lambda i, j: (i, 0, j))
        padt = lambda w, dt: jax.ShapeDtypeStruct((b, w, tp), dt)
        out_shape = [padt(HD, bf), exact(HD, f32), pad(HD, bf), exact(HD, f32), padt(HD, bf),
                     padt(IDX_QW, bf), exact(IDX_DIM, f32), pad(IDX_DIM, bf), padt(LANES, f32)]
        out_specs = [col(HD), row(HD), row(HD), row(HD), col(HD),
                     col(IDX_QW), row(IDX_DIM), row(IDX_DIM), col(LANES)]
        body = _proj_a_kernel
    else:
        out_shape = [pad(HD, bf), exact(HD, f32), pad(HD, bf), exact(HD, f32), pad(HD, bf)]
        out_specs = [row(HD)] * 5
        body = _proj_b_kernel
    return pl.pallas_call(
        body, grid=grid, in_specs=in_specs, out_specs=out_specs, out_shape=out_shape,
        compiler_params=pltpu.CompilerParams(
            dimension_semantics=("parallel", "parallel"), vmem_limit_bytes=VMEM_LIMIT),
        name="proj_a" if is_a else "proj_b",
    )(x, *weights)


def _vis_end(qpos, n_lead, l_valid):
    ve = jnp.where(qpos < n_lead, n_lead,
                   n_lead + ((((qpos - n_lead) >> CHUNK_SHIFT) + 1) << CHUNK_SHIFT))
    return jnp.minimum(ve, l_valid)


def _key_blocks(i, tq, tk, q_off, n_lead, l_valid):
    return (_vis_end(q_off + i * tq + tq - 1, n_lead, l_valid) + tk - 1) // tk


def _dsa_select_kernel(qit_ref, kwt_ref, kib_ref, madd_ref,
                       keys_ref, hi_ref, lo_ref, thr_ref, need_ref, cut_ref,
                       *, tq, tk, q_off, n_lead, l_valid, topk, idx_bits):
    i = pl.program_id(1)
    qpos0 = q_off + i * tq
    nkb = _key_blocks(i, tq, tk, q_off, n_lead, l_valid)
    g = tk // SUBLANES

    def blk_rows(j):
        return pl.ds(pl.multiple_of(j * g, g), g)

    def blk_keys(j):
        return pl.ds(pl.multiple_of(j * tk, tk), tk)

    g16 = tk // PACKED_ROWS

    def blk_halves(j):
        return pl.ds(pl.multiple_of(j * g16, g16), g16)

    qpos = qpos0 + lax.broadcasted_iota(jnp.int32, (1, tq), 1)
    vis_end = _vis_end(qpos, n_lead, l_valid)

    def score_block(j, carry):
        kib = kib_ref[0, blk_keys(j), :]
        sc = jnp.zeros((tk, tq), jnp.float32)
        for ih in range(IDX_HEADS):
            s = jnp.dot(kib, qit_ref[0, ih * IDX_DIM:(ih + 1) * IDX_DIM, :],
                        preferred_element_type=jnp.float32)
            sc = sc + jnp.maximum(s, 0.0) * kwt_ref[0, IDX_DIM + ih:IDX_DIM + ih + 1, :]
        sc = jnp.where(sc == 0.0, 0.0, sc)
        bits = pltpu.bitcast(sc, jnp.int32)
        key = jnp.where(bits < 0, bits ^ 0x7FFFFFFF, bits)
        pos = j * tk + lax.broadcasted_iota(jnp.int32, (tk, tq), 0)
        key = jnp.where(pos < vis_end, key, KEY_NEG_INF)
        keys_ref[blk_rows(j)] = key.reshape(g, SUBLANES, tq)
        hi_ref[blk_halves(j)] = (key >> HALF_BITS).astype(jnp.int16).reshape(g16, PACKED_ROWS, tq)
        return carry

    lax.fori_loop(0, nkb, score_block, 0)

    def key_pos(j):
        shape = (g, SUBLANES, tq)
        return (j * tk + lax.broadcasted_iota(jnp.int32, shape, 0) * SUBLANES
                + lax.broadcasted_iota(jnp.int32, shape, 1))

    def count(pred):
        def blk(j, cnt):
            hit = pred(keys_ref[blk_rows(j)], j).astype(jnp.int32)
            part = jnp.sum(hit.reshape(4, g // 4, SUBLANES, tq), axis=1)
            return cnt + jnp.sum(part, axis=0)
        cnt = lax.fori_loop(0, nkb, blk, jnp.zeros((SUBLANES, tq), jnp.int32))
        return jnp.broadcast_to(jnp.sum(cnt, axis=0, keepdims=True), (SUBLANES, tq))

    def count_half(half_ref, cand):
        cand16 = jnp.broadcast_to(cand, (PACKED_ROWS, tq)).astype(jnp.int16)

        def blk(j, cnt):
            hit = jnp.where(half_ref[blk_halves(j)] >= cand16, jnp.bfloat16(1), jnp.bfloat16(0))
            parts = [hit[a] for a in range(g16)]
            while len(parts) > 1:
                parts = [parts[a] + parts[a + 1] for a in range(0, len(parts), 2)]
            return cnt + parts[0].astype(jnp.float32)
        cnt = lax.fori_loop(0, nkb, blk, jnp.zeros((PACKED_ROWS, tq), jnp.float32))
        return jnp.sum(cnt, axis=0, keepdims=True).astype(jnp.int32)

    def kth_largest_half(half_ref, k):
        def step(b, ut):
            cand = ut | (jnp.int32(1) << (HALF_BITS - 1 - b))
            cnt = count_half(half_ref, cand - HALF_BIAS)
            return jnp.where(cnt >= k, cand, ut)
        return lax.fori_loop(0, HALF_BITS, step, jnp.zeros((1, tq), jnp.int32)) - HALF_BIAS

    thr_hi = kth_largest_half(hi_ref, topk)
    above = jnp.where(thr_hi == HALF_BIAS - 1, 0,
                      count_half(hi_ref, jnp.minimum(thr_hi + 1, HALF_BIAS - 1)))

    def low_halves(j, carry):
        kk = keys_ref[blk_rows(j)].reshape(tk, tq)
        lo = jnp.where((kk >> HALF_BITS) == thr_hi, (kk & HALF_MASK) - HALF_BIAS, -HALF_BIAS)
        lo_ref[blk_halves(j)] = lo.astype(jnp.int16).reshape(g16, PACKED_ROWS, tq)
        return carry

    lax.fori_loop(0, nkb, low_halves, 0)
    thr_lo = kth_largest_half(lo_ref, topk - above)
    thr = jnp.broadcast_to((thr_hi << HALF_BITS) | (thr_lo + HALF_BIAS), (SUBLANES, tq))
    cnt_gt = count(lambda kk, j: kk > thr)
    cnt_ge = count(lambda kk, j: kk >= thr)
    thr_ref[...] = thr
    need_ref[...] = topk - cnt_gt
    cut_ref[...] = jnp.full((SUBLANES, tq), 1 << idx_bits, jnp.int32)
    excess = (cnt_ge > topk) & (thr != KEY_NEG_INF)

    @pl.when(jnp.max(excess.astype(jnp.int32)) > 0)
    def _ties():
        need = need_ref[...]

        def cut_step(b, cut):
            cand = cut | (jnp.int32(1) << (idx_bits - 1 - b))
            cnt = count(lambda kk, j: (kk == thr) & (key_pos(j) < cand))
            return jnp.where(cnt <= need, cand, cut)

        cut_ref[...] = lax.fori_loop(0, idx_bits, cut_step, jnp.zeros((SUBLANES, tq), jnp.int32))

    def mask_block(j, carry):
        kk = keys_ref[blk_rows(j)]
        thr_v = thr_ref[...]
        sel = (((kk > thr_v) | ((kk == thr_v) & (key_pos(j) < cut_ref[...])))
               & (kk != KEY_NEG_INF))
        madd_ref[0, 0, blk_keys(j), :] = (
            jnp.where(sel, 0.0, NEG_BIG).reshape(tk, tq).astype(madd_ref.dtype))
        return carry

    lax.fori_loop(0, nkb, mask_block, 0)

    def mask_tail(j, carry):
        madd_ref[0, 0, blk_keys(j), :] = jnp.full((tk, tq), NEG_BIG, madd_ref.dtype)
        return carry

    lax.fori_loop(nkb, madd_ref.shape[2] // tk, mask_tail, 0)


def _dsa_attend_kernel(qt_ref, k_ref, vt_ref, madd_ref, bias_ref, o_ref, s0_ref, s1_ref, p1_ref,
                       *, tq, tk, q_off, n_lead, l_valid):
    i = pl.program_id(2)
    qpos0 = q_off + i * tq
    nkb = _key_blocks(i, tq, tk, q_off, n_lead, l_valid)
    wide = ATTN_GROUP * tk
    n_trips = (nkb + ATTN_GROUP - 1) // ATTN_GROUP
    jq = qpos0 // tk
    s_refs = (s0_ref, s1_ref)
    heads = [slice(hh * HEAD_DIM, (hh + 1) * HEAD_DIM) for hh in range(HEADS_PER_STEP)]

    def keys_of(t):
        return pl.ds(pl.multiple_of(t * wide, wide), wide)

    def scores(hh, t):
        s_refs[hh][...] = jnp.dot(k_ref[0, keys_of(t), heads[hh]], qt_ref[0, heads[hh], :],
                                  preferred_element_type=jnp.float32)

    def softmax(hh, t, m, near):
        parts = []
        for u in range(ATTN_GROUP):
            j = t * ATTN_GROUP + u
            madd = madd_ref[0, 0, pl.ds(pl.multiple_of(j * tk, tk), tk), :]
            part = s_refs[hh][u * tk:(u + 1) * tk] + madd.astype(jnp.float32)
            if near:
                part = part + bias_ref[jnp.clip(j - jq, -2, 1) + 2, hh]
            parts.append(part)
        m_new = m
        for part in parts:
            m_new = jnp.maximum(m_new, jnp.max(part, axis=0, keepdims=True))
        alpha = jnp.exp(m - m_new)
        pb = jnp.concatenate([jnp.exp(part - m_new).astype(jnp.bfloat16) for part in parts], axis=0)
        return m_new, alpha, pb

    ones = jnp.ones((SUBLANES, wide), jnp.bfloat16)

    def weighted_values(hh, t, pb):
        num = jnp.dot(vt_ref[0, heads[hh], keys_of(t)], pb, preferred_element_type=jnp.float32)
        den = jnp.dot(ones, pb, preferred_element_type=jnp.float32)
        return jnp.concatenate([num, den], axis=0)

    def trip(t, carries, near):
        (m0, acc0), (m1, acc1) = carries
        acc1 = acc1 + weighted_values(1, jnp.maximum(t - 1, 0), p1_ref[...])
        scores(1, t)
        m0, alpha0, pb0 = softmax(0, t, m0, near)
        acc0 = alpha0 * acc0 + weighted_values(0, t, pb0)
        scores(0, jnp.minimum(t + 1, n_trips - 1))
        m1, alpha1, pb1 = softmax(1, t, m1, near)
        p1_ref[...] = pb1
        return (m0, acc0), (m1, alpha1 * acc1)

    scores(0, 0)
    p1_ref[...] = jnp.zeros(p1_ref.shape, p1_ref.dtype)
    carries = tuple((jnp.full((1, tq), NEG_BIG, jnp.float32),
                     jnp.zeros((HEAD_DIM + SUBLANES, tq), jnp.float32)) for _ in range(2))
    n_far = jnp.clip((jq - 1) // ATTN_GROUP, 0, n_trips)
    carries = lax.fori_loop(0, n_far, functools.partial(trip, near=False), carries)
    (_, acc0), (_, acc1) = lax.fori_loop(n_far, n_trips, functools.partial(trip, near=True), carries)
    acc1 = acc1 + weighted_values(1, n_trips - 1, p1_ref[...])
    for hs, acc in zip(heads, (acc0, acc1)):
        o_ref[0, :, hs] = (acc[:HEAD_DIM] / acc[HEAD_DIM:HEAD_DIM + 1]).T.astype(o_ref.dtype)


def _attn_a(qt, qit, kwt, k, vt, kib, bias, *, tq, tk, q_off, n_lead, l_valid, topk):
    b, _, tqp = qt.shape
    lp = k.shape[1]
    assert lp % (ATTN_GROUP * tk) == 0 and tk % (4 * SUBLANES) == 0 and tk >= topk
    assert tqp % tq == 0 and q_off % tk == 0 and HEADS_PER_STEP == 2
    n_q = tqp // tq
    hw = HEADS_PER_STEP * HEAD_DIM
    geom = dict(tq=tq, tk=tk, q_off=q_off, n_lead=n_lead, l_valid=l_valid)
    madd = pl.pallas_call(
        functools.partial(_dsa_select_kernel, topk=topk, idx_bits=lp.bit_length(), **geom),
        grid=(b, n_q),
        in_specs=[
            pl.BlockSpec((1, IDX_QW, tq), lambda bi, i: (bi, 0, i)),
            pl.BlockSpec((1, LANES, tq), lambda bi, i: (bi, 0, i)),
            pl.BlockSpec((1, lp, IDX_DIM), lambda bi, i: (bi, 0, 0)),
        ],
        out_specs=pl.BlockSpec((1, 1, lp, tq), lambda bi, i: (bi, i, 0, 0)),
        out_shape=jax.ShapeDtypeStruct((b, n_q, lp, tq), jnp.bfloat16),
        scratch_shapes=[
            pltpu.VMEM((lp // SUBLANES, SUBLANES, tq), jnp.int32),
            pltpu.VMEM((lp // PACKED_ROWS, PACKED_ROWS, tq), jnp.int16),
            pltpu.VMEM((lp // PACKED_ROWS, PACKED_ROWS, tq), jnp.int16),
            pltpu.VMEM((SUBLANES, tq), jnp.int32),
            pltpu.VMEM((SUBLANES, tq), jnp.int32),
            pltpu.VMEM((SUBLANES, tq), jnp.int32),
        ],
        compiler_params=pltpu.CompilerParams(
            dimension_semantics=("parallel", "parallel"), vmem_limit_bytes=VMEM_LIMIT),
        name="dsa_select",
    )(qit, kwt, kib)
    return pl.pallas_call(
        functools.partial(_dsa_attend_kernel, **geom),
        grid=(b, N_HEADS // HEADS_PER_STEP, n_q),
        in_specs=[
            pl.BlockSpec((1, hw, tq), lambda bi, h, i: (bi, h, i)),
            pl.BlockSpec((1, lp, hw), lambda bi, h, i: (bi, 0, h)),
            pl.BlockSpec((1, hw, lp), lambda bi, h, i: (bi, h, 0)),
            pl.BlockSpec((1, 1, lp, tq), lambda bi, h, i: (bi, i, 0, 0)),
            pl.BlockSpec((4, HEADS_PER_STEP, tk, tq), lambda bi, h, i: (0, h, 0, 0)),
        ],
        out_specs=pl.BlockSpec((1, tq, hw), lambda bi, h, i: (bi, i, h)),
        out_shape=jax.ShapeDtypeStruct((b, tqp, HD), jnp.bfloat16),
        scratch_shapes=[
            pltpu.VMEM((ATTN_GROUP * tk, tq), jnp.float32),
            pltpu.VMEM((ATTN_GROUP * tk, tq), jnp.float32),
            pltpu.VMEM((ATTN_GROUP * tk, tq), jnp.bfloat16),
        ],
        compiler_params=pltpu.CompilerParams(
            dimension_semantics=("parallel", "parallel", "parallel"),
            vmem_limit_bytes=VMEM_LIMIT),
        name="dsa_attend",
    )(qt, k, vt, madd, bias)


def _rel_bucket(rel):
    nb = N_BUCKETS // 2
    max_exact = nb // 2
    ret = jnp.where(rel > 0, nb, 0)
    n = jnp.abs(rel)
    nf = jnp.maximum(n, 1).astype(jnp.float32)
    large = max_exact + (jnp.log(nf / max_exact) / math.log(MAX_DISTANCE / max_exact)
                         * (nb - max_exact)).astype(jnp.int32)
    large = jnp.minimum(large, nb - 1)
    return ret + jnp.where(n < max_exact, n, large)


def _bias_tiles(rel_bias, tq, tk):
    w = tk + tq
    tiles = []
    for d in range(-2, 2):
        rel = d * tk + tk - 1 - jnp.arange(w, dtype=jnp.int32)
        vec = rel_bias.astype(jnp.float32)[_rel_bucket(rel)].T
        vec = jnp.roll(vec, -(tk - 1), axis=1)
        skew = jnp.tile(vec, (1, tk))[:, :tk * (w - 1)].reshape(-1, tk, w - 1)
        tiles.append(skew[:, :, :tq])
    tiles = jnp.stack(tiles)
    return tiles - tiles[0, :, :1, :1]


def _attn_b_kernel(q_ref, k_ref, v_ref, tri_ref, o_ref, *, tq, tk, q_off):
    i = pl.program_id(2)
    qpos0 = q_off + i * tq
    heads = [slice(hh * HEAD_DIM, (hh + 1) * HEAD_DIM) for hh in range(SB_HEADS_PER_STEP)]
    tri = tri_ref[...]

    def block(j, diagonal, runs, accs):
        ds = pl.ds(pl.multiple_of(j * tk, tk), tk)
        if diagonal:
            causal = ((j * tk + lax.broadcasted_iota(jnp.int32, (tq, tk), 1))
                      < (qpos0 + lax.broadcasted_iota(jnp.int32, (tq, tk), 0)))
        zs = [lax.dot_general(q_ref[0, :, hs], k_ref[0, ds, hs], _NT,
                              preferred_element_type=jnp.float32) for hs in heads]
        tails, log_1ms, withins = [], [], []
        for z in zs:
            tail = jnp.log(1.0 + jnp.exp(-jnp.abs(z)))
            log_1m = -jnp.maximum(z, 0.0) - tail
            if diagonal:
                log_1m = jnp.where(causal, log_1m, 0.0)
            hi = log_1m.astype(jnp.bfloat16)
            lo = (log_1m - hi.astype(jnp.float32)).astype(jnp.bfloat16)
            withins.append(jnp.dot(hi, tri, preferred_element_type=jnp.float32)
                           + jnp.dot(lo, tri, preferred_element_type=jnp.float32))
            tails.append(tail)
            log_1ms.append(log_1m)
        new_runs, new_accs = [], []
        for hh, hs in enumerate(heads):
            a = jnp.exp(jnp.minimum(zs[hh], 0.0) - tails[hh] + withins[hh] + runs[hh])
            if diagonal:
                a = jnp.where(causal, a, 0.0)
            new_accs.append(accs[hh] + jnp.dot(a.astype(jnp.bfloat16), v_ref[0, ds, hs],
                                               preferred_element_type=jnp.float32))
            new_runs.append(runs[hh] + withins[hh][:, 0:1] + log_1ms[hh][:, 0:1])
        return tuple(new_runs), tuple(new_accs)

    def cond(carry):
        j, runs, _ = carry
        return (j >= 0) & (jnp.max(functools.reduce(jnp.maximum, runs)) > EXP_ZERO_BELOW)

    def body(carry):
        j, runs, accs = carry
        runs, accs = block(j, False, runs, accs)
        return j - 1, runs, accs

    j0 = (qpos0 + tq - 2) // tk
    runs = tuple(jnp.zeros((tq, 1), jnp.float32) for _ in heads)
    accs = tuple(jnp.zeros((tq, HEAD_DIM), jnp.float32) for _ in heads)
    runs, accs = block(j0, True, runs, accs)
    _, _, accs = lax.while_loop(cond, body, (j0 - 1, runs, accs))
    for hs, acc in zip(heads, accs):
        o_ref[0, :, hs] = acc.astype(o_ref.dtype)


def _attn_b(q, k, v, *, tq, tk, q_off):
    b, tqp, _ = q.shape
    lp = k.shape[1]
    tri = (jnp.arange(tk)[:, None] > jnp.arange(tk)[None, :]).astype(jnp.bfloat16)
    assert q_off % tk == 0 and tq <= tk and (tq == tk or tqp == tq)
    hw = SB_HEADS_PER_STEP * HEAD_DIM
    kern = functools.partial(_attn_b_kernel, tq=tq, tk=tk, q_off=q_off)
    return pl.pallas_call(
        kern, grid=(b, N_HEADS // SB_HEADS_PER_STEP, tqp // tq),
        in_specs=[
            pl.BlockSpec((1, tq, hw), lambda bi, h, i: (bi, i, h)),
            pl.BlockSpec((1, lp, hw), lambda bi, h, i: (bi, 0, h)),
            pl.BlockSpec((1, lp, hw), lambda bi, h, i: (bi, 0, h)),
            _const_spec((tk, tk)),
        ],
        out_specs=pl.BlockSpec((1, tq, hw), lambda bi, h, i: (bi, i, h)),
        out_shape=jax.ShapeDtypeStruct((b, tqp, HD), jnp.bfloat16),
        compiler_params=pltpu.CompilerParams(
            dimension_semantics=("parallel", "parallel", "parallel"),
            vmem_limit_bytes=VMEM_LIMIT),
        name="stick_breaking_attention",
    )(q, k, v, tri)


def _post_kernel(x_ref, o_ref, left_ref, wo_ref, g1_ref, b1_ref, wg_ref, wu_ref, cw_ref, cb_ref,
                 wd_ref, g2_ref, b2_ref, y_ref, conv_ref, carry_ref,
                 *, tm, d_ff, alpha, r_last):
    t = pl.program_id(1)

    @pl.when(t == 0)
    def _init():
        carry_ref[...] = left_ref[0]

    mix = jnp.dot(o_ref[0], wo_ref[...], preferred_element_type=jnp.float32)
    x1 = _layer_norm(alpha * x_ref[0] + mix, g1_ref[...], b1_ref[...])
    xb = x1.astype(jnp.bfloat16)
    row = lax.broadcasted_iota(jnp.int32, (tm, FF_CHUNK), 0)
    acc = jnp.zeros((tm, D_MODEL), jnp.float32)
    n_chunks = d_ff // FF_CHUNK

    def gate_up(c):
        cs = slice(c * FF_CHUNK, (c + 1) * FF_CHUNK)
        return (jnp.dot(xb, wg_ref[:, cs], preferred_element_type=jnp.float32),
                jnp.dot(xb, wu_ref[:, cs], preferred_element_type=jnp.float32))

    nxt = gate_up(0)
    for c in range(n_chunks):
        cs = slice(c * FF_CHUNK, (c + 1) * FF_CHUNK)
        g, u = nxt
        if c + 1 < n_chunks:
            nxt = gate_up(c + 1)
        prev = carry_ref[:, cs]
        g1 = jnp.where(row == 0, prev[7:8], pltpu.roll(g, 1, 0))
        g2 = jnp.where(row == 0, prev[6:7], jnp.where(row == 1, prev[7:8], pltpu.roll(g, 2, 0)))
        gc = cb_ref[:, cs] + cw_ref[0:1, cs] * g2
        gc = gc + cw_ref[1:2, cs] * g1
        gc = gc + cw_ref[2:3, cs] * g
        hid = jax.nn.gelu(gc) * u
        acc = acc + jnp.dot(hid.astype(jnp.bfloat16), wd_ref[cs, :],
                            preferred_element_type=jnp.float32)
        carry_ref[:, cs] = g[tm - SUBLANES:tm]
        conv_ref[0, :, cs] = g[r_last:r_last + SUBLANES]

    y_ref[0] = _layer_norm(alpha * x1 + acc, g2_ref[...], b2_ref[...])


def _post(x, o, left8, t_valid, tm, w, alpha):
    b, tp, _ = x.shape
    d_ff = w["wg"].shape[1]
    assert d_ff % FF_CHUNK == 0 and t_valid % SUBLANES == 0 and t_valid >= SUBLANES
    assert (t_valid - 1) // tm == tp // tm - 1
    r_last = (t_valid - SUBLANES) % tm
    row = lambda width: pl.BlockSpec((1, tm, width), lambda i, j: (i, j, 0))
    vec = lambda a: a.reshape(1, -1).astype(jnp.float32)
    consts = [w["wo"], vec(w["g1"]), vec(w["b1"]), w["wg"], w["wu"], w["cw"].astype(jnp.float32),
              vec(w["cb"]), w["wd"], vec(w["g2"]), vec(w["b2"])]
    kern = functools.partial(_post_kernel, tm=tm, d_ff=d_ff, alpha=alpha, r_last=r_last)
    return pl.pallas_call(
        kern, grid=(b, tp // tm),
        in_specs=[row(D_MODEL), row(HD), pl.BlockSpec((1, SUBLANES, d_ff), lambda i, j: (i, 0, 0))]
                 + [_const_spec(c.shape) for c in consts],
        out_specs=[row(D_MODEL), pl.BlockSpec((1, SUBLANES, d_ff), lambda i, j: (i, 0, 0))],
        out_shape=[jax.ShapeDtypeStruct((b, tp, D_MODEL), jnp.float32),
                   jax.ShapeDtypeStruct((b, SUBLANES, d_ff), jnp.float32)],
        scratch_shapes=[pltpu.VMEM((SUBLANES, d_ff), jnp.float32)],
        compiler_params=pltpu.CompilerParams(
            dimension_semantics=("parallel", "arbitrary"), vmem_limit_bytes=VMEM_LIMIT),
        name="outproj_ln_convffn_ln",
    )(x, o, left8, *consts)


def _split_a(w_in):
    bf = jnp.bfloat16
    off_qi = 3 * HD
    off_ki = off_qi + IDX_QW
    wq, wk, wv = w_in[:, :HD], w_in[:, HD:2 * HD], w_in[:, 2 * HD:3 * HD]
    wqi = w_in[:, off_qi:off_ki]
    wkw = jnp.pad(w_in[:, off_ki:], ((0, 0), (0, LANES - IDX_DIM - IDX_HEADS)))
    row_scale = jnp.concatenate([
        jnp.ones((IDX_DIM,), jnp.float32),
        jnp.full((IDX_HEADS,), IDX_QW ** -0.5, jnp.float32),
        jnp.zeros((LANES - IDX_DIM - IDX_HEADS,), jnp.float32)]).reshape(LANES, 1)
    return [wq.T.astype(bf), wk.astype(bf), wv.astype(bf), wv.T.astype(bf), wqi.T.astype(bf),
            wkw.astype(bf), wkw.T.astype(bf), row_scale]


def _split_b(w_in):
    bf = jnp.bfloat16
    return [w_in[:, :HD].astype(bf), w_in[:, HD:2 * HD].astype(bf), w_in[:, 2 * HD:].astype(bf)]


def _with_cache(cache, new_bf, lp):
    b, p = cache.shape[:2]
    flat = cache.reshape(b, p, -1).astype(jnp.bfloat16)
    t = new_bf.shape[1]
    return jnp.pad(jnp.concatenate([flat, new_bf], axis=1), ((0, 0), (0, lp - p - t), (0, 0)))


def kernel(x_prompt, x_sample, cache_a_k, cache_a_v, cache_a_idx_k, cache_b_k, cache_b_v,
           state_ffn_conv, meta_tokens, rel_bias, w_a_in, w_a_out, w_b_in, w_b_out,
           ln1_g, ln1_b, ln2_g, ln2_b, w_ffn_gate, w_ffn_up, ffn_conv_w, ffn_conv_b, w_ffn_down):
    bp, seq, _ = x_prompt.shape
    bs, ts, _ = x_sample.shape
    depth = ln1_g.shape[0]
    d_ff = w_ffn_gate.shape[2]
    past = cache_a_k.shape[2]
    alpha = (2 * depth) ** 0.25
    t_p = N_META + seq
    tp_pad = _round_up(t_p, ROW_TILE)
    ls = past + ts
    ls_pad = _round_up(ls, ATTN_GROUP * KEY_TILE)
    assert past % KEY_TILE == 0 and ts % SUBLANES == 0 and ts <= SAMPLE_TILE
    topk_p = min(TOPK_MAX, seq // 4)
    topk_s = min(TOPK_MAX, ls // 4)
    bf = jnp.bfloat16

    meta = jnp.broadcast_to(meta_tokens.astype(x_prompt.dtype)[None], (bp, N_META, D_MODEL))
    xp = jnp.pad(jnp.concatenate([meta, x_prompt], axis=1), ((0, 0), (0, tp_pad - t_p), (0, 0)))
    xs = x_sample
    bias_p = _bias_tiles(rel_bias, ROW_TILE, KEY_TILE)
    bias_s = _bias_tiles(rel_bias, SAMPLE_TILE, KEY_TILE)
    left_p = jnp.zeros((bp, SUBLANES, d_ff), jnp.float32)
    wg_all, wu_all, wd_all = w_ffn_gate.astype(bf), w_ffn_up.astype(bf), w_ffn_down.astype(bf)
    wo_a, wo_b = w_a_out.astype(bf), w_b_out.astype(bf)

    heads = lambda a: a.reshape(a.shape[0], a.shape[1], N_HEADS, HEAD_DIM)
    outs = {k: [] for k in ("akp", "avp", "aip", "aks", "avs", "ais",
                            "bkp", "bvp", "bks", "bvs", "cp", "cs")}
    for i in range(depth):
        j = i // N_MIXERS
        if i % N_MIXERS == 0:
            wts = _split_a(w_a_in[j])
            qt, kc, kb, vc, vt, qit, kic, kib, kwt = _project(xp, t_p, ROW_TILE, wts, True)
            key_pad = _round_up(tp_pad, ATTN_GROUP * KEY_TILE) - tp_pad
            op = _attn_a(qt, qit, kwt, jnp.pad(kb, ((0, 0), (0, key_pad), (0, 0))),
                         jnp.pad(vt, ((0, 0), (0, 0), (0, key_pad))),
                         jnp.pad(kib, ((0, 0), (0, key_pad), (0, 0))), bias_p,
                         tq=ROW_TILE, tk=KEY_TILE, q_off=0, n_lead=N_META, l_valid=t_p, topk=topk_p)
            outs["akp"].append(heads(kc)); outs["avp"].append(heads(vc)); outs["aip"].append(kic)
            xs_pad = jnp.pad(xs, ((0, 0), (0, SAMPLE_TILE - ts), (0, 0)))
            qt, kc, kb, vc, vt, qit, kic, kib, kwt = _project(xs_pad, ts, SAMPLE_TILE, wts, True)
            cache_vt = jnp.swapaxes(cache_a_v[j].reshape(bs, past, HD).astype(bf), 1, 2)
            vt_all = jnp.pad(jnp.concatenate([cache_vt, vt[:, :, :ts]], axis=2),
                             ((0, 0), (0, 0), (0, ls_pad - ls)))
            os_ = _attn_a(qt, qit, kwt, _with_cache(cache_a_k[j], kb[:, :ts], ls_pad), vt_all,
                          _with_cache(cache_a_idx_k[j], kib[:, :ts], ls_pad), bias_s,
                          tq=SAMPLE_TILE, tk=KEY_TILE, q_off=past, n_lead=0, l_valid=ls,
                          topk=topk_s)[:, :ts]
            outs["aks"].append(heads(kc)); outs["avs"].append(heads(vc)); outs["ais"].append(kic)
            w_out = wo_a[j]
        else:
            wts = _split_b(w_b_in[j])
            q, kc, kb, vc, vb = _project(xp, t_p, ROW_TILE, wts, False)
            op = _attn_b(q, kb, vb, tq=ROW_TILE, tk=KEY_TILE, q_off=0)
            outs["bkp"].append(heads(kc)); outs["bvp"].append(heads(vc))
            q, kc, kb, vc, vb = _project(xs, ts, ts, wts, False)
            os_ = _attn_b(q, _with_cache(cache_b_k[j], kb, ls_pad),
                          _with_cache(cache_b_v[j], vb, ls_pad), tq=ts, tk=KEY_TILE, q_off=past)
            outs["bks"].append(heads(kc)); outs["bvs"].append(heads(vc))
            w_out = wo_b[j]
        w = dict(wo=w_out, g1=ln1_g[i], b1=ln1_b[i], wg=wg_all[i], wu=wu_all[i], cw=ffn_conv_w[i],
                 cb=ffn_conv_b[i], wd=wd_all[i], g2=ln2_g[i], b2=ln2_b[i])
        xp, conv_p = _post(xp, op, left_p, t_p, ROW_TILE, w, alpha)
        left_s = jnp.pad(state_ffn_conv[i].astype(jnp.float32),
                         ((0, 0), (SUBLANES - (CONV_W - 1), 0), (0, 0)))
        xs, conv_s = _post(xs, os_, left_s, ts, ts, w, alpha)
        outs["cp"].append(conv_p[:, SUBLANES - (CONV_W - 1):])
        outs["cs"].append(conv_s[:, SUBLANES - (CONV_W - 1):])

    st = jnp.stack
    return (xp[:, N_META:t_p], xs,
            st(outs["akp"]), st(outs["avp"]), st(outs["aip"]),
            st(outs["bkp"]), st(outs["bvp"]), st(outs["cp"]),
            st(outs["aks"]), st(outs["avs"]), st(outs["ais"]),
            st(outs["bks"]), st(outs["bvs"]), st(outs["cs"]))
```

```python
import functools
import math

import jax
import jax.numpy as jnp
from jax import lax
from jax.experimental import pallas as pl
from jax.experimental.pallas import tpu as pltpu

D_MODEL = 1024
N_HEADS = 8
HEAD_DIM = 128
HD = N_HEADS * HEAD_DIM
IDX_HEADS = 8
IDX_DIM = 64
IDX_QW = IDX_HEADS * IDX_DIM
CHUNK = 64
CHUNK_SHIFT = 6
N_META = 16
N_MIXERS = 2
TOPK_MAX = 256
N_BUCKETS = 32
MAX_DISTANCE = 128
CONV_W = 3
LN_EPS = 1e-5

LANES = 128
SUBLANES = 8
ROW_TILE = 256
KEY_TILE = 256
FF_CHUNK = 256
ATTN_GROUP = 4
HEADS_PER_STEP = 2
SB_HEADS_PER_STEP = 4
SAMPLE_TILE = 128
VMEM_LIMIT = 56 * 1024 * 1024

NEG_BIG = -1e30
INT_MIN = -2 ** 31
KEY_NEG_INF = -2139095041
HALF_BITS = 16
HALF_BIAS = 1 << (HALF_BITS - 1)
HALF_MASK = (1 << HALF_BITS) - 1
PACKED_ROWS = 16
EXP_ZERO_BELOW = -104.0

_NT = (((1,), (1,)), ((), ()))


def _round_up(n, m):
    return -(-n // m) * m


def _layer_norm(x, g, b):
    mu = jnp.mean(x, -1, keepdims=True)
    xc = x - mu
    var = jnp.mean(xc * xc, -1, keepdims=True)
    return xc * lax.rsqrt(var + LN_EPS) * g + b


def _proj_a_kernel(x_ref, wqt_ref, wk_ref, wv_ref, wvt_ref, wqit_ref, wkw_ref, wkwt_ref, rs_ref,
                   qt_ref, kc_ref, kb_ref, vc_ref, vt_ref, qit_ref, kic_ref, kib_ref, kwt_ref,
                   *, n_tiles):
    live = pl.program_id(1) < n_tiles
    xb = x_ref[0].astype(jnp.bfloat16)

    def cols(wt_ref):
        return lax.dot_general(wt_ref[...], xb, _NT, preferred_element_type=jnp.float32)

    qt_ref[0] = (cols(wqt_ref) * (HEAD_DIM ** -0.5)).astype(jnp.bfloat16)
    k = jnp.dot(xb, wk_ref[...], preferred_element_type=jnp.float32)
    kc_ref[0] = k
    kb_ref[0] = jnp.where(live, k, 0.0).astype(jnp.bfloat16)
    vc_ref[0] = jnp.dot(xb, wv_ref[...], preferred_element_type=jnp.float32)
    vt_ref[0] = jnp.where(live, cols(wvt_ref), 0.0).astype(jnp.bfloat16)
    qit_ref[0] = cols(wqit_ref).astype(jnp.bfloat16)
    kw = jnp.dot(xb, wkw_ref[...], preferred_element_type=jnp.float32)
    kic_ref[0] = kw[:, :IDX_DIM]
    kib_ref[0] = jnp.where(live, kw[:, :IDX_DIM], 0.0).astype(jnp.bfloat16)
    kwt_ref[0] = cols(wkwt_ref) * rs_ref[...]


def _proj_b_kernel(x_ref, wq_ref, wk_ref, wv_ref, q_ref, kc_ref, kb_ref, vc_ref, vb_ref):
    xb = x_ref[0].astype(jnp.bfloat16)
    q = jnp.dot(xb, wq_ref[...], preferred_element_type=jnp.float32)
    q_ref[0] = (q * (HEAD_DIM ** -0.5)).astype(jnp.bfloat16)
    k = jnp.dot(xb, wk_ref[...], preferred_element_type=jnp.float32)
    kc_ref[0] = k
    kb_ref[0] = k.astype(jnp.bfloat16)
    v = jnp.dot(xb, wv_ref[...], preferred_element_type=jnp.float32)
    vc_ref[0] = v
    vb_ref[0] = v.astype(jnp.bfloat16)


def _const_spec(shape):
    return pl.BlockSpec(shape, lambda *_: (0,) * len(shape))


def _project(x, t_valid, tm, weights, is_a, key_rows=None):
    b, tp, _ = x.shape
    n_tiles = tp // tm
    grid = (b, n_tiles)
    row = lambda w: pl.BlockSpec((1, tm, w), lambda i, j: (i, j, 0))
    bf = jnp.bfloat16
    f32 = jnp.float32
    pad = lambda w, dt: jax.ShapeDtypeStruct((b, tp, w), dt)
    exact = lambda w, dt: jax.ShapeDtypeStruct((b, t_valid, w), dt)
    in_specs = [row(D_MODEL)] + [_const_spec(w.shape) for w in weights]
    if is_a:
        assert key_rows % tm == 0 and key_rows >= tp
        grid = (b, key_rows // tm)
        last = n_tiles - 1
        row = lambda w: pl.BlockSpec((1, tm, w), lambda i, j: (i, jnp.minimum(j, last), 0))
        col = lambda w: pl.BlockSpec((1, w, tm), lambda i, j: (i, 0, jnp.minimum(j, last)))
        key_row = lambda w: pl.BlockSpec((1, tm, w), lambda i, j: (i, j, 0))
        key_col = lambda w: pl.BlockSpec((1, w, tm), lambda i, j: (i, 0, j))
        padt = lambda w, dt: jax.ShapeDtypeStruct((b, w, tp), dt)
        out_shape = [padt(HD, bf), exact(HD, f32), jax.ShapeDtypeStruct((b, key_rows, HD), bf),
                     exact(HD, f32), jax.ShapeDtypeStruct((b, HD, key_rows), bf),
                     padt(IDX_QW, bf), exact(IDX_DIM, f32),
                     jax.ShapeDtypeStruct((b, key_rows, IDX_DIM), bf), padt(LANES, f32)]
        out_specs = [col(HD), row(HD), key_row(HD), row(HD), key_col(HD),
                     col(IDX_QW), row(IDX_DIM), key_row(IDX_DIM), col(LANES)]
        in_specs[0] = row(D_MODEL)
        body = functools.partial(_proj_a_kernel, n_tiles=n_tiles)
    else:
        out_shape = [pad(HD, bf), exact(HD, f32), pad(HD, bf), exact(HD, f32), pad(HD, bf)]
        out_specs = [row(HD)] * 5
        body = _proj_b_kernel
    return pl.pallas_call(
        body, grid=grid, in_specs=in_specs, out_specs=out_specs, out_shape=out_shape,
        compiler_params=pltpu.CompilerParams(
            dimension_semantics=("parallel", "arbitrary"), vmem_limit_bytes=VMEM_LIMIT),
        name="proj_a" if is_a else "proj_b",
    )(x, *weights)


def _vis_end(qpos, n_lead, l_valid):
    ve = jnp.where(qpos < n_lead, n_lead,
                   n_lead + ((((qpos - n_lead) >> CHUNK_SHIFT) + 1) << CHUNK_SHIFT))
    return jnp.minimum(ve, l_valid)


def _key_blocks(i, tq, tk, q_off, n_lead, l_valid):
    return (_vis_end(q_off + i * tq + tq - 1, n_lead, l_valid) + tk - 1) // tk


def _dsa_select_kernel(qit_ref, kwt_ref, kib_ref, madd_ref,
                       keys_ref, hi_ref, lo_ref, thr_ref, need_ref, cut_ref,
                       *, tq, tk, q_off, n_lead, l_valid, topk, idx_bits):
    i = pl.program_id(1)
    qpos0 = q_off + i * tq
    nkb = _key_blocks(i, tq, tk, q_off, n_lead, l_valid)
    g = tk // SUBLANES

    def blk_rows(j):
        return pl.ds(pl.multiple_of(j * g, g), g)

    def blk_keys(j):
        return pl.ds(pl.multiple_of(j * tk, tk), tk)

    g16 = tk // PACKED_ROWS

    def blk_halves(j):
        return pl.ds(pl.multiple_of(j * g16, g16), g16)

    qpos = qpos0 + lax.broadcasted_iota(jnp.int32, (1, tq), 1)
    vis_end = _vis_end(qpos, n_lead, l_valid)

    def score_block(j, carry):
        kib = kib_ref[0, blk_keys(j), :]
        sc = jnp.zeros((tk, tq), jnp.float32)
        for ih in range(IDX_HEADS):
            s = jnp.dot(kib, qit_ref[0, ih * IDX_DIM:(ih + 1) * IDX_DIM, :],
                        preferred_element_type=jnp.float32)
            sc = sc + jnp.maximum(s, 0.0) * kwt_ref[0, IDX_DIM + ih:IDX_DIM + ih + 1, :]
        sc = jnp.where(sc == 0.0, 0.0, sc)
        bits = pltpu.bitcast(sc, jnp.int32)
        key = jnp.where(bits < 0, bits ^ 0x7FFFFFFF, bits)
        pos = j * tk + lax.broadcasted_iota(jnp.int32, (tk, tq), 0)
        key = jnp.where(pos < vis_end, key, KEY_NEG_INF)
        keys_ref[blk_rows(j)] = key.reshape(g, SUBLANES, tq)
        hi_ref[blk_halves(j)] = (key >> HALF_BITS).astype(jnp.int16).reshape(g16, PACKED_ROWS, tq)
        return carry

    lax.fori_loop(0, nkb, score_block, 0)

    def key_pos(j):
        shape = (g, SUBLANES, tq)
        return (j * tk + lax.broadcasted_iota(jnp.int32, shape, 0) * SUBLANES
                + lax.broadcasted_iota(jnp.int32, shape, 1))

    def count(pred):
        def blk(j, cnt):
            hit = pred(keys_ref[blk_rows(j)], j).astype(jnp.int32)
            part = jnp.sum(hit.reshape(4, g // 4, SUBLANES, tq), axis=1)
            return cnt + jnp.sum(part, axis=0)
        cnt = lax.fori_loop(0, nkb, blk, jnp.zeros((SUBLANES, tq), jnp.int32))
        return jnp.broadcast_to(jnp.sum(cnt, axis=0, keepdims=True), (SUBLANES, tq))

    def count_half(half_ref, cand):
        cand16 = jnp.broadcast_to(cand, (PACKED_ROWS, tq)).astype(jnp.int16)

        def pair(jj, cnt):
            rows = pl.ds(pl.multiple_of(jj * (2 * g16), 2 * g16), 2 * g16)
            hit = jnp.where(half_ref[rows] >= cand16, jnp.bfloat16(1), jnp.bfloat16(0))
            parts = [hit[a] for a in range(2 * g16)]
            while len(parts) > 1:
                parts = [parts[a] + parts[a + 1] for a in range(0, len(parts), 2)]
            return cnt + parts[0].astype(jnp.float32)
        cnt = lax.fori_loop(0, (nkb + 1) // 2, pair, jnp.zeros((PACKED_ROWS, tq), jnp.float32))
        return jnp.sum(cnt, axis=0, keepdims=True).astype(jnp.int32)

    def fill_pair_partner(half_ref):
        half_ref[blk_halves(nkb)] = jnp.full((g16, PACKED_ROWS, tq), -HALF_BIAS, jnp.int16)

    def kth_largest_half(half_ref, k):
        def step(b, ut):
            cand = ut | (jnp.int32(1) << (HALF_BITS - 1 - b))
            cnt = count_half(half_ref, cand - HALF_BIAS)
            return jnp.where(cnt >= k, cand, ut)
        return lax.fori_loop(0, HALF_BITS, step, jnp.zeros((1, tq), jnp.int32)) - HALF_BIAS

    fill_pair_partner(hi_ref)
    thr_hi = kth_largest_half(hi_ref, topk)
    above = jnp.where(thr_hi == HALF_BIAS - 1, 0,
                      count_half(hi_ref, jnp.minimum(thr_hi + 1, HALF_BIAS - 1)))

    def low_halves(j, carry):
        kk = keys_ref[blk_rows(j)].reshape(tk, tq)
        lo = jnp.where((kk >> HALF_BITS) == thr_hi, (kk & HALF_MASK) - HALF_BIAS, -HALF_BIAS)
        lo_ref[blk_halves(j)] = lo.astype(jnp.int16).reshape(g16, PACKED_ROWS, tq)
        return carry

    lax.fori_loop(0, nkb, low_halves, 0)
    fill_pair_partner(lo_ref)
    thr_lo = kth_largest_half(lo_ref, topk - above)
    thr = jnp.broadcast_to((thr_hi << HALF_BITS) | (thr_lo + HALF_BIAS), (SUBLANES, tq))
    cnt_gt = count(lambda kk, j: kk > thr)
    cnt_ge = count(lambda kk, j: kk >= thr)
    thr_ref[...] = thr
    need_ref[...] = topk - cnt_gt
    cut_ref[...] = jnp.full((SUBLANES, tq), 1 << idx_bits, jnp.int32)
    excess = (cnt_ge > topk) & (thr != KEY_NEG_INF)

    any_excess = jnp.max(excess.astype(jnp.int32))

    @pl.when(any_excess > 0)
    def _ties():
        need = need_ref[...]

        def cut_step(b, cut):
            cand = cut | (jnp.int32(1) << (idx_bits - 1 - b))
            cnt = count(lambda kk, j: (kk == thr) & (key_pos(j) < cand))
            return jnp.where(cnt <= need, cand, cut)

        cut_ref[...] = lax.fori_loop(0, idx_bits, cut_step, jnp.zeros((SUBLANES, tq), jnp.int32))

    def mask_block(j, carry, tied):
        kk = keys_ref[blk_rows(j)]
        thr_v = thr_ref[...]
        if tied:
            sel = (kk > thr_v) | ((kk == thr_v) & (key_pos(j) < cut_ref[...]))
        else:
            sel = kk >= thr_v
        sel = sel & (kk != KEY_NEG_INF)
        madd_ref[0, 0, blk_keys(j), :] = (
            jnp.where(sel, 0.0, NEG_BIG).reshape(tk, tq).astype(madd_ref.dtype))
        return carry

    @pl.when(any_excess > 0)
    def _mask_tied():
        lax.fori_loop(0, nkb, functools.partial(mask_block, tied=True), 0)

    @pl.when(any_excess == 0)
    def _mask_plain():
        lax.fori_loop(0, nkb, functools.partial(mask_block, tied=False), 0)

    def mask_tail(j, carry):
        madd_ref[0, 0, blk_keys(j), :] = jnp.full((tk, tq), NEG_BIG, madd_ref.dtype)
        return carry

    lax.fori_loop(nkb, madd_ref.shape[2] // tk, mask_tail, 0)


def _dsa_attend_kernel(qt_ref, k_ref, vt_ref, madd_ref, bias_ref, o_ref, s0_ref, s1_ref, p1_ref,
                       *, tq, tk, q_off, n_lead, l_valid):
    i = pl.program_id(2)
    qpos0 = q_off + i * tq
    nkb = _key_blocks(i, tq, tk, q_off, n_lead, l_valid)
    wide = ATTN_GROUP * tk
    n_trips = (nkb + ATTN_GROUP - 1) // ATTN_GROUP
    jq = qpos0 // tk
    s_refs = (s0_ref, s1_ref)
    heads = [slice(hh * HEAD_DIM, (hh + 1) * HEAD_DIM) for hh in range(HEADS_PER_STEP)]

    def keys_of(t):
        return pl.ds(pl.multiple_of(t * wide, wide), wide)

    def scores(hh, t):
        s_refs[hh][...] = jnp.dot(k_ref[0, keys_of(t), heads[hh]], qt_ref[0, heads[hh], :],
                                  preferred_element_type=jnp.float32)

    def softmax(hh, t, m, near):
        parts = []
        for u in range(ATTN_GROUP):
            j = t * ATTN_GROUP + u
            madd = madd_ref[0, 0, pl.ds(pl.multiple_of(j * tk, tk), tk), :]
            part = s_refs[hh][u * tk:(u + 1) * tk] + madd.astype(jnp.float32)
            if near:
                part = part + bias_ref[jnp.clip(j - jq, -2, 1) + 2, hh]
            parts.append(part)
        m_new = m
        for part in parts:
            m_new = jnp.maximum(m_new, jnp.max(part, axis=0, keepdims=True))
        alpha = jnp.exp(m - m_new)
        pb = jnp.concatenate([jnp.exp(part - m_new).astype(jnp.bfloat16) for part in parts], axis=0)
        return m_new, alpha, pb

    ones = jnp.ones((SUBLANES, wide), jnp.bfloat16)

    def weighted_values(hh, t, pb):
        num = jnp.dot(vt_ref[0, heads[hh], keys_of(t)], pb, preferred_element_type=jnp.float32)
        den = jnp.dot(ones, pb, preferred_element_type=jnp.float32)
        return jnp.concatenate([num, den], axis=0)

    def trip(t, carries, near):
        (m0, acc0), (m1, acc1) = carries
        acc1 = acc1 + weighted_values(1, jnp.maximum(t - 1, 0), p1_ref[...])
        scores(1, t)
        m0, alpha0, pb0 = softmax(0, t, m0, near)
        acc0 = alpha0 * acc0 + weighted_values(0, t, pb0)
        scores(0, jnp.minimum(t + 1, n_trips - 1))
        m1, alpha1, pb1 = softmax(1, t, m1, near)
        p1_ref[...] = pb1
        return (m0, acc0), (m1, alpha1 * acc1)

    scores(0, 0)
    p1_ref[...] = jnp.zeros(p1_ref.shape, p1_ref.dtype)
    carries = tuple((jnp.full((1, tq), NEG_BIG, jnp.float32),
                     jnp.zeros((HEAD_DIM + SUBLANES, tq), jnp.float32)) for _ in range(2))
    n_far = jnp.clip((jq - 1) // ATTN_GROUP, 0, n_trips)
    carries = lax.fori_loop(0, n_far, functools.partial(trip, near=False), carries)
    (_, acc0), (_, acc1) = lax.fori_loop(n_far, n_trips, functools.partial(trip, near=True), carries)
    acc1 = acc1 + weighted_values(1, n_trips - 1, p1_ref[...])
    for hs, acc in zip(heads, (acc0, acc1)):
        o_ref[0, :, hs] = (acc[:HEAD_DIM] / acc[HEAD_DIM:HEAD_DIM + 1]).T.astype(o_ref.dtype)


def _attn_a(qt, qit, kwt, k, vt, kib, bias, *, tq, tk, q_off, n_lead, l_valid, topk):
    b, _, tqp = qt.shape
    lp = k.shape[1]
    assert lp % (ATTN_GROUP * tk) == 0 and tk % (4 * SUBLANES) == 0 and tk >= topk
    assert tqp % tq == 0 and q_off % tk == 0 and HEADS_PER_STEP == 2
    n_q = tqp // tq
    hw = HEADS_PER_STEP * HEAD_DIM
    geom = dict(tq=tq, tk=tk, q_off=q_off, n_lead=n_lead, l_valid=l_valid)
    madd = pl.pallas_call(
        functools.partial(_dsa_select_kernel, topk=topk, idx_bits=lp.bit_length(), **geom),
        grid=(b, n_q),
        in_specs=[
            pl.BlockSpec((1, IDX_QW, tq), lambda bi, i: (bi, 0, i)),
            pl.BlockSpec((1, LANES, tq), lambda bi, i: (bi, 0, i)),
            pl.BlockSpec((1, lp, IDX_DIM), lambda bi, i: (bi, 0, 0)),
        ],
        out_specs=pl.BlockSpec((1, 1, lp, tq), lambda bi, i: (bi, i, 0, 0)),
        out_shape=jax.ShapeDtypeStruct((b, n_q, lp, tq), jnp.bfloat16),
        scratch_shapes=[
            pltpu.VMEM((lp // SUBLANES, SUBLANES, tq), jnp.int32),
            pltpu.VMEM(((lp + tk) // PACKED_ROWS, PACKED_ROWS, tq), jnp.int16),
            pltpu.VMEM(((lp + tk) // PACKED_ROWS, PACKED_ROWS, tq), jnp.int16),
            pltpu.VMEM((SUBLANES, tq), jnp.int32),
            pltpu.VMEM((SUBLANES, tq), jnp.int32),
            pltpu.VMEM((SUBLANES, tq), jnp.int32),
        ],
        compiler_params=pltpu.CompilerParams(
            dimension_semantics=("parallel", "parallel"), vmem_limit_bytes=VMEM_LIMIT),
        name="dsa_select",
    )(qit, kwt, kib)
    return pl.pallas_call(
        functools.partial(_dsa_attend_kernel, **geom),
        grid=(b, N_HEADS // HEADS_PER_STEP, n_q),
        in_specs=[
            pl.BlockSpec((1, hw, tq), lambda bi, h, i: (bi, h, i)),
            pl.BlockSpec((1, lp, hw), lambda bi, h, i: (bi, 0, h)),
            pl.BlockSpec((1, hw, lp), lambda bi, h, i: (bi, h, 0)),
            pl.BlockSpec((1, 1, lp, tq), lambda bi, h, i: (bi, i, 0, 0)),
            pl.BlockSpec((4, HEADS_PER_STEP, tk, tq), lambda bi, h, i: (0, h, 0, 0)),
        ],
        out_specs=pl.BlockSpec((1, tq, hw), lambda bi, h, i: (bi, i, h)),
        out_shape=jax.ShapeDtypeStruct((b, tqp, HD), jnp.bfloat16),
        scratch_shapes=[
            pltpu.VMEM((ATTN_GROUP * tk, tq), jnp.float32),
            pltpu.VMEM((ATTN_GROUP * tk, tq), jnp.float32),
            pltpu.VMEM((ATTN_GROUP * tk, tq), jnp.bfloat16),
        ],
        compiler_params=pltpu.CompilerParams(
            dimension_semantics=("parallel", "parallel", "parallel"),
            vmem_limit_bytes=VMEM_LIMIT),
        name="dsa_attend",
    )(qt, k, vt, madd, bias)


def _rel_bucket(rel):
    nb = N_BUCKETS // 2
    max_exact = nb // 2
    ret = jnp.where(rel > 0, nb, 0)
    n = jnp.abs(rel)
    nf = jnp.maximum(n, 1).astype(jnp.float32)
    large = max_exact + (jnp.log(nf / max_exact) / math.log(MAX_DISTANCE / max_exact)
                         * (nb - max_exact)).astype(jnp.int32)
    large = jnp.minimum(large, nb - 1)
    return ret + jnp.where(n < max_exact, n, large)


def _bias_tiles(rel_bias, tq, tk):
    w = tk + tq
    tiles = []
    for d in range(-2, 2):
        rel = d * tk + tk - 1 - jnp.arange(w, dtype=jnp.int32)
        vec = rel_bias.astype(jnp.float32)[_rel_bucket(rel)].T
        vec = jnp.roll(vec, -(tk - 1), axis=1)
        skew = jnp.tile(vec, (1, tk))[:, :tk * (w - 1)].reshape(-1, tk, w - 1)
        tiles.append(skew[:, :, :tq])
    tiles = jnp.stack(tiles)
    return tiles - tiles[0, :, :1, :1]


def _attn_b_kernel(q_ref, k_ref, v_ref, tri_ref, o_ref, *, tq, tk, q_off):
    i = pl.program_id(2)
    qpos0 = q_off + i * tq
    heads = [slice(hh * HEAD_DIM, (hh + 1) * HEAD_DIM) for hh in range(SB_HEADS_PER_STEP)]
    tri = tri_ref[...]

    def block(j, diagonal, runs, accs):
        ds = pl.ds(pl.multiple_of(j * tk, tk), tk)
        if diagonal:
            causal = ((j * tk + lax.broadcasted_iota(jnp.int32, (tq, tk), 1))
                      < (qpos0 + lax.broadcasted_iota(jnp.int32, (tq, tk), 0)))
        zs = [lax.dot_general(q_ref[0, :, hs], k_ref[0, ds, hs], _NT,
                              preferred_element_type=jnp.float32) for hs in heads]
        tails, log_1ms, withins = [], [], []
        for z in zs:
            tail = jnp.log(1.0 + jnp.exp(-jnp.abs(z)))
            log_1m = -jnp.maximum(z, 0.0) - tail
            if diagonal:
                log_1m = jnp.where(causal, log_1m, 0.0)
            hi = log_1m.astype(jnp.bfloat16)
            lo = (log_1m - hi.astype(jnp.float32)).astype(jnp.bfloat16)
            withins.append(jnp.dot(hi, tri, preferred_element_type=jnp.float32)
                           + jnp.dot(lo, tri, preferred_element_type=jnp.float32))
            tails.append(tail)
            log_1ms.append(log_1m)
        new_runs, new_accs = [], []
        for hh, hs in enumerate(heads):
            a = jnp.exp(jnp.minimum(zs[hh], 0.0) - tails[hh] + withins[hh] + runs[hh])
            if diagonal:
                a = jnp.where(causal, a, 0.0)
            new_accs.append(accs[hh] + jnp.dot(a.astype(jnp.bfloat16), v_ref[0, ds, hs],
                                               preferred_element_type=jnp.float32))
            new_runs.append(runs[hh] + withins[hh][:, 0:1] + log_1ms[hh][:, 0:1])
        return tuple(new_runs), tuple(new_accs)

    def cond(carry):
        j, runs, _ = carry
        return (j >= 0) & (jnp.max(functools.reduce(jnp.maximum, runs)) > EXP_ZERO_BELOW)

    def body(carry):
        j, runs, accs = carry
        runs, accs = block(j, False, runs, accs)
        return j - 1, runs, accs

    j0 = (qpos0 + tq - 2) // tk
    runs = tuple(jnp.zeros((tq, 1), jnp.float32) for _ in heads)
    accs = tuple(jnp.zeros((tq, HEAD_DIM), jnp.float32) for _ in heads)
    runs, accs = block(j0, True, runs, accs)
    _, _, accs = lax.while_loop(cond, body, (j0 - 1, runs, accs))
    for hs, acc in zip(heads, accs):
        o_ref[0, :, hs] = acc.astype(o_ref.dtype)


def _attn_b(q, k, v, *, tq, tk, q_off):
    b, tqp, _ = q.shape
    lp = k.shape[1]
    tri = (jnp.arange(tk)[:, None] > jnp.arange(tk)[None, :]).astype(jnp.bfloat16)
    assert q_off % tk == 0 and tq <= tk and (tq == tk or tqp == tq)
    hw = SB_HEADS_PER_STEP * HEAD_DIM
    kern = functools.partial(_attn_b_kernel, tq=tq, tk=tk, q_off=q_off)
    return pl.pallas_call(
        kern, grid=(b, N_HEADS // SB_HEADS_PER_STEP, tqp // tq),
        in_specs=[
            pl.BlockSpec((1, tq, hw), lambda bi, h, i: (bi, i, h)),
            pl.BlockSpec((1, lp, hw), lambda bi, h, i: (bi, 0, h)),
            pl.BlockSpec((1, lp, hw), lambda bi, h, i: (bi, 0, h)),
            _const_spec((tk, tk)),
        ],
        out_specs=pl.BlockSpec((1, tq, hw), lambda bi, h, i: (bi, i, h)),
        out_shape=jax.ShapeDtypeStruct((b, tqp, HD), jnp.bfloat16),
        compiler_params=pltpu.CompilerParams(
            dimension_semantics=("parallel", "parallel", "parallel"),
            vmem_limit_bytes=VMEM_LIMIT),
        name="stick_breaking_attention",
    )(q, k, v, tri)


def _post_kernel(x_ref, o_ref, left_ref, wo_ref, g1_ref, b1_ref, wg_ref, wu_ref, cw_ref, cb_ref,
                 wd_ref, g2_ref, b2_ref, y_ref, conv_ref, carry_ref,
                 *, tm, d_ff, alpha, r_last):
    t = pl.program_id(1)

    @pl.when(t == 0)
    def _init():
        carry_ref[...] = left_ref[0]

    mix = jnp.dot(o_ref[0], wo_ref[...], preferred_element_type=jnp.float32)
    x1 = _layer_norm(alpha * x_ref[0] + mix, g1_ref[...], b1_ref[...])
    xb = x1.astype(jnp.bfloat16)
    row = lax.broadcasted_iota(jnp.int32, (tm, FF_CHUNK), 0)
    acc = jnp.zeros((tm, D_MODEL), jnp.float32)
    n_chunks = d_ff // FF_CHUNK

    def gate_up(c):
        cs = slice(c * FF_CHUNK, (c + 1) * FF_CHUNK)
        return (jnp.dot(xb, wg_ref[:, cs], preferred_element_type=jnp.float32),
                jnp.dot(xb, wu_ref[:, cs], preferred_element_type=jnp.float32))

    nxt = gate_up(0)
    for c in range(n_chunks):
        cs = slice(c * FF_CHUNK, (c + 1) * FF_CHUNK)
        g, u = nxt
        if c + 1 < n_chunks:
            nxt = gate_up(c + 1)
        prev = carry_ref[:, cs]
        g1 = jnp.where(row == 0, prev[7:8], pltpu.roll(g, 1, 0))
        g2 = jnp.where(row == 0, prev[6:7], jnp.where(row == 1, prev[7:8], pltpu.roll(g, 2, 0)))
        gc = cb_ref[:, cs] + cw_ref[0:1, cs] * g2
        gc = gc + cw_ref[1:2, cs] * g1
        gc = gc + cw_ref[2:3, cs] * g
        hid = jax.nn.gelu(gc) * u
        acc = acc + jnp.dot(hid.astype(jnp.bfloat16), wd_ref[cs, :],
                            preferred_element_type=jnp.float32)
        carry_ref[:, cs] = g[tm - SUBLANES:tm]
        conv_ref[0, :, cs] = g[r_last:r_last + SUBLANES]

    y_ref[0] = _layer_norm(alpha * x1 + acc, g2_ref[...], b2_ref[...])


def _post(x, o, left8, t_valid, tm, w, alpha):
    b, tp, _ = x.shape
    d_ff = w["wg"].shape[1]
    assert d_ff % FF_CHUNK == 0 and t_valid % SUBLANES == 0 and t_valid >= SUBLANES
    assert (t_valid - 1) // tm == tp // tm - 1
    r_last = (t_valid - SUBLANES) % tm
    row = lambda width: pl.BlockSpec((1, tm, width), lambda i, j: (i, j, 0))
    vec = lambda a: a.reshape(1, -1).astype(jnp.float32)
    consts = [w["wo"], vec(w["g1"]), vec(w["b1"]), w["wg"], w["wu"], w["cw"].astype(jnp.float32),
              vec(w["cb"]), w["wd"], vec(w["g2"]), vec(w["b2"])]
    kern = functools.partial(_post_kernel, tm=tm, d_ff=d_ff, alpha=alpha, r_last=r_last)
    return pl.pallas_call(
        kern, grid=(b, tp // tm),
        in_specs=[row(D_MODEL), row(HD), pl.BlockSpec((1, SUBLANES, d_ff), lambda i, j: (i, 0, 0))]
                 + [_const_spec(c.shape) for c in consts],
        out_specs=[row(D_MODEL), pl.BlockSpec((1, SUBLANES, d_ff), lambda i, j: (i, 0, 0))],
        out_shape=[jax.ShapeDtypeStruct((b, tp, D_MODEL), jnp.float32),
                   jax.ShapeDtypeStruct((b, SUBLANES, d_ff), jnp.float32)],
        scratch_shapes=[pltpu.VMEM((SUBLANES, d_ff), jnp.float32)],
        compiler_params=pltpu.CompilerParams(
            dimension_semantics=("parallel", "arbitrary"), vmem_limit_bytes=VMEM_LIMIT),
        name="outproj_ln_convffn_ln",
    )(x, o, left8, *consts)


def _split_a(w_in):
    bf = jnp.bfloat16
    off_qi = 3 * HD
    off_ki = off_qi + IDX_QW
    wq, wk, wv = w_in[:, :HD], w_in[:, HD:2 * HD], w_in[:, 2 * HD:3 * HD]
    wqi = w_in[:, off_qi:off_ki]
    wkw = jnp.pad(w_in[:, off_ki:], ((0, 0), (0, LANES - IDX_DIM - IDX_HEADS)))
    row_scale = jnp.concatenate([
        jnp.ones((IDX_DIM,), jnp.float32),
        jnp.full((IDX_HEADS,), IDX_QW ** -0.5, jnp.float32),
        jnp.zeros((LANES - IDX_DIM - IDX_HEADS,), jnp.float32)]).reshape(LANES, 1)
    return [wq.T.astype(bf), wk.astype(bf), wv.astype(bf), wv.T.astype(bf), wqi.T.astype(bf),
            wkw.astype(bf), wkw.T.astype(bf), row_scale]


def _split_b(w_in):
    bf = jnp.bfloat16
    return [w_in[:, :HD].astype(bf), w_in[:, HD:2 * HD].astype(bf), w_in[:, 2 * HD:].astype(bf)]


def _with_cache(cache, new_bf, lp):
    b, p = cache.shape[:2]
    flat = cache.reshape(b, p, -1).astype(jnp.bfloat16)
    t = new_bf.shape[1]
    return jnp.pad(jnp.concatenate([flat, new_bf], axis=1), ((0, 0), (0, lp - p - t), (0, 0)))


def kernel(x_prompt, x_sample, cache_a_k, cache_a_v, cache_a_idx_k, cache_b_k, cache_b_v,
           state_ffn_conv, meta_tokens, rel_bias, w_a_in, w_a_out, w_b_in, w_b_out,
           ln1_g, ln1_b, ln2_g, ln2_b, w_ffn_gate, w_ffn_up, ffn_conv_w, ffn_conv_b, w_ffn_down):
    bp, seq, _ = x_prompt.shape
    bs, ts, _ = x_sample.shape
    depth = ln1_g.shape[0]
    d_ff = w_ffn_gate.shape[2]
    past = cache_a_k.shape[2]
    alpha = (2 * depth) ** 0.25
    t_p = N_META + seq
    tp_pad = _round_up(t_p, ROW_TILE)
    ls = past + ts
    ls_pad = _round_up(ls, ATTN_GROUP * KEY_TILE)
    assert past % KEY_TILE == 0 and ts % SUBLANES == 0 and ts <= SAMPLE_TILE
    topk_p = min(TOPK_MAX, seq // 4)
    topk_s = min(TOPK_MAX, ls // 4)
    bf = jnp.bfloat16

    meta = jnp.broadcast_to(meta_tokens.astype(x_prompt.dtype)[None], (bp, N_META, D_MODEL))
    xp = jnp.pad(jnp.concatenate([meta, x_prompt], axis=1), ((0, 0), (0, tp_pad - t_p), (0, 0)))
    xs = x_sample
    bias_p = _bias_tiles(rel_bias, ROW_TILE, KEY_TILE)
    bias_s = _bias_tiles(rel_bias, SAMPLE_TILE, KEY_TILE)
    left_p = jnp.zeros((bp, SUBLANES, d_ff), jnp.float32)
    wg_all, wu_all, wd_all = w_ffn_gate.astype(bf), w_ffn_up.astype(bf), w_ffn_down.astype(bf)
    wo_a, wo_b = w_a_out.astype(bf), w_b_out.astype(bf)

    heads = lambda a: a.reshape(a.shape[0], a.shape[1], N_HEADS, HEAD_DIM)
    outs = {k: [] for k in ("akp", "avp", "aip", "aks", "avs", "ais",
                            "bkp", "bvp", "bks", "bvs", "cp", "cs")}
    for i in range(depth):
        j = i // N_MIXERS
        if i % N_MIXERS == 0:
            wts = _split_a(w_a_in[j])
            qt, kc, kb, vc, vt, qit, kic, kib, kwt = _project(
                xp, t_p, ROW_TILE, wts, True, key_rows=_round_up(tp_pad, ATTN_GROUP * KEY_TILE))
            op = _attn_a(qt, qit, kwt, kb, vt, kib, bias_p,
                         tq=ROW_TILE, tk=KEY_TILE, q_off=0, n_lead=N_META, l_valid=t_p, topk=topk_p)
            outs["akp"].append(heads(kc)); outs["avp"].append(heads(vc)); outs["aip"].append(kic)
            xs_pad = jnp.pad(xs, ((0, 0), (0, SAMPLE_TILE - ts), (0, 0)))
            qt, kc, kb, vc, vt, qit, kic, kib, kwt = _project(
                xs_pad, ts, SAMPLE_TILE, wts, True, key_rows=SAMPLE_TILE)
            cache_vt = jnp.swapaxes(cache_a_v[j].reshape(bs, past, HD).astype(bf), 1, 2)
            vt_all = jnp.pad(jnp.concatenate([cache_vt, vt[:, :, :ts]], axis=2),
                             ((0, 0), (0, 0), (0, ls_pad - ls)))
            os_ = _attn_a(qt, qit, kwt, _with_cache(cache_a_k[j], kb[:, :ts], ls_pad), vt_all,
                          _with_cache(cache_a_idx_k[j], kib[:, :ts], ls_pad), bias_s,
                          tq=SAMPLE_TILE, tk=KEY_TILE, q_off=past, n_lead=0, l_valid=ls,
                          topk=topk_s)[:, :ts]
            outs["aks"].append(heads(kc)); outs["avs"].append(heads(vc)); outs["ais"].append(kic)
            w_out = wo_a[j]
        else:
            wts = _split_b(w_b_in[j])
            q, kc, kb, vc, vb = _project(xp, t_p, ROW_TILE, wts, False)
            op = _attn_b(q, kb, vb, tq=ROW_TILE, tk=KEY_TILE, q_off=0)
            outs["bkp"].append(heads(kc)); outs["bvp"].append(heads(vc))
            q, kc, kb, vc, vb = _project(xs, ts, ts, wts, False)
            os_ = _attn_b(q, _with_cache(cache_b_k[j], kb, ls_pad),
                          _with_cache(cache_b_v[j], vb, ls_pad), tq=ts, tk=KEY_TILE, q_off=past)
            outs["bks"].append(heads(kc)); outs["bvs"].append(heads(vc))
            w_out = wo_b[j]
        w = dict(wo=w_out, g1=ln1_g[i], b1=ln1_b[i], wg=wg_all[i], wu=wu_all[i], cw=ffn_conv_w[i],
                 cb=ffn_conv_b[i], wd=wd_all[i], g2=ln2_g[i], b2=ln2_b[i])
        xp, conv_p = _post(xp, op, left_p, t_p, ROW_TILE, w, alpha)
        left_s = jnp.pad(state_ffn_conv[i].astype(jnp.float32),
                         ((0, 0), (SUBLANES - (CONV_W - 1), 0), (0, 0)))
        xs, conv_s = _post(xs, os_, left_s, ts, ts, w, alpha)
        outs["cp"].append(conv_p[:, SUBLANES - (CONV_W - 1):])
        outs["cs"].append(conv_s[:, SUBLANES - (CONV_W - 1):])

    st = jnp.stack
    return (xp[:, N_META:t_p], xs,
            st(outs["akp"]), st(outs["avp"]), st(outs["aip"]),
            st(outs["bkp"]), st(outs["bvp"]), st(outs["cp"]),
            st(outs["aks"]), st(outs["avs"]), st(outs["ais"]),
            st(outs["bks"]), st(outs["bvs"]), st(outs["cs"]))
```

```python
import functools
import math

import jax
import jax.numpy as jnp
from jax import lax
from jax.experimental import pallas as pl
from jax.experimental.pallas import tpu as pltpu

D_MODEL = 1024
N_HEADS = 8
HEAD_DIM = 128
HD = N_HEADS * HEAD_DIM
IDX_HEADS = 8
IDX_DIM = 64
IDX_QW = IDX_HEADS * IDX_DIM
CHUNK = 64
CHUNK_SHIFT = 6
N_META = 16
N_MIXERS = 2
TOPK_MAX = 256
N_BUCKETS = 32
MAX_DISTANCE = 128
CONV_W = 3
LN_EPS = 1e-5

LANES = 128
SUBLANES = 8
ROW_TILE = 256
KEY_TILE = 256
FF_CHUNK = 256
ATTN_GROUP = 4
HEADS_PER_STEP = 2
SB_HEADS_PER_STEP = 4
SAMPLE_TILE = 128
VMEM_LIMIT = 56 * 1024 * 1024

NEG_BIG = -1e30
INT_MIN = -2 ** 31
KEY_NEG_INF = -2139095041
HALF_BITS = 16
HALF_BIAS = 1 << (HALF_BITS - 1)
HALF_MASK = (1 << HALF_BITS) - 1
PACKED_ROWS = 16
EXP_ZERO_BELOW = -104.0

_NT = (((1,), (1,)), ((), ()))


def _round_up(n, m):
    return -(-n // m) * m


def _layer_norm(x, g, b):
    mu = jnp.mean(x, -1, keepdims=True)
    xc = x - mu
    var = jnp.mean(xc * xc, -1, keepdims=True)
    return xc * lax.rsqrt(var + LN_EPS) * g + b


def _proj_a_kernel(x_ref, wqt_ref, wk_ref, wv_ref, wvt_ref, wqit_ref, wkw_ref, wkwt_ref, rs_ref,
                   qt_ref, kc_ref, kb_ref, vc_ref, vt_ref, qit_ref, kic_ref, kib_ref, kwt_ref,
                   *, n_tiles):
    live = pl.program_id(1) < n_tiles
    xb = x_ref[0].astype(jnp.bfloat16)

    def cols(wt_ref):
        return lax.dot_general(wt_ref[...], xb, _NT, preferred_element_type=jnp.float32)

    qt_ref[0] = (cols(wqt_ref) * (HEAD_DIM ** -0.5)).astype(jnp.bfloat16)
    k = jnp.dot(xb, wk_ref[...], preferred_element_type=jnp.float32)
    kc_ref[0] = k
    kb_ref[0] = jnp.where(live, k, 0.0).astype(jnp.bfloat16)
    vc_ref[0] = jnp.dot(xb, wv_ref[...], preferred_element_type=jnp.float32)
    vt_ref[0] = jnp.where(live, cols(wvt_ref), 0.0).astype(jnp.bfloat16)
    qit_ref[0] = cols(wqit_ref).astype(jnp.bfloat16)
    kw = jnp.dot(xb, wkw_ref[...], preferred_element_type=jnp.float32)
    kic_ref[0] = kw[:, :IDX_DIM]
    kib_ref[0] = jnp.where(live, kw[:, :IDX_DIM], 0.0).astype(jnp.bfloat16)
    kwt_ref[0] = cols(wkwt_ref) * rs_ref[...]


def _proj_b_kernel(x_ref, wq_ref, wk_ref, wv_ref, q_ref, kc_ref, kb_ref, vc_ref, vb_ref):
    xb = x_ref[0].astype(jnp.bfloat16)
    q = jnp.dot(xb, wq_ref[...], preferred_element_type=jnp.float32)
    q_ref[0] = (q * (HEAD_DIM ** -0.5)).astype(jnp.bfloat16)
    k = jnp.dot(xb, wk_ref[...], preferred_element_type=jnp.float32)
    kc_ref[0] = k
    kb_ref[0] = k.astype(jnp.bfloat16)
    v = jnp.dot(xb, wv_ref[...], preferred_element_type=jnp.float32)
    vc_ref[0] = v
    vb_ref[0] = v.astype(jnp.bfloat16)


def _const_spec(shape):
    return pl.BlockSpec(shape, lambda *_: (0,) * len(shape))


def _project(x, t_valid, tm, weights, is_a, key_rows=None):
    b, tp, _ = x.shape
    n_tiles = tp // tm
    grid = (b, n_tiles)
    row = lambda w: pl.BlockSpec((1, tm, w), lambda i, j: (i, j, 0))
    bf = jnp.bfloat16
    f32 = jnp.float32
    pad = lambda w, dt: jax.ShapeDtypeStruct((b, tp, w), dt)
    exact = lambda w, dt: jax.ShapeDtypeStruct((b, t_valid, w), dt)
    in_specs = [row(D_MODEL)] + [_const_spec(w.shape) for w in weights]
    if is_a:
        assert key_rows % tm == 0 and key_rows >= tp
        grid = (b, key_rows // tm)
        last = n_tiles - 1
        row = lambda w: pl.BlockSpec((1, tm, w), lambda i, j: (i, jnp.minimum(j, last), 0))
        col = lambda w: pl.BlockSpec((1, w, tm), lambda i, j: (i, 0, jnp.minimum(j, last)))
        key_row = lambda w: pl.BlockSpec((1, tm, w), lambda i, j: (i, j, 0))
        key_col = lambda w: pl.BlockSpec((1, w, tm), lambda i, j: (i, 0, j))
        padt = lambda w, dt: jax.ShapeDtypeStruct((b, w, tp), dt)
        out_shape = [padt(HD, bf), exact(HD, f32), jax.ShapeDtypeStruct((b, key_rows, HD), bf),
                     exact(HD, f32), jax.ShapeDtypeStruct((b, HD, key_rows), bf),
                     padt(IDX_QW, bf), exact(IDX_DIM, f32),
                     jax.ShapeDtypeStruct((b, key_rows, IDX_DIM), bf), padt(LANES, f32)]
        out_specs = [col(HD), row(HD), key_row(HD), row(HD), key_col(HD),
                     col(IDX_QW), row(IDX_DIM), key_row(IDX_DIM), col(LANES)]
        in_specs[0] = row(D_MODEL)
        body = functools.partial(_proj_a_kernel, n_tiles=n_tiles)
    else:
        out_shape = [pad(HD, bf), exact(HD, f32), pad(HD, bf), exact(HD, f32), pad(HD, bf)]
        out_specs = [row(HD)] * 5
        body = _proj_b_kernel
    return pl.pallas_call(
        body, grid=grid, in_specs=in_specs, out_specs=out_specs, out_shape=out_shape,
        compiler_params=pltpu.CompilerParams(
            dimension_semantics=("parallel", "arbitrary"), vmem_limit_bytes=VMEM_LIMIT),
        name="proj_a" if is_a else "proj_b",
    )(x, *weights)


def _vis_end(qpos, n_lead, l_valid):
    ve = jnp.where(qpos < n_lead, n_lead,
                   n_lead + ((((qpos - n_lead) >> CHUNK_SHIFT) + 1) << CHUNK_SHIFT))
    return jnp.minimum(ve, l_valid)


def _key_blocks(i, tq, tk, q_off, n_lead, l_valid):
    return (_vis_end(q_off + i * tq + tq - 1, n_lead, l_valid) + tk - 1) // tk


def _dsa_select_kernel(qit_ref, kwt_ref, kib_ref, madd_ref,
                       keys_ref, hi_ref, lo_ref, *, tq, tk, q_off, n_lead, l_valid, topk):
    i = pl.program_id(1)
    qpos0 = q_off + i * tq
    nkb = _key_blocks(i, tq, tk, q_off, n_lead, l_valid)
    g = tk // SUBLANES

    def blk_rows(j):
        return pl.ds(pl.multiple_of(j * g, g), g)

    def blk_keys(j):
        return pl.ds(pl.multiple_of(j * tk, tk), tk)

    g16 = tk // PACKED_ROWS

    def blk_halves(j):
        return pl.ds(pl.multiple_of(j * g16, g16), g16)

    qpos = qpos0 + lax.broadcasted_iota(jnp.int32, (1, tq), 1)
    vis_end = _vis_end(qpos, n_lead, l_valid)

    def score_block(j, carry):
        kib = kib_ref[0, blk_keys(j), :]
        sc = jnp.zeros((tk, tq), jnp.float32)
        for ih in range(IDX_HEADS):
            s = jnp.dot(kib, qit_ref[0, ih * IDX_DIM:(ih + 1) * IDX_DIM, :],
                        preferred_element_type=jnp.float32)
            sc = sc + jnp.maximum(s, 0.0) * kwt_ref[0, IDX_DIM + ih:IDX_DIM + ih + 1, :]
        sc = jnp.where(sc == 0.0, 0.0, sc)
        bits = pltpu.bitcast(sc, jnp.int32)
        key = jnp.where(bits < 0, bits ^ 0x7FFFFFFF, bits)
        pos = j * tk + lax.broadcasted_iota(jnp.int32, (tk, tq), 0)
        key = jnp.where(pos < vis_end, key, KEY_NEG_INF)
        keys_ref[blk_rows(j)] = key.reshape(g, SUBLANES, tq)
        hi_ref[blk_halves(j)] = (key >> HALF_BITS).astype(jnp.int16).reshape(g16, PACKED_ROWS, tq)
        return carry

    lax.fori_loop(0, nkb, score_block, 0)

    def count(pred):
        def blk(j, cnt):
            hit = pred(keys_ref[blk_rows(j)], j).astype(jnp.int32)
            part = jnp.sum(hit.reshape(4, g // 4, SUBLANES, tq), axis=1)
            return cnt + jnp.sum(part, axis=0)
        cnt = lax.fori_loop(0, nkb, blk, jnp.zeros((SUBLANES, tq), jnp.int32))
        return jnp.broadcast_to(jnp.sum(cnt, axis=0, keepdims=True), (SUBLANES, tq))

    def count_half(half_ref, cand):
        cand16 = jnp.broadcast_to(cand, (PACKED_ROWS, tq)).astype(jnp.int16)

        def pair(jj, cnt):
            rows = pl.ds(pl.multiple_of(jj * (2 * g16), 2 * g16), 2 * g16)
            hit = jnp.where(half_ref[rows] >= cand16, jnp.bfloat16(1), jnp.bfloat16(0))
            parts = [hit[a] for a in range(2 * g16)]
            while len(parts) > 1:
                parts = [parts[a] + parts[a + 1] for a in range(0, len(parts), 2)]
            return cnt + parts[0].astype(jnp.float32)
        cnt = lax.fori_loop(0, (nkb + 1) // 2, pair, jnp.zeros((PACKED_ROWS, tq), jnp.float32))
        return jnp.sum(cnt, axis=0, keepdims=True).astype(jnp.int32)

    def fill_pair_partner(half_ref):
        half_ref[blk_halves(nkb)] = jnp.full((g16, PACKED_ROWS, tq), -HALF_BIAS, jnp.int16)

    def kth_largest_half(half_ref, k):
        def step(b, ut):
            cand = ut | (jnp.int32(1) << (HALF_BITS - 1 - b))
            cnt = count_half(half_ref, cand - HALF_BIAS)
            return jnp.where(cnt >= k, cand, ut)
        return lax.fori_loop(0, HALF_BITS, step, jnp.zeros((1, tq), jnp.int32)) - HALF_BIAS

    fill_pair_partner(hi_ref)
    thr_hi = kth_largest_half(hi_ref, topk)
    above = jnp.where(thr_hi == HALF_BIAS - 1, 0,
                      count_half(hi_ref, jnp.minimum(thr_hi + 1, HALF_BIAS - 1)))

    def low_halves(j, carry):
        kk = keys_ref[blk_rows(j)].reshape(tk, tq)
        lo = jnp.where((kk >> HALF_BITS) == thr_hi, (kk & HALF_MASK) - HALF_BIAS, -HALF_BIAS)
        lo_ref[blk_halves(j)] = lo.astype(jnp.int16).reshape(g16, PACKED_ROWS, tq)
        return carry

    lax.fori_loop(0, nkb, low_halves, 0)
    fill_pair_partner(lo_ref)
    thr_lo = kth_largest_half(lo_ref, topk - above)
    thr = jnp.broadcast_to((thr_hi << HALF_BITS) | (thr_lo + HALF_BIAS), (SUBLANES, tq))
    cnt_gt = count(lambda kk, j: kk > thr)
    cnt_ge = count(lambda kk, j: kk >= thr)
    excess = (cnt_ge > topk) & (thr != KEY_NEG_INF)
    any_excess = jnp.max(excess.astype(jnp.int32))
    thr_row = thr[0:1]
    may_keep = (topk - cnt_gt[0:1]).astype(jnp.float32)

    def store_mask(j, sel):
        sel = sel & (keys_ref[blk_rows(j)].reshape(tk, tq) != KEY_NEG_INF)
        madd_ref[0, 0, blk_keys(j), :] = jnp.where(sel, 0.0, NEG_BIG).astype(madd_ref.dtype)

    @pl.when(any_excess > 0)
    def _mask_tied():
        earlier = (lax.broadcasted_iota(jnp.int32, (tk, tk), 1)
                   < lax.broadcasted_iota(jnp.int32, (tk, tk), 0)).astype(jnp.bfloat16)

        def block(j, seen):
            kk = keys_ref[blk_rows(j)].reshape(tk, tq)
            tie = jnp.where(kk == thr_row, 1.0, 0.0)
            before = seen + jnp.dot(earlier, tie.astype(jnp.bfloat16),
                                    preferred_element_type=jnp.float32)
            store_mask(j, (kk > thr_row) | ((kk == thr_row) & (before < may_keep)))
            return before[tk - 1:tk] + tie[tk - 1:tk]

        lax.fori_loop(0, nkb, block, jnp.zeros((1, tq), jnp.float32))

    @pl.when(any_excess == 0)
    def _mask_plain():
        def block(j, carry):
            store_mask(j, keys_ref[blk_rows(j)].reshape(tk, tq) >= thr_row)
            return carry

        lax.fori_loop(0, nkb, block, 0)

    def mask_tail(j, carry):
        madd_ref[0, 0, blk_keys(j), :] = jnp.full((tk, tq), NEG_BIG, madd_ref.dtype)
        return carry

    lax.fori_loop(nkb, madd_ref.shape[2] // tk, mask_tail, 0)


def _dsa_attend_kernel(qt_ref, k_ref, vt_ref, madd_ref, bias_ref, o_ref, s0_ref, s1_ref, p1_ref,
                       *, tq, tk, q_off, n_lead, l_valid):
    i = pl.program_id(2)
    qpos0 = q_off + i * tq
    nkb = _key_blocks(i, tq, tk, q_off, n_lead, l_valid)
    wide = ATTN_GROUP * tk
    n_trips = (nkb + ATTN_GROUP - 1) // ATTN_GROUP
    jq = qpos0 // tk
    s_refs = (s0_ref, s1_ref)
    heads = [slice(hh * HEAD_DIM, (hh + 1) * HEAD_DIM) for hh in range(HEADS_PER_STEP)]

    def keys_of(t):
        return pl.ds(pl.multiple_of(t * wide, wide), wide)

    def scores(hh, t):
        s_refs[hh][...] = jnp.dot(k_ref[0, keys_of(t), heads[hh]], qt_ref[0, heads[hh], :],
                                  preferred_element_type=jnp.float32)

    def softmax(hh, t, m, near):
        parts = []
        for u in range(ATTN_GROUP):
            j = t * ATTN_GROUP + u
            madd = madd_ref[0, 0, pl.ds(pl.multiple_of(j * tk, tk), tk), :]
            part = s_refs[hh][u * tk:(u + 1) * tk] + madd.astype(jnp.float32)
            if near:
                part = part + bias_ref[jnp.clip(j - jq, -2, 1) + 2, hh]
            parts.append(part)
        m_new = m
        for part in parts:
            m_new = jnp.maximum(m_new, jnp.max(part, axis=0, keepdims=True))
        alpha = jnp.exp(m - m_new)
        pb = jnp.concatenate([jnp.exp(part - m_new).astype(jnp.bfloat16) for part in parts], axis=0)
        return m_new, alpha, pb

    ones = jnp.ones((SUBLANES, wide), jnp.bfloat16)

    def weighted_values(hh, t, pb):
        num = jnp.dot(vt_ref[0, heads[hh], keys_of(t)], pb, preferred_element_type=jnp.float32)
        den = jnp.dot(ones, pb, preferred_element_type=jnp.float32)
        return jnp.concatenate([num, den], axis=0)

    def trip(t, carries, near):
        (m0, acc0), (m1, acc1) = carries
        acc1 = acc1 + weighted_values(1, jnp.maximum(t - 1, 0), p1_ref[...])
        scores(1, t)
        m0, alpha0, pb0 = softmax(0, t, m0, near)
        acc0 = alpha0 * acc0 + weighted_values(0, t, pb0)
        scores(0, jnp.minimum(t + 1, n_trips - 1))
        m1, alpha1, pb1 = softmax(1, t, m1, near)
        p1_ref[...] = pb1
        return (m0, acc0), (m1, alpha1 * acc1)

    scores(0, 0)
    p1_ref[...] = jnp.zeros(p1_ref.shape, p1_ref.dtype)
    carries = tuple((jnp.full((1, tq), NEG_BIG, jnp.float32),
                     jnp.zeros((HEAD_DIM + SUBLANES, tq), jnp.float32)) for _ in range(2))
    n_far = jnp.clip((jq - 1) // ATTN_GROUP, 0, n_trips)
    carries = lax.fori_loop(0, n_far, functools.partial(trip, near=False), carries)
    (_, acc0), (_, acc1) = lax.fori_loop(n_far, n_trips, functools.partial(trip, near=True), carries)
    acc1 = acc1 + weighted_values(1, n_trips - 1, p1_ref[...])
    for hs, acc in zip(heads, (acc0, acc1)):
        o_ref[0, :, hs] = (acc[:HEAD_DIM] / acc[HEAD_DIM:HEAD_DIM + 1]).T.astype(o_ref.dtype)


def _attn_a(qt, qit, kwt, k, vt, kib, bias, *, tq, tk, q_off, n_lead, l_valid, topk):
    b, _, tqp = qt.shape
    lp = k.shape[1]
    assert lp % (ATTN_GROUP * tk) == 0 and tk % (4 * SUBLANES) == 0 and tk >= topk
    assert tqp % tq == 0 and q_off % tk == 0 and HEADS_PER_STEP == 2
    n_q = tqp // tq
    hw = HEADS_PER_STEP * HEAD_DIM
    geom = dict(tq=tq, tk=tk, q_off=q_off, n_lead=n_lead, l_valid=l_valid)
    madd = pl.pallas_call(
        functools.partial(_dsa_select_kernel, topk=topk, **geom),
        grid=(b, n_q),
        in_specs=[
            pl.BlockSpec((1, IDX_QW, tq), lambda bi, i: (bi, 0, i)),
            pl.BlockSpec((1, LANES, tq), lambda bi, i: (bi, 0, i)),
            pl.BlockSpec((1, lp, IDX_DIM), lambda bi, i: (bi, 0, 0)),
        ],
        out_specs=pl.BlockSpec((1, 1, lp, tq), lambda bi, i: (bi, i, 0, 0)),
        out_shape=jax.ShapeDtypeStruct((b, n_q, lp, tq), jnp.bfloat16),
        scratch_shapes=[
            pltpu.VMEM((lp // SUBLANES, SUBLANES, tq), jnp.int32),
            pltpu.VMEM(((lp + tk) // PACKED_ROWS, PACKED_ROWS, tq), jnp.int16),
            pltpu.VMEM(((lp + tk) // PACKED_ROWS, PACKED_ROWS, tq), jnp.int16),
        ],
        compiler_params=pltpu.CompilerParams(
            dimension_semantics=("parallel", "parallel"), vmem_limit_bytes=VMEM_LIMIT),
        name="dsa_select",
    )(qit, kwt, kib)
    return pl.pallas_call(
        functools.partial(_dsa_attend_kernel, **geom),
        grid=(b, N_HEADS // HEADS_PER_STEP, n_q),
        in_specs=[
            pl.BlockSpec((1, hw, tq), lambda bi, h, i: (bi, h, i)),
            pl.BlockSpec((1, lp, hw), lambda bi, h, i: (bi, 0, h)),
            pl.BlockSpec((1, hw, lp), lambda bi, h, i: (bi, h, 0)),
            pl.BlockSpec((1, 1, lp, tq), lambda bi, h, i: (bi, i, 0, 0)),
            pl.BlockSpec((4, HEADS_PER_STEP, tk, tq), lambda bi, h, i: (0, h, 0, 0)),
        ],
        out_specs=pl.BlockSpec((1, tq, hw), lambda bi, h, i: (bi, i, h)),
        out_shape=jax.ShapeDtypeStruct((b, tqp, HD), jnp.bfloat16),
        scratch_shapes=[
            pltpu.VMEM((ATTN_GROUP * tk, tq), jnp.float32),
            pltpu.VMEM((ATTN_GROUP * tk, tq), jnp.float32),
            pltpu.VMEM((ATTN_GROUP * tk, tq), jnp.bfloat16),
        ],
        compiler_params=pltpu.CompilerParams(
            dimension_semantics=("parallel", "parallel", "parallel"),
            vmem_limit_bytes=VMEM_LIMIT),
        name="dsa_attend",
    )(qt, k, vt, madd, bias)


def _rel_bucket(rel):
    nb = N_BUCKETS // 2
    max_exact = nb // 2
    ret = jnp.where(rel > 0, nb, 0)
    n = jnp.abs(rel)
    nf = jnp.maximum(n, 1).astype(jnp.float32)
    large = max_exact + (jnp.log(nf / max_exact) / math.log(MAX_DISTANCE / max_exact)
                         * (nb - max_exact)).astype(jnp.int32)
    large = jnp.minimum(large, nb - 1)
    return ret + jnp.where(n < max_exact, n, large)


def _bias_tiles(rel_bias, tq, tk):
    w = tk + tq
    tiles = []
    for d in range(-2, 2):
        rel = d * tk + tk - 1 - jnp.arange(w, dtype=jnp.int32)
        vec = rel_bias.astype(jnp.float32)[_rel_bucket(rel)].T
        vec = jnp.roll(vec, -(tk - 1), axis=1)
        skew = jnp.tile(vec, (1, tk))[:, :tk * (w - 1)].reshape(-1, tk, w - 1)
        tiles.append(skew[:, :, :tq])
    tiles = jnp.stack(tiles)
    return tiles - tiles[0, :, :1, :1]


def _attn_b_kernel(q_ref, k_ref, v_ref, tri_ref, o_ref, *, tq, tk, q_off):
    i = pl.program_id(2)
    qpos0 = q_off + i * tq
    heads = [slice(hh * HEAD_DIM, (hh + 1) * HEAD_DIM) for hh in range(SB_HEADS_PER_STEP)]
    tri = tri_ref[...]

    def block(j, diagonal, runs, accs):
        ds = pl.ds(pl.multiple_of(j * tk, tk), tk)
        if diagonal:
            causal = ((j * tk + lax.broadcasted_iota(jnp.int32, (tq, tk), 1))
                      < (qpos0 + lax.broadcasted_iota(jnp.int32, (tq, tk), 0)))
        zs = [lax.dot_general(q_ref[0, :, hs], k_ref[0, ds, hs], _NT,
                              preferred_element_type=jnp.float32) for hs in heads]
        tails, log_1ms, withins = [], [], []
        for z in zs:
            tail = jnp.log(1.0 + jnp.exp(-jnp.abs(z)))
            log_1m = -jnp.maximum(z, 0.0) - tail
            if diagonal:
                log_1m = jnp.where(causal, log_1m, 0.0)
            hi = log_1m.astype(jnp.bfloat16)
            lo = (log_1m - hi.astype(jnp.float32)).astype(jnp.bfloat16)
            withins.append(jnp.dot(hi, tri, preferred_element_type=jnp.float32)
                           + jnp.dot(lo, tri, preferred_element_type=jnp.float32))
            tails.append(tail)
            log_1ms.append(log_1m)
        new_runs, new_accs = [], []
        for hh, hs in enumerate(heads):
            a = jnp.exp(jnp.minimum(zs[hh], 0.0) - tails[hh] + withins[hh] + runs[hh])
            if diagonal:
                a = jnp.where(causal, a, 0.0)
            new_accs.append(accs[hh] + jnp.dot(a.astype(jnp.bfloat16), v_ref[0, ds, hs],
                                               preferred_element_type=jnp.float32))
            new_runs.append(runs[hh] + withins[hh][:, 0:1] + log_1ms[hh][:, 0:1])
        return tuple(new_runs), tuple(new_accs)

    def cond(carry):
        j, runs, _ = carry
        return (j >= 0) & (jnp.max(functools.reduce(jnp.maximum, runs)) > EXP_ZERO_BELOW)

    def body(carry):
        j, runs, accs = carry
        runs, accs = block(j, False, runs, accs)
        return j - 1, runs, accs

    j0 = (qpos0 + tq - 2) // tk
    runs = tuple(jnp.zeros((tq, 1), jnp.float32) for _ in heads)
    accs = tuple(jnp.zeros((tq, HEAD_DIM), jnp.float32) for _ in heads)
    runs, accs = block(j0, True, runs, accs)
    _, _, accs = lax.while_loop(cond, body, (j0 - 1, runs, accs))
    for hs, acc in zip(heads, accs):
        o_ref[0, :, hs] = acc.astype(o_ref.dtype)


def _attn_b(q, k, v, *, tq, tk, q_off):
    b, tqp, _ = q.shape
    lp = k.shape[1]
    tri = (jnp.arange(tk)[:, None] > jnp.arange(tk)[None, :]).astype(jnp.bfloat16)
    assert q_off % tk == 0 and tq <= tk and (tq == tk or tqp == tq)
    hw = SB_HEADS_PER_STEP * HEAD_DIM
    kern = functools.partial(_attn_b_kernel, tq=tq, tk=tk, q_off=q_off)
    return pl.pallas_call(
        kern, grid=(b, N_HEADS // SB_HEADS_PER_STEP, tqp // tq),
        in_specs=[
            pl.BlockSpec((1, tq, hw), lambda bi, h, i: (bi, i, h)),
            pl.BlockSpec((1, lp, hw), lambda bi, h, i: (bi, 0, h)),
            pl.BlockSpec((1, lp, hw), lambda bi, h, i: (bi, 0, h)),
            _const_spec((tk, tk)),
        ],
        out_specs=pl.BlockSpec((1, tq, hw), lambda bi, h, i: (bi, i, h)),
        out_shape=jax.ShapeDtypeStruct((b, tqp, HD), jnp.bfloat16),
        compiler_params=pltpu.CompilerParams(
            dimension_semantics=("parallel", "parallel", "parallel"),
            vmem_limit_bytes=VMEM_LIMIT),
        name="stick_breaking_attention",
    )(q, k, v, tri)


def _post_kernel(x_ref, o_ref, left_ref, wo_ref, g1_ref, b1_ref, wg_ref, wu_ref, cw_ref, cb_ref,
                 wd_ref, g2_ref, b2_ref, y_ref, conv_ref, carry_ref,
                 *, tm, d_ff, alpha, r_last):
    t = pl.program_id(1)

    @pl.when(t == 0)
    def _init():
        carry_ref[...] = left_ref[0]

    mix = jnp.dot(o_ref[0], wo_ref[...], preferred_element_type=jnp.float32)
    x1 = _layer_norm(alpha * x_ref[0] + mix, g1_ref[...], b1_ref[...])
    xb = x1.astype(jnp.bfloat16)
    row = lax.broadcasted_iota(jnp.int32, (tm, FF_CHUNK), 0)
    acc = jnp.zeros((tm, D_MODEL), jnp.float32)
    n_chunks = d_ff // FF_CHUNK

    def gate_up(c):
        cs = slice(c * FF_CHUNK, (c + 1) * FF_CHUNK)
        return (jnp.dot(xb, wg_ref[:, cs], preferred_element_type=jnp.float32),
                jnp.dot(xb, wu_ref[:, cs], preferred_element_type=jnp.float32))

    nxt = gate_up(0)
    for c in range(n_chunks):
        cs = slice(c * FF_CHUNK, (c + 1) * FF_CHUNK)
        g, u = nxt
        if c + 1 < n_chunks:
            nxt = gate_up(c + 1)
        prev = carry_ref[:, cs]
        g1 = jnp.where(row == 0, prev[7:8], pltpu.roll(g, 1, 0))
        g2 = jnp.where(row == 0, prev[6:7], jnp.where(row == 1, prev[7:8], pltpu.roll(g, 2, 0)))
        gc = cb_ref[:, cs] + cw_ref[0:1, cs] * g2
        gc = gc + cw_ref[1:2, cs] * g1
        gc = gc + cw_ref[2:3, cs] * g
        hid = jax.nn.gelu(gc) * u
        acc = acc + jnp.dot(hid.astype(jnp.bfloat16), wd_ref[cs, :],
                            preferred_element_type=jnp.float32)
        carry_ref[:, cs] = g[tm - SUBLANES:tm]
        conv_ref[0, :, cs] = g[r_last:r_last + SUBLANES]

    y_ref[0] = _layer_norm(alpha * x1 + acc, g2_ref[...], b2_ref[...])


def _post(x, o, left8, t_valid, tm, w, alpha):
    b, tp, _ = x.shape
    d_ff = w["wg"].shape[1]
    assert d_ff % FF_CHUNK == 0 and t_valid % SUBLANES == 0 and t_valid >= SUBLANES
    assert (t_valid - 1) // tm == tp // tm - 1
    r_last = (t_valid - SUBLANES) % tm
    row = lambda width: pl.BlockSpec((1, tm, width), lambda i, j: (i, j, 0))
    vec = lambda a: a.reshape(1, -1).astype(jnp.float32)
    consts = [w["wo"], vec(w["g1"]), vec(w["b1"]), w["wg"], w["wu"], w["cw"].astype(jnp.float32),
              vec(w["cb"]), w["wd"], vec(w["g2"]), vec(w["b2"])]
    kern = functools.partial(_post_kernel, tm=tm, d_ff=d_ff, alpha=alpha, r_last=r_last)
    return pl.pallas_call(
        kern, grid=(b, tp // tm),
        in_specs=[row(D_MODEL), row(HD), pl.BlockSpec((1, SUBLANES, d_ff), lambda i, j: (i, 0, 0))]
                 + [_const_spec(c.shape) for c in consts],
        out_specs=[row(D_MODEL), pl.BlockSpec((1, SUBLANES, d_ff), lambda i, j: (i, 0, 0))],
        out_shape=[jax.ShapeDtypeStruct((b, tp, D_MODEL), jnp.float32),
                   jax.ShapeDtypeStruct((b, SUBLANES, d_ff), jnp.float32)],
        scratch_shapes=[pltpu.VMEM((SUBLANES, d_ff), jnp.float32)],
        compiler_params=pltpu.CompilerParams(
            dimension_semantics=("parallel", "arbitrary"), vmem_limit_bytes=VMEM_LIMIT),
        name="outproj_ln_convffn_ln",
    )(x, o, left8, *consts)


def _split_a(w_in):
    bf = jnp.bfloat16
    off_qi = 3 * HD
    off_ki = off_qi + IDX_QW
    wq, wk, wv = w_in[:, :HD], w_in[:, HD:2 * HD], w_in[:, 2 * HD:3 * HD]
    wqi = w_in[:, off_qi:off_ki]
    wkw = jnp.pad(w_in[:, off_ki:], ((0, 0), (0, LANES - IDX_DIM - IDX_HEADS)))
    row_scale = jnp.concatenate([
        jnp.ones((IDX_DIM,), jnp.float32),
        jnp.full((IDX_HEADS,), IDX_QW ** -0.5, jnp.float32),
        jnp.zeros((LANES - IDX_DIM - IDX_HEADS,), jnp.float32)]).reshape(LANES, 1)
    return [wq.T.astype(bf), wk.astype(bf), wv.astype(bf), wv.T.astype(bf), wqi.T.astype(bf),
            wkw.astype(bf), wkw.T.astype(bf), row_scale]


def _split_b(w_in):
    bf = jnp.bfloat16
    return [w_in[:, :HD].astype(bf), w_in[:, HD:2 * HD].astype(bf), w_in[:, 2 * HD:].astype(bf)]


def _with_cache(cache, new_bf, lp):
    b, p = cache.shape[:2]
    flat = cache.reshape(b, p, -1).astype(jnp.bfloat16)
    t = new_bf.shape[1]
    return jnp.pad(jnp.concatenate([flat, new_bf], axis=1), ((0, 0), (0, lp - p - t), (0, 0)))


def kernel(x_prompt, x_sample, cache_a_k, cache_a_v, cache_a_idx_k, cache_b_k, cache_b_v,
           state_ffn_conv, meta_tokens, rel_bias, w_a_in, w_a_out, w_b_in, w_b_out,
           ln1_g, ln1_b, ln2_g, ln2_b, w_ffn_gate, w_ffn_up, ffn_conv_w, ffn_conv_b, w_ffn_down):
    bp, seq, _ = x_prompt.shape
    bs, ts, _ = x_sample.shape
    depth = ln1_g.shape[0]
    d_ff = w_ffn_gate.shape[2]
    past = cache_a_k.shape[2]
    alpha = (2 * depth) ** 0.25
    t_p = N_META + seq
    tp_pad = _round_up(t_p, ROW_TILE)
    ls = past + ts
    ls_pad = _round_up(ls, ATTN_GROUP * KEY_TILE)
    assert past % KEY_TILE == 0 and ts % SUBLANES == 0 and ts <= SAMPLE_TILE
    topk_p = min(TOPK_MAX, seq // 4)
    topk_s = min(TOPK_MAX, ls // 4)
    bf = jnp.bfloat16

    meta = jnp.broadcast_to(meta_tokens.astype(x_prompt.dtype)[None], (bp, N_META, D_MODEL))
    xp = jnp.pad(jnp.concatenate([meta, x_prompt], axis=1), ((0, 0), (0, tp_pad - t_p), (0, 0)))
    xs = x_sample
    bias_p = _bias_tiles(rel_bias, ROW_TILE, KEY_TILE)
    bias_s = _bias_tiles(rel_bias, SAMPLE_TILE, KEY_TILE)
    left_p = jnp.zeros((bp, SUBLANES, d_ff), jnp.float32)
    wg_all, wu_all, wd_all = w_ffn_gate.astype(bf), w_ffn_up.astype(bf), w_ffn_down.astype(bf)
    wo_a, wo_b = w_a_out.astype(bf), w_b_out.astype(bf)

    heads = lambda a: a.reshape(a.shape[0], a.shape[1], N_HEADS, HEAD_DIM)
    outs = {k: [] for k in ("akp", "avp", "aip", "aks", "avs", "ais",
                            "bkp", "bvp", "bks", "bvs", "cp", "cs")}
    for i in range(depth):
        j = i // N_MIXERS
        if i % N_MIXERS == 0:
            wts = _split_a(w_a_in[j])
            qt, kc, kb, vc, vt, qit, kic, kib, kwt = _project(
                xp, t_p, ROW_TILE, wts, True, key_rows=_round_up(tp_pad, ATTN_GROUP * KEY_TILE))
            op = _attn_a(qt, qit, kwt, kb, vt, kib, bias_p,
                         tq=ROW_TILE, tk=KEY_TILE, q_off=0, n_lead=N_META, l_valid=t_p, topk=topk_p)
            outs["akp"].append(heads(kc)); outs["avp"].append(heads(vc)); outs["aip"].append(kic)
            xs_pad = jnp.pad(xs, ((0, 0), (0, SAMPLE_TILE - ts), (0, 0)))
            qt, kc, kb, vc, vt, qit, kic, kib, kwt = _project(
                xs_pad, ts, SAMPLE_TILE, wts, True, key_rows=SAMPLE_TILE)
            cache_vt = jnp.swapaxes(cache_a_v[j].reshape(bs, past, HD).astype(bf), 1, 2)
            vt_all = jnp.pad(jnp.concatenate([cache_vt, vt[:, :, :ts]], axis=2),
                             ((0, 0), (0, 0), (0, ls_pad - ls)))
            os_ = _attn_a(qt, qit, kwt, _with_cache(cache_a_k[j], kb[:, :ts], ls_pad), vt_all,
                          _with_cache(cache_a_idx_k[j], kib[:, :ts], ls_pad), bias_s,
                          tq=SAMPLE_TILE, tk=KEY_TILE, q_off=past, n_lead=0, l_valid=ls,
                          topk=topk_s)[:, :ts]
            outs["aks"].append(heads(kc)); outs["avs"].append(heads(vc)); outs["ais"].append(kic)
            w_out = wo_a[j]
        else:
            wts = _split_b(w_b_in[j])
            q, kc, kb, vc, vb = _project(xp, t_p, ROW_TILE, wts, False)
            op = _attn_b(q, kb, vb, tq=ROW_TILE, tk=KEY_TILE, q_off=0)
            outs["bkp"].append(heads(kc)); outs["bvp"].append(heads(vc))
            q, kc, kb, vc, vb = _project(xs, ts, ts, wts, False)
            os_ = _attn_b(q, _with_cache(cache_b_k[j], kb, ls_pad),
                          _with_cache(cache_b_v[j], vb, ls_pad), tq=ts, tk=KEY_TILE, q_off=past)
            outs["bks"].append(heads(kc)); outs["bvs"].append(heads(vc))
            w_out = wo_b[j]
        w = dict(wo=w_out, g1=ln1_g[i], b1=ln1_b[i], wg=wg_all[i], wu=wu_all[i], cw=ffn_conv_w[i],
                 cb=ffn_conv_b[i], wd=wd_all[i], g2=ln2_g[i], b2=ln2_b[i])
        xp, conv_p = _post(xp, op, left_p, t_p, ROW_TILE, w, alpha)
        left_s = jnp.pad(state_ffn_conv[i].astype(jnp.float32),
                         ((0, 0), (SUBLANES - (CONV_W - 1), 0), (0, 0)))
        xs, conv_s = _post(xs, os_, left_s, ts, ts, w, alpha)
        outs["cp"].append(conv_p[:, SUBLANES - (CONV_W - 1):])
        outs["cs"].append(conv_s[:, SUBLANES - (CONV_W - 1):])

    st = jnp.stack
    return (xp[:, N_META:t_p], xs,
            st(outs["akp"]), st(outs["avp"]), st(outs["aip"]),
            st(outs["bkp"]), st(outs["bvp"]), st(outs["cp"]),
            st(outs["aks"]), st(outs["avs"]), st(outs["ais"]),
            st(outs["bks"]), st(outs["bvs"]), st(outs["cs"]))
```

```python
import functools
import math

import jax
import jax.numpy as jnp
from jax import lax
from jax.experimental import pallas as pl
from jax.experimental.pallas import tpu as pltpu

D_MODEL = 1024
N_HEADS = 8
HEAD_DIM = 128
HD = N_HEADS * HEAD_DIM
IDX_HEADS = 8
IDX_DIM = 64
IDX_QW = IDX_HEADS * IDX_DIM
CHUNK = 64
CHUNK_SHIFT = 6
N_META = 16
N_MIXERS = 2
TOPK_MAX = 256
N_BUCKETS = 32
MAX_DISTANCE = 128
CONV_W = 3
LN_EPS = 1e-5

LANES = 128
SUBLANES = 8
ROW_TILE = 256
KEY_TILE = 256
FF_CHUNK = 256
ATTN_GROUP = 4
HEADS_PER_STEP = 2
SB_HEADS_PER_STEP = 4
SAMPLE_TILE = 128
VMEM_LIMIT = 56 * 1024 * 1024

NEG_BIG = -1e30
INT_MIN = -2 ** 31
KEY_NEG_INF = -2139095041
HALF_BITS = 16
HALF_BIAS = 1 << (HALF_BITS - 1)
HALF_MASK = (1 << HALF_BITS) - 1
PACKED_ROWS = 16
EXP_ZERO_BELOW = -104.0

_NT = (((1,), (1,)), ((), ()))


def _round_up(n, m):
    return -(-n // m) * m


def _layer_norm(x, g, b):
    mu = jnp.mean(x, -1, keepdims=True)
    xc = x - mu
    var = jnp.mean(xc * xc, -1, keepdims=True)
    return xc * lax.rsqrt(var + LN_EPS) * g + b


def _proj_a_kernel(x_ref, wqt_ref, wk_ref, wv_ref, wvt_ref, wqit_ref, wkw_ref, wkwt_ref, rs_ref,
                   qt_ref, kc_ref, kb_ref, vc_ref, vt_ref, qit_ref, kic_ref, kib_ref, kwt_ref,
                   *, n_tiles):
    live = pl.program_id(1) < n_tiles
    xb = x_ref[0].astype(jnp.bfloat16)

    def cols(wt_ref):
        return lax.dot_general(wt_ref[...], xb, _NT, preferred_element_type=jnp.float32)

    qt_ref[0] = (cols(wqt_ref) * (HEAD_DIM ** -0.5)).astype(jnp.bfloat16)
    k = jnp.dot(xb, wk_ref[...], preferred_element_type=jnp.float32)
    kc_ref[0] = k
    kb_ref[0] = jnp.where(live, k, 0.0).astype(jnp.bfloat16)
    vc_ref[0] = jnp.dot(xb, wv_ref[...], preferred_element_type=jnp.float32)
    vt_ref[0] = jnp.where(live, cols(wvt_ref), 0.0).astype(jnp.bfloat16)
    qit_ref[0] = cols(wqit_ref).astype(jnp.bfloat16)
    kw = jnp.dot(xb, wkw_ref[...], preferred_element_type=jnp.float32)
    kic_ref[0] = kw[:, :IDX_DIM]
    kib_ref[0] = jnp.where(live, kw[:, :IDX_DIM], 0.0).astype(jnp.bfloat16)
    kwt_ref[0] = cols(wkwt_ref) * rs_ref[...]


def _proj_b_kernel(x_ref, wq_ref, wk_ref, wv_ref, q_ref, kc_ref, kb_ref, vc_ref, vb_ref):
    xb = x_ref[0].astype(jnp.bfloat16)
    q = jnp.dot(xb, wq_ref[...], preferred_element_type=jnp.float32)
    q_ref[0] = (q * (HEAD_DIM ** -0.5)).astype(jnp.bfloat16)
    k = jnp.dot(xb, wk_ref[...], preferred_element_type=jnp.float32)
    kc_ref[0] = k
    kb_ref[0] = k.astype(jnp.bfloat16)
    v = jnp.dot(xb, wv_ref[...], preferred_element_type=jnp.float32)
    vc_ref[0] = v
    vb_ref[0] = v.astype(jnp.bfloat16)


def _const_spec(shape):
    return pl.BlockSpec(shape, lambda *_: (0,) * len(shape))


def _project(x, t_valid, tm, weights, is_a, key_rows=None):
    b, tp, _ = x.shape
    n_tiles = tp // tm
    grid = (b, n_tiles)
    row = lambda w: pl.BlockSpec((1, tm, w), lambda i, j: (i, j, 0))
    bf = jnp.bfloat16
    f32 = jnp.float32
    pad = lambda w, dt: jax.ShapeDtypeStruct((b, tp, w), dt)
    exact = lambda w, dt: jax.ShapeDtypeStruct((b, t_valid, w), dt)
    in_specs = [row(D_MODEL)] + [_const_spec(w.shape) for w in weights]
    if is_a:
        assert key_rows % tm == 0 and key_rows >= tp
        grid = (b, key_rows // tm)
        last = n_tiles - 1
        row = lambda w: pl.BlockSpec((1, tm, w), lambda i, j: (i, jnp.minimum(j, last), 0))
        col = lambda w: pl.BlockSpec((1, w, tm), lambda i, j: (i, 0, jnp.minimum(j, last)))
        key_row = lambda w: pl.BlockSpec((1, tm, w), lambda i, j: (i, j, 0))
        key_col = lambda w: pl.BlockSpec((1, w, tm), lambda i, j: (i, 0, j))
        padt = lambda w, dt: jax.ShapeDtypeStruct((b, w, tp), dt)
        out_shape = [padt(HD, bf), exact(HD, f32), jax.ShapeDtypeStruct((b, key_rows, HD), bf),
                     exact(HD, f32), jax.ShapeDtypeStruct((b, HD, key_rows), bf),
                     padt(IDX_QW, bf), exact(IDX_DIM, f32),
                     jax.ShapeDtypeStruct((b, key_rows, IDX_DIM), bf), padt(LANES, f32)]
        out_specs = [col(HD), row(HD), key_row(HD), row(HD), key_col(HD),
                     col(IDX_QW), row(IDX_DIM), key_row(IDX_DIM), col(LANES)]
        in_specs[0] = row(D_MODEL)
        body = functools.partial(_proj_a_kernel, n_tiles=n_tiles)
    else:
        out_shape = [pad(HD, bf), exact(HD, f32), pad(HD, bf), exact(HD, f32), pad(HD, bf)]
        out_specs = [row(HD)] * 5
        body = _proj_b_kernel
    return pl.pallas_call(
        body, grid=grid, in_specs=in_specs, out_specs=out_specs, out_shape=out_shape,
        compiler_params=pltpu.CompilerParams(
            dimension_semantics=("parallel", "arbitrary"), vmem_limit_bytes=VMEM_LIMIT),
        name="proj_a" if is_a else "proj_b",
    )(x, *weights)


def _vis_end(qpos, n_lead, l_valid):
    ve = jnp.where(qpos < n_lead, n_lead,
                   n_lead + ((((qpos - n_lead) >> CHUNK_SHIFT) + 1) << CHUNK_SHIFT))
    return jnp.minimum(ve, l_valid)


def _key_blocks(i, tq, tk, q_off, n_lead, l_valid):
    return (_vis_end(q_off + i * tq + tq - 1, n_lead, l_valid) + tk - 1) // tk


def _dsa_select_kernel(qit_ref, kwt_ref, kib_ref, madd_ref,
                       keys_ref, hi_ref, lo_ref, *, tq, tk, q_off, n_lead, l_valid, topk):
    i = pl.program_id(1)
    qpos0 = q_off + i * tq
    nkb = _key_blocks(i, tq, tk, q_off, n_lead, l_valid)
    g = tk // SUBLANES

    def blk_rows(j):
        return pl.ds(pl.multiple_of(j * g, g), g)

    def blk_keys(j):
        return pl.ds(pl.multiple_of(j * tk, tk), tk)

    g16 = tk // PACKED_ROWS

    def blk_halves(j):
        return pl.ds(pl.multiple_of(j * g16, g16), g16)

    qpos = qpos0 + lax.broadcasted_iota(jnp.int32, (1, tq), 1)
    vis_end = _vis_end(qpos, n_lead, l_valid)

    def score_pair(jj, carry):
        n = 2 * tk
        kib = kib_ref[0, pl.ds(pl.multiple_of(jj * n, n), n), :]
        sc = jnp.zeros((n, tq), jnp.float32)
        for ih in range(IDX_HEADS):
            s = jnp.dot(kib, qit_ref[0, ih * IDX_DIM:(ih + 1) * IDX_DIM, :],
                        preferred_element_type=jnp.float32)
            sc = sc + jnp.maximum(s, 0.0) * kwt_ref[0, IDX_DIM + ih:IDX_DIM + ih + 1, :]
        sc = jnp.where(sc == 0.0, 0.0, sc)
        bits = pltpu.bitcast(sc, jnp.int32)
        key = jnp.where(bits < 0, bits ^ 0x7FFFFFFF, bits)
        pos = jj * n + lax.broadcasted_iota(jnp.int32, (n, tq), 0)
        key = jnp.where(pos < vis_end, key, KEY_NEG_INF)
        keys_ref[pl.ds(pl.multiple_of(jj * (2 * g), 2 * g), 2 * g)] = key.reshape(2 * g, SUBLANES, tq)
        hi_ref[pl.ds(pl.multiple_of(jj * (2 * g16), 2 * g16), 2 * g16)] = (
            (key >> HALF_BITS).astype(jnp.int16).reshape(2 * g16, PACKED_ROWS, tq))
        return carry

    lax.fori_loop(0, (nkb + 1) // 2, score_pair, 0)

    def count(pred):
        def blk(j, cnt):
            hit = pred(keys_ref[blk_rows(j)], j).astype(jnp.int32)
            part = jnp.sum(hit.reshape(4, g // 4, SUBLANES, tq), axis=1)
            return cnt + jnp.sum(part, axis=0)
        cnt = lax.fori_loop(0, nkb, blk, jnp.zeros((SUBLANES, tq), jnp.int32))
        return jnp.broadcast_to(jnp.sum(cnt, axis=0, keepdims=True), (SUBLANES, tq))

    def count_half(half_ref, cand):
        cand16 = jnp.broadcast_to(cand, (PACKED_ROWS, tq)).astype(jnp.int16)

        def pair(jj, cnt):
            rows = pl.ds(pl.multiple_of(jj * (2 * g16), 2 * g16), 2 * g16)
            hit = jnp.where(half_ref[rows] >= cand16, jnp.bfloat16(1), jnp.bfloat16(0))
            parts = [hit[a] for a in range(2 * g16)]
            while len(parts) > 1:
                parts = [parts[a] + parts[a + 1] for a in range(0, len(parts), 2)]
            return cnt + parts[0].astype(jnp.float32)
        cnt = lax.fori_loop(0, (nkb + 1) // 2, pair, jnp.zeros((PACKED_ROWS, tq), jnp.float32))
        return jnp.sum(cnt, axis=0, keepdims=True).astype(jnp.int32)

    def fill_pair_partner(half_ref):
        half_ref[blk_halves(nkb)] = jnp.full((g16, PACKED_ROWS, tq), -HALF_BIAS, jnp.int16)

    def kth_largest_half(half_ref, k):
        def step(b, ut):
            cand = ut | (jnp.int32(1) << (HALF_BITS - 1 - b))
            cnt = count_half(half_ref, cand - HALF_BIAS)
            return jnp.where(cnt >= k, cand, ut)
        return lax.fori_loop(0, HALF_BITS, step, jnp.zeros((1, tq), jnp.int32)) - HALF_BIAS

    fill_pair_partner(hi_ref)
    thr_hi = kth_largest_half(hi_ref, topk)
    above = jnp.where(thr_hi == HALF_BIAS - 1, 0,
                      count_half(hi_ref, jnp.minimum(thr_hi + 1, HALF_BIAS - 1)))

    def low_halves(j, carry):
        kk = keys_ref[blk_rows(j)].reshape(tk, tq)
        lo = jnp.where((kk >> HALF_BITS) == thr_hi, (kk & HALF_MASK) - HALF_BIAS, -HALF_BIAS)
        lo_ref[blk_halves(j)] = lo.astype(jnp.int16).reshape(g16, PACKED_ROWS, tq)
        return carry

    lax.fori_loop(0, nkb, low_halves, 0)
    fill_pair_partner(lo_ref)
    thr_lo = kth_largest_half(lo_ref, topk - above)
    thr = jnp.broadcast_to((thr_hi << HALF_BITS) | (thr_lo + HALF_BIAS), (SUBLANES, tq))
    cnt_gt = count(lambda kk, j: kk > thr)
    cnt_ge = count(lambda kk, j: kk >= thr)
    excess = (cnt_ge > topk) & (thr != KEY_NEG_INF)
    any_excess = jnp.max(excess.astype(jnp.int32))
    thr_row = thr[0:1]
    may_keep = (topk - cnt_gt[0:1]).astype(jnp.float32)

    def store_mask(j, sel):
        sel = sel & (keys_ref[blk_rows(j)].reshape(tk, tq) != KEY_NEG_INF)
        madd_ref[0, 0, blk_keys(j), :] = jnp.where(sel, 0.0, NEG_BIG).astype(madd_ref.dtype)

    @pl.when(any_excess > 0)
    def _mask_tied():
        earlier = (lax.broadcasted_iota(jnp.int32, (tk, tk), 1)
                   < lax.broadcasted_iota(jnp.int32, (tk, tk), 0)).astype(jnp.bfloat16)

        def block(j, seen):
            kk = keys_ref[blk_rows(j)].reshape(tk, tq)
            tie = jnp.where(kk == thr_row, 1.0, 0.0)
            before = seen + jnp.dot(earlier, tie.astype(jnp.bfloat16),
                                    preferred_element_type=jnp.float32)
            store_mask(j, (kk > thr_row) | ((kk == thr_row) & (before < may_keep)))
            return before[tk - 1:tk] + tie[tk - 1:tk]

        lax.fori_loop(0, nkb, block, jnp.zeros((1, tq), jnp.float32))

    @pl.when(any_excess == 0)
    def _mask_plain():
        def block(j, carry):
            store_mask(j, keys_ref[blk_rows(j)].reshape(tk, tq) >= thr_row)
            return carry

        lax.fori_loop(0, nkb, block, 0)

    def mask_tail(j, carry):
        madd_ref[0, 0, blk_keys(j), :] = jnp.full((tk, tq), NEG_BIG, madd_ref.dtype)
        return carry

    lax.fori_loop(nkb, madd_ref.shape[2] // tk, mask_tail, 0)


def _dsa_attend_kernel(qt_ref, k_ref, vt_ref, madd_ref, bias_ref, o_ref, s0_ref, s1_ref, p1_ref,
                       *, tq, tk, q_off, n_lead, l_valid):
    i = pl.program_id(2)
    qpos0 = q_off + i * tq
    nkb = _key_blocks(i, tq, tk, q_off, n_lead, l_valid)
    wide = ATTN_GROUP * tk
    n_trips = (nkb + ATTN_GROUP - 1) // ATTN_GROUP
    jq = qpos0 // tk
    s_refs = (s0_ref, s1_ref)
    heads = [slice(hh * HEAD_DIM, (hh + 1) * HEAD_DIM) for hh in range(HEADS_PER_STEP)]

    def keys_of(t):
        return pl.ds(pl.multiple_of(t * wide, wide), wide)

    def scores(hh, t):
        s_refs[hh][...] = jnp.dot(k_ref[0, keys_of(t), heads[hh]], qt_ref[0, heads[hh], :],
                                  preferred_element_type=jnp.float32)

    def softmax(hh, t, m, near):
        parts = []
        for u in range(ATTN_GROUP):
            j = t * ATTN_GROUP + u
            madd = madd_ref[0, 0, pl.ds(pl.multiple_of(j * tk, tk), tk), :]
            part = s_refs[hh][u * tk:(u + 1) * tk] + madd.astype(jnp.float32)
            if near:
                part = part + bias_ref[jnp.clip(j - jq, -2, 1) + 2, hh]
            parts.append(part)
        m_new = m
        for part in parts:
            m_new = jnp.maximum(m_new, jnp.max(part, axis=0, keepdims=True))
        alpha = jnp.exp(m - m_new)
        pb = jnp.concatenate([jnp.exp(part - m_new).astype(jnp.bfloat16) for part in parts], axis=0)
        return m_new, alpha, pb

    ones = jnp.ones((SUBLANES, wide), jnp.bfloat16)

    def weighted_values(hh, t, pb):
        num = jnp.dot(vt_ref[0, heads[hh], keys_of(t)], pb, preferred_element_type=jnp.float32)
        den = jnp.dot(ones, pb, preferred_element_type=jnp.float32)
        return jnp.concatenate([num, den], axis=0)

    def trip(t, carries, near, last=False):
        (m0, acc0), (m1, acc1) = carries
        acc1 = acc1 + weighted_values(1, jnp.maximum(t - 1, 0), p1_ref[...])
        scores(1, t)
        m0, alpha0, pb0 = softmax(0, t, m0, near)
        acc0 = alpha0 * acc0 + weighted_values(0, t, pb0)
        if not last:
            scores(0, t + 1)
        m1, alpha1, pb1 = softmax(1, t, m1, near)
        if last:
            return (m0, acc0), (m1, alpha1 * acc1 + weighted_values(1, t, pb1))
        p1_ref[...] = pb1
        return (m0, acc0), (m1, alpha1 * acc1)

    scores(0, 0)
    p1_ref[...] = jnp.zeros(p1_ref.shape, p1_ref.dtype)
    carries = tuple((jnp.full((1, tq), NEG_BIG, jnp.float32),
                     jnp.zeros((HEAD_DIM + SUBLANES, tq), jnp.float32)) for _ in range(2))
    n_far = jnp.clip((jq - 1) // ATTN_GROUP, 0, n_trips - 1)
    carries = lax.fori_loop(0, n_far, functools.partial(trip, near=False), carries)
    carries = lax.fori_loop(n_far, n_trips - 1, functools.partial(trip, near=True), carries)
    (_, acc0), (_, acc1) = trip(n_trips - 1, carries, near=True, last=True)
    for hs, acc in zip(heads, (acc0, acc1)):
        o_ref[0, :, hs] = (acc[:HEAD_DIM] / acc[HEAD_DIM:HEAD_DIM + 1]).T.astype(o_ref.dtype)


def _attn_a(qt, qit, kwt, k, vt, kib, bias, *, tq, tk, q_off, n_lead, l_valid, topk):
    b, _, tqp = qt.shape
    lp = k.shape[1]
    assert lp % (ATTN_GROUP * tk) == 0 and tk % (4 * SUBLANES) == 0 and tk >= topk
    assert tqp % tq == 0 and q_off % tk == 0 and HEADS_PER_STEP == 2
    n_q = tqp // tq
    hw = HEADS_PER_STEP * HEAD_DIM
    geom = dict(tq=tq, tk=tk, q_off=q_off, n_lead=n_lead, l_valid=l_valid)
    madd = pl.pallas_call(
        functools.partial(_dsa_select_kernel, topk=topk, **geom),
        grid=(b, n_q),
        in_specs=[
            pl.BlockSpec((1, IDX_QW, tq), lambda bi, i: (bi, 0, i)),
            pl.BlockSpec((1, LANES, tq), lambda bi, i: (bi, 0, i)),
            pl.BlockSpec((1, lp, IDX_DIM), lambda bi, i: (bi, 0, 0)),
        ],
        out_specs=pl.BlockSpec((1, 1, lp, tq), lambda bi, i: (bi, i, 0, 0)),
        out_shape=jax.ShapeDtypeStruct((b, n_q, lp, tq), jnp.bfloat16),
        scratch_shapes=[
            pltpu.VMEM((lp // SUBLANES, SUBLANES, tq), jnp.int32),
            pltpu.VMEM(((lp + tk) // PACKED_ROWS, PACKED_ROWS, tq), jnp.int16),
            pltpu.VMEM(((lp + tk) // PACKED_ROWS, PACKED_ROWS, tq), jnp.int16),
        ],
        compiler_params=pltpu.CompilerParams(
            dimension_semantics=("parallel", "parallel"), vmem_limit_bytes=VMEM_LIMIT),
        name="dsa_select",
    )(qit, kwt, kib)
    return pl.pallas_call(
        functools.partial(_dsa_attend_kernel, **geom),
        grid=(b, N_HEADS // HEADS_PER_STEP, n_q),
        in_specs=[
            pl.BlockSpec((1, hw, tq), lambda bi, h, i: (bi, h, i)),
            pl.BlockSpec((1, lp, hw), lambda bi, h, i: (bi, 0, h)),
            pl.BlockSpec((1, hw, lp), lambda bi, h, i: (bi, h, 0)),
            pl.BlockSpec((1, 1, lp, tq), lambda bi, h, i: (bi, i, 0, 0)),
            pl.BlockSpec((4, HEADS_PER_STEP, tk, tq), lambda bi, h, i: (0, h, 0, 0)),
        ],
        out_specs=pl.BlockSpec((1, tq, hw), lambda bi, h, i: (bi, i, h)),
        out_shape=jax.ShapeDtypeStruct((b, tqp, HD), jnp.bfloat16),
        scratch_shapes=[
            pltpu.VMEM((ATTN_GROUP * tk, tq), jnp.float32),
            pltpu.VMEM((ATTN_GROUP * tk, tq), jnp.float32),
            pltpu.VMEM((ATTN_GROUP * tk, tq), jnp.bfloat16),
        ],
        compiler_params=pltpu.CompilerParams(
            dimension_semantics=("parallel", "parallel", "parallel"),
            vmem_limit_bytes=VMEM_LIMIT),
        name="dsa_attend",
    )(qt, k, vt, madd, bias)


def _rel_bucket(rel):
    nb = N_BUCKETS // 2
    max_exact = nb // 2
    ret = jnp.where(rel > 0, nb, 0)
    n = jnp.abs(rel)
    nf = jnp.maximum(n, 1).astype(jnp.float32)
    large = max_exact + (jnp.log(nf / max_exact) / math.log(MAX_DISTANCE / max_exact)
                         * (nb - max_exact)).astype(jnp.int32)
    large = jnp.minimum(large, nb - 1)
    return ret + jnp.where(n < max_exact, n, large)


def _bias_tiles(rel_bias, tq, tk):
    w = tk + tq
    tiles = []
    for d in range(-2, 2):
        rel = d * tk + tk - 1 - jnp.arange(w, dtype=jnp.int32)
        vec = rel_bias.astype(jnp.float32)[_rel_bucket(rel)].T
        vec = jnp.roll(vec, -(tk - 1), axis=1)
        skew = jnp.tile(vec, (1, tk))[:, :tk * (w - 1)].reshape(-1, tk, w - 1)
        tiles.append(skew[:, :, :tq])
    tiles = jnp.stack(tiles)
    return tiles - tiles[0, :, :1, :1]


def _attn_b_kernel(q_ref, k_ref, v_ref, tri_ref, o_ref, *, tq, tk, q_off):
    i = pl.program_id(2)
    qpos0 = q_off + i * tq
    heads = [slice(hh * HEAD_DIM, (hh + 1) * HEAD_DIM) for hh in range(SB_HEADS_PER_STEP)]
    tri = tri_ref[...]

    def block(j, diagonal, runs, accs):
        ds = pl.ds(pl.multiple_of(j * tk, tk), tk)
        if diagonal:
            causal = ((j * tk + lax.broadcasted_iota(jnp.int32, (tq, tk), 1))
                      < (qpos0 + lax.broadcasted_iota(jnp.int32, (tq, tk), 0)))
        zs = [lax.dot_general(q_ref[0, :, hs], k_ref[0, ds, hs], _NT,
                              preferred_element_type=jnp.float32) for hs in heads]
        tails, log_1ms, withins = [], [], []
        for z in zs:
            tail = jnp.log(1.0 + jnp.exp(-jnp.abs(z)))
            log_1m = -jnp.maximum(z, 0.0) - tail
            if diagonal:
                log_1m = jnp.where(causal, log_1m, 0.0)
            hi = log_1m.astype(jnp.bfloat16)
            lo = (log_1m - hi.astype(jnp.float32)).astype(jnp.bfloat16)
            withins.append(jnp.dot(hi, tri, preferred_element_type=jnp.float32)
                           + jnp.dot(lo, tri, preferred_element_type=jnp.float32))
            tails.append(tail)
            log_1ms.append(log_1m)
        new_runs, new_accs = [], []
        for hh, hs in enumerate(heads):
            a = jnp.exp(jnp.minimum(zs[hh], 0.0) - tails[hh] + withins[hh] + runs[hh])
            if diagonal:
                a = jnp.where(causal, a, 0.0)
            new_accs.append(accs[hh] + jnp.dot(a.astype(jnp.bfloat16), v_ref[0, ds, hs],
                                               preferred_element_type=jnp.float32))
            new_runs.append(runs[hh] + withins[hh][:, 0:1] + log_1ms[hh][:, 0:1])
        return tuple(new_runs), tuple(new_accs)

    def cond(carry):
        j, runs, _ = carry
        return (j >= 0) & (jnp.max(functools.reduce(jnp.maximum, runs)) > EXP_ZERO_BELOW)

    def body(carry):
        j, runs, accs = carry
        runs, accs = block(j, False, runs, accs)
        return j - 1, runs, accs

    j0 = (qpos0 + tq - 2) // tk
    runs = tuple(jnp.zeros((tq, 1), jnp.float32) for _ in heads)
    accs = tuple(jnp.zeros((tq, HEAD_DIM), jnp.float32) for _ in heads)
    runs, accs = block(j0, True, runs, accs)
    _, _, accs = lax.while_loop(cond, body, (j0 - 1, runs, accs))
    for hs, acc in zip(heads, accs):
        o_ref[0, :, hs] = acc.astype(o_ref.dtype)


def _attn_b(q, k, v, *, tq, tk, q_off):
    b, tqp, _ = q.shape
    lp = k.shape[1]
    tri = (jnp.arange(tk)[:, None] > jnp.arange(tk)[None, :]).astype(jnp.bfloat16)
    assert q_off % tk == 0 and tq <= tk and (tq == tk or tqp == tq)
    hw = SB_HEADS_PER_STEP * HEAD_DIM
    kern = functools.partial(_attn_b_kernel, tq=tq, tk=tk, q_off=q_off)
    return pl.pallas_call(
        kern, grid=(b, N_HEADS // SB_HEADS_PER_STEP, tqp // tq),
        in_specs=[
            pl.BlockSpec((1, tq, hw), lambda bi, h, i: (bi, i, h)),
            pl.BlockSpec((1, lp, hw), lambda bi, h, i: (bi, 0, h)),
            pl.BlockSpec((1, lp, hw), lambda bi, h, i: (bi, 0, h)),
            _const_spec((tk, tk)),
        ],
        out_specs=pl.BlockSpec((1, tq, hw), lambda bi, h, i: (bi, i, h)),
        out_shape=jax.ShapeDtypeStruct((b, tqp, HD), jnp.bfloat16),
        compiler_params=pltpu.CompilerParams(
            dimension_semantics=("parallel", "parallel", "parallel"),
            vmem_limit_bytes=VMEM_LIMIT),
        name="stick_breaking_attention",
    )(q, k, v, tri)


def _post_kernel(x_ref, o_ref, left_ref, wo_ref, g1_ref, b1_ref, wg_ref, wu_ref, cw_ref, cb_ref,
                 wd_ref, g2_ref, b2_ref, y_ref, conv_ref, carry_ref,
                 *, tm, d_ff, alpha, r_last):
    t = pl.program_id(1)

    @pl.when(t == 0)
    def _init():
        carry_ref[...] = left_ref[0]

    mix = jnp.dot(o_ref[0], wo_ref[...], preferred_element_type=jnp.float32)
    x1 = _layer_norm(alpha * x_ref[0] + mix, g1_ref[...], b1_ref[...])
    xb = x1.astype(jnp.bfloat16)
    row = lax.broadcasted_iota(jnp.int32, (tm, FF_CHUNK), 0)
    acc = jnp.zeros((tm, D_MODEL), jnp.float32)
    n_chunks = d_ff // FF_CHUNK

    def gate_up(c):
        cs = slice(c * FF_CHUNK, (c + 1) * FF_CHUNK)
        return (jnp.dot(xb, wg_ref[:, cs], preferred_element_type=jnp.float32),
                jnp.dot(xb, wu_ref[:, cs], preferred_element_type=jnp.float32))

    nxt = gate_up(0)
    for c in range(n_chunks):
        cs = slice(c * FF_CHUNK, (c + 1) * FF_CHUNK)
        g, u = nxt
        if c + 1 < n_chunks:
            nxt = gate_up(c + 1)
        prev = carry_ref[:, cs]
        g1 = jnp.where(row == 0, prev[7:8], pltpu.roll(g, 1, 0))
        g2 = jnp.where(row == 0, prev[6:7], jnp.where(row == 1, prev[7:8], pltpu.roll(g, 2, 0)))
        gc = cb_ref[:, cs] + cw_ref[0:1, cs] * g2
        gc = gc + cw_ref[1:2, cs] * g1
        gc = gc + cw_ref[2:3, cs] * g
        hid = jax.nn.gelu(gc) * u
        acc = acc + jnp.dot(hid.astype(jnp.bfloat16), wd_ref[cs, :],
                            preferred_element_type=jnp.float32)
        carry_ref[:, cs] = g[tm - SUBLANES:tm]
        conv_ref[0, :, cs] = g[r_last:r_last + SUBLANES]

    y_ref[0] = _layer_norm(alpha * x1 + acc, g2_ref[...], b2_ref[...])


def _post(x, o, left8, t_valid, tm, w, alpha):
    b, tp, _ = x.shape
    d_ff = w["wg"].shape[1]
    assert d_ff % FF_CHUNK == 0 and t_valid % SUBLANES == 0 and t_valid >= SUBLANES
    assert (t_valid - 1) // tm == tp // tm - 1
    r_last = (t_valid - SUBLANES) % tm
    row = lambda width: pl.BlockSpec((1, tm, width), lambda i, j: (i, j, 0))
    vec = lambda a: a.reshape(1, -1).astype(jnp.float32)
    consts = [w["wo"], vec(w["g1"]), vec(w["b1"]), w["wg"], w["wu"], w["cw"].astype(jnp.float32),
              vec(w["cb"]), w["wd"], vec(w["g2"]), vec(w["b2"])]
    kern = functools.partial(_post_kernel, tm=tm, d_ff=d_ff, alpha=alpha, r_last=r_last)
    return pl.pallas_call(
        kern, grid=(b, tp // tm),
        in_specs=[row(D_MODEL), row(HD), pl.BlockSpec((1, SUBLANES, d_ff), lambda i, j: (i, 0, 0))]
                 + [_const_spec(c.shape) for c in consts],
        out_specs=[row(D_MODEL), pl.BlockSpec((1, SUBLANES, d_ff), lambda i, j: (i, 0, 0))],
        out_shape=[jax.ShapeDtypeStruct((b, tp, D_MODEL), jnp.float32),
                   jax.ShapeDtypeStruct((b, SUBLANES, d_ff), jnp.float32)],
        scratch_shapes=[pltpu.VMEM((SUBLANES, d_ff), jnp.float32)],
        compiler_params=pltpu.CompilerParams(
            dimension_semantics=("parallel", "arbitrary"), vmem_limit_bytes=VMEM_LIMIT),
        name="outproj_ln_convffn_ln",
    )(x, o, left8, *consts)


def _split_a(w_in):
    bf = jnp.bfloat16
    off_qi = 3 * HD
    off_ki = off_qi + IDX_QW
    wq, wk, wv = w_in[:, :HD], w_in[:, HD:2 * HD], w_in[:, 2 * HD:3 * HD]
    wqi = w_in[:, off_qi:off_ki]
    wkw = jnp.pad(w_in[:, off_ki:], ((0, 0), (0, LANES - IDX_DIM - IDX_HEADS)))
    row_scale = jnp.concatenate([
        jnp.ones((IDX_DIM,), jnp.float32),
        jnp.full((IDX_HEADS,), IDX_QW ** -0.5, jnp.float32),
        jnp.zeros((LANES - IDX_DIM - IDX_HEADS,), jnp.float32)]).reshape(LANES, 1)
    return [wq.T.astype(bf), wk.astype(bf), wv.astype(bf), wv.T.astype(bf), wqi.T.astype(bf),
            wkw.astype(bf), wkw.T.astype(bf), row_scale]


def _split_b(w_in):
    bf = jnp.bfloat16
    return [w_in[:, :HD].astype(bf), w_in[:, HD:2 * HD].astype(bf), w_in[:, 2 * HD:].astype(bf)]


def _with_cache(cache, new_bf, lp):
    b, p = cache.shape[:2]
    flat = cache.reshape(b, p, -1).astype(jnp.bfloat16)
    t = new_bf.shape[1]
    return jnp.pad(jnp.concatenate([flat, new_bf], axis=1), ((0, 0), (0, lp - p - t), (0, 0)))


def kernel(x_prompt, x_sample, cache_a_k, cache_a_v, cache_a_idx_k, cache_b_k, cache_b_v,
           state_ffn_conv, meta_tokens, rel_bias, w_a_in, w_a_out, w_b_in, w_b_out,
           ln1_g, ln1_b, ln2_g, ln2_b, w_ffn_gate, w_ffn_up, ffn_conv_w, ffn_conv_b, w_ffn_down):
    bp, seq, _ = x_prompt.shape
    bs, ts, _ = x_sample.shape
    depth = ln1_g.shape[0]
    d_ff = w_ffn_gate.shape[2]
    past = cache_a_k.shape[2]
    alpha = (2 * depth) ** 0.25
    t_p = N_META + seq
    tp_pad = _round_up(t_p, ROW_TILE)
    ls = past + ts
    ls_pad = _round_up(ls, ATTN_GROUP * KEY_TILE)
    assert past % KEY_TILE == 0 and ts % SUBLANES == 0 and ts <= SAMPLE_TILE
    topk_p = min(TOPK_MAX, seq // 4)
    topk_s = min(TOPK_MAX, ls // 4)
    bf = jnp.bfloat16

    meta = jnp.broadcast_to(meta_tokens.astype(x_prompt.dtype)[None], (bp, N_META, D_MODEL))
    xp = jnp.pad(jnp.concatenate([meta, x_prompt], axis=1), ((0, 0), (0, tp_pad - t_p), (0, 0)))
    xs = x_sample
    bias_p = _bias_tiles(rel_bias, ROW_TILE, KEY_TILE)
    bias_s = _bias_tiles(rel_bias, SAMPLE_TILE, KEY_TILE)
    left_p = jnp.zeros((bp, SUBLANES, d_ff), jnp.float32)
    wg_all, wu_all, wd_all = w_ffn_gate.astype(bf), w_ffn_up.astype(bf), w_ffn_down.astype(bf)
    wo_a, wo_b = w_a_out.astype(bf), w_b_out.astype(bf)

    heads = lambda a: a.reshape(a.shape[0], a.shape[1], N_HEADS, HEAD_DIM)
    outs = {k: [] for k in ("akp", "avp", "aip", "aks", "avs", "ais",
                            "bkp", "bvp", "bks", "bvs", "cp", "cs")}
    for i in range(depth):
        j = i // N_MIXERS
        if i % N_MIXERS == 0:
            wts = _split_a(w_a_in[j])
            qt, kc, kb, vc, vt, qit, kic, kib, kwt = _project(
                xp, t_p, ROW_TILE, wts, True, key_rows=_round_up(tp_pad, ATTN_GROUP * KEY_TILE))
            op = _attn_a(qt, qit, kwt, kb, vt, kib, bias_p,
                         tq=ROW_TILE, tk=KEY_TILE, q_off=0, n_lead=N_META, l_valid=t_p, topk=topk_p)
            outs["akp"].append(heads(kc)); outs["avp"].append(heads(vc)); outs["aip"].append(kic)
            xs_pad = jnp.pad(xs, ((0, 0), (0, SAMPLE_TILE - ts), (0, 0)))
            qt, kc, kb, vc, vt, qit, kic, kib, kwt = _project(
                xs_pad, ts, SAMPLE_TILE, wts, True, key_rows=SAMPLE_TILE)
            cache_vt = jnp.swapaxes(cache_a_v[j].reshape(bs, past, HD).astype(bf), 1, 2)
            vt_all = jnp.pad(jnp.concatenate([cache_vt, vt[:, :, :ts]], axis=2),
                             ((0, 0), (0, 0), (0, ls_pad - ls)))
            os_ = _attn_a(qt, qit, kwt, _with_cache(cache_a_k[j], kb[:, :ts], ls_pad), vt_all,
                          _with_cache(cache_a_idx_k[j], kib[:, :ts], ls_pad), bias_s,
                          tq=SAMPLE_TILE, tk=KEY_TILE, q_off=past, n_lead=0, l_valid=ls,
                          topk=topk_s)[:, :ts]
            outs["aks"].append(heads(kc)); outs["avs"].append(heads(vc)); outs["ais"].append(kic)
            w_out = wo_a[j]
        else:
            wts = _split_b(w_b_in[j])
            q, kc, kb, vc, vb = _project(xp, t_p, ROW_TILE, wts, False)
            op = _attn_b(q, kb, vb, tq=ROW_TILE, tk=KEY_TILE, q_off=0)
            outs["bkp"].append(heads(kc)); outs["bvp"].append(heads(vc))
            q, kc, kb, vc, vb = _project(xs, ts, ts, wts, False)
            os_ = _attn_b(q, _with_cache(cache_b_k[j], kb, ls_pad),
                          _with_cache(cache_b_v[j], vb, ls_pad), tq=ts, tk=KEY_TILE, q_off=past)
            outs["bks"].append(heads(kc)); outs["bvs"].append(heads(vc))
            w_out = wo_b[j]
        w = dict(wo=w_out, g1=ln1_g[i], b1=ln1_b[i], wg=wg_all[i], wu=wu_all[i], cw=ffn_conv_w[i],
                 cb=ffn_conv_b[i], wd=wd_all[i], g2=ln2_g[i], b2=ln2_b[i])
        xp, conv_p = _post(xp, op, left_p, t_p, ROW_TILE, w, alpha)
        left_s = jnp.pad(state_ffn_conv[i].astype(jnp.float32),
                         ((0, 0), (SUBLANES - (CONV_W - 1), 0), (0, 0)))
        xs, conv_s = _post(xs, os_, left_s, ts, ts, w, alpha)
        outs["cp"].append(conv_p[:, SUBLANES - (CONV_W - 1):])
        outs["cs"].append(conv_s[:, SUBLANES - (CONV_W - 1):])

    st = jnp.stack
    return (xp[:, N_META:t_p], xs,
            st(outs["akp"]), st(outs["avp"]), st(outs["aip"]),
            st(outs["bkp"]), st(outs["bvp"]), st(outs["cp"]),
            st(outs["aks"]), st(outs["avs"]), st(outs["ais"]),
            st(outs["bks"]), st(outs["bvs"]), st(outs["cs"]))
```

```python
import functools
import math

import jax
import jax.numpy as jnp
from jax import lax
from jax.experimental import pallas as pl
from jax.experimental.pallas import tpu as pltpu

D_MODEL = 1024
N_HEADS = 8
HEAD_DIM = 128
HD = N_HEADS * HEAD_DIM
IDX_HEADS = 8
IDX_DIM = 64
IDX_QW = IDX_HEADS * IDX_DIM
CHUNK = 64
CHUNK_SHIFT = 6
N_META = 16
N_MIXERS = 2
TOPK_MAX = 256
N_BUCKETS = 32
MAX_DISTANCE = 128
CONV_W = 3
LN_EPS = 1e-5

LANES = 128
SUBLANES = 8
ROW_TILE = 256
KEY_TILE = 256
FF_CHUNK = 256
ATTN_GROUP = 4
HEADS_PER_STEP = 2
SB_HEADS_PER_STEP = 4
COUNT_GROUP = 2
SAMPLE_TILE = 128
VMEM_LIMIT = 56 * 1024 * 1024

NEG_BIG = -1e30
INT_MIN = -2 ** 31
KEY_NEG_INF = -2139095041
HALF_BITS = 16
HALF_BIAS = 1 << (HALF_BITS - 1)
HALF_MASK = (1 << HALF_BITS) - 1
PACKED_ROWS = 16
EXP_ZERO_BELOW = -104.0

_NT = (((1,), (1,)), ((), ()))


def _round_up(n, m):
    return -(-n // m) * m


def _layer_norm(x, g, b):
    mu = jnp.mean(x, -1, keepdims=True)
    xc = x - mu
    var = jnp.mean(xc * xc, -1, keepdims=True)
    return xc * lax.rsqrt(var + LN_EPS) * g + b


def _proj_a_kernel(x_ref, wqt_ref, wk_ref, wv_ref, wvt_ref, wqit_ref, wkw_ref, wkwt_ref, rs_ref,
                   qt_ref, kc_ref, kb_ref, vc_ref, vt_ref, qit_ref, kic_ref, kib_ref, kwt_ref,
                   *, n_tiles):
    live = pl.program_id(1) < n_tiles
    xb = x_ref[0].astype(jnp.bfloat16)

    def cols(wt_ref):
        return lax.dot_general(wt_ref[...], xb, _NT, preferred_element_type=jnp.float32)

    qt_ref[0] = (cols(wqt_ref) * (HEAD_DIM ** -0.5)).astype(jnp.bfloat16)
    k = jnp.dot(xb, wk_ref[...], preferred_element_type=jnp.float32)
    kc_ref[0] = k
    kb_ref[0] = jnp.where(live, k, 0.0).astype(jnp.bfloat16)
    vc_ref[0] = jnp.dot(xb, wv_ref[...], preferred_element_type=jnp.float32)
    vt_ref[0] = jnp.where(live, cols(wvt_ref), 0.0).astype(jnp.bfloat16)
    qit_ref[0] = cols(wqit_ref).astype(jnp.bfloat16)
    kw = jnp.dot(xb, wkw_ref[...], preferred_element_type=jnp.float32)
    kic_ref[0] = kw[:, :IDX_DIM]
    kib_ref[0] = jnp.where(live, kw[:, :IDX_DIM], 0.0).astype(jnp.bfloat16)
    kwt_ref[0] = cols(wkwt_ref) * rs_ref[...]


def _proj_b_kernel(x_ref, wq_ref, wk_ref, wv_ref, q_ref, kc_ref, kb_ref, vc_ref, vb_ref):
    xb = x_ref[0].astype(jnp.bfloat16)
    q = jnp.dot(xb, wq_ref[...], preferred_element_type=jnp.float32)
    q_ref[0] = (q * (HEAD_DIM ** -0.5)).astype(jnp.bfloat16)
    k = jnp.dot(xb, wk_ref[...], preferred_element_type=jnp.float32)
    kc_ref[0] = k
    kb_ref[0] = k.astype(jnp.bfloat16)
    v = jnp.dot(xb, wv_ref[...], preferred_element_type=jnp.float32)
    vc_ref[0] = v
    vb_ref[0] = v.astype(jnp.bfloat16)


def _const_spec(shape):
    return pl.BlockSpec(shape, lambda *_: (0,) * len(shape))


def _project(x, t_valid, tm, weights, is_a, key_rows=None):
    b, tp, _ = x.shape
    n_tiles = tp // tm
    grid = (b, n_tiles)
    row = lambda w: pl.BlockSpec((1, tm, w), lambda i, j: (i, j, 0))
    bf = jnp.bfloat16
    f32 = jnp.float32
    pad = lambda w, dt: jax.ShapeDtypeStruct((b, tp, w), dt)
    exact = lambda w, dt: jax.ShapeDtypeStruct((b, t_valid, w), dt)
    in_specs = [row(D_MODEL)] + [_const_spec(w.shape) for w in weights]
    if is_a:
        assert key_rows % tm == 0 and key_rows >= tp
        grid = (b, key_rows // tm)
        last = n_tiles - 1
        row = lambda w: pl.BlockSpec((1, tm, w), lambda i, j: (i, jnp.minimum(j, last), 0))
        col = lambda w: pl.BlockSpec((1, w, tm), lambda i, j: (i, 0, jnp.minimum(j, last)))
        key_row = lambda w: pl.BlockSpec((1, tm, w), lambda i, j: (i, j, 0))
        key_col = lambda w: pl.BlockSpec((1, w, tm), lambda i, j: (i, 0, j))
        padt = lambda w, dt: jax.ShapeDtypeStruct((b, w, tp), dt)
        out_shape = [padt(HD, bf), exact(HD, f32), jax.ShapeDtypeStruct((b, key_rows, HD), bf),
                     exact(HD, f32), jax.ShapeDtypeStruct((b, HD, key_rows), bf),
                     padt(IDX_QW, bf), exact(IDX_DIM, f32),
                     jax.ShapeDtypeStruct((b, key_rows, IDX_DIM), bf), padt(LANES, f32)]
        out_specs = [col(HD), row(HD), key_row(HD), row(HD), key_col(HD),
                     col(IDX_QW), row(IDX_DIM), key_row(IDX_DIM), col(LANES)]
        in_specs[0] = row(D_MODEL)
        body = functools.partial(_proj_a_kernel, n_tiles=n_tiles)
    else:
        out_shape = [pad(HD, bf), exact(HD, f32), pad(HD, bf), exact(HD, f32), pad(HD, bf)]
        out_specs = [row(HD)] * 5
        body = _proj_b_kernel
    return pl.pallas_call(
        body, grid=grid, in_specs=in_specs, out_specs=out_specs, out_shape=out_shape,
        compiler_params=pltpu.CompilerParams(
            dimension_semantics=("parallel", "arbitrary"), vmem_limit_bytes=VMEM_LIMIT),
        name="proj_a" if is_a else "proj_b",
    )(x, *weights)


def _vis_end(qpos, n_lead, l_valid):
    ve = jnp.where(qpos < n_lead, n_lead,
                   n_lead + ((((qpos - n_lead) >> CHUNK_SHIFT) + 1) << CHUNK_SHIFT))
    return jnp.minimum(ve, l_valid)


def _key_blocks(i, tq, tk, q_off, n_lead, l_valid):
    return (_vis_end(q_off + i * tq + tq - 1, n_lead, l_valid) + tk - 1) // tk


def _dsa_select_kernel(qit_ref, kwt_ref, kib_ref, madd_ref,
                       keys_ref, hi_ref, lo_ref, *, tq, tk, q_off, n_lead, l_valid, topk):
    i = pl.program_id(1)
    qpos0 = q_off + i * tq
    nkb = _key_blocks(i, tq, tk, q_off, n_lead, l_valid)
    g = tk // SUBLANES

    def blk_rows(j):
        return pl.ds(pl.multiple_of(j * g, g), g)

    def blk_keys(j):
        return pl.ds(pl.multiple_of(j * tk, tk), tk)

    g16 = tk // PACKED_ROWS

    def blk_halves(j):
        return pl.ds(pl.multiple_of(j * g16, g16), g16)

    qpos = qpos0 + lax.broadcasted_iota(jnp.int32, (1, tq), 1)
    vis_end = _vis_end(qpos, n_lead, l_valid)

    def score_pair(jj, carry):
        n = 2 * tk
        kib = kib_ref[0, pl.ds(pl.multiple_of(jj * n, n), n), :]
        sc = jnp.zeros((n, tq), jnp.float32)
        for ih in range(IDX_HEADS):
            s = jnp.dot(kib, qit_ref[0, ih * IDX_DIM:(ih + 1) * IDX_DIM, :],
                        preferred_element_type=jnp.float32)
            sc = sc + jnp.maximum(s, 0.0) * kwt_ref[0, IDX_DIM + ih:IDX_DIM + ih + 1, :]
        sc = jnp.where(sc == 0.0, 0.0, sc)
        bits = pltpu.bitcast(sc, jnp.int32)
        key = jnp.where(bits < 0, bits ^ 0x7FFFFFFF, bits)
        pos = jj * n + lax.broadcasted_iota(jnp.int32, (n, tq), 0)
        key = jnp.where(pos < vis_end, key, KEY_NEG_INF)
        keys_ref[pl.ds(pl.multiple_of(jj * (2 * g), 2 * g), 2 * g)] = key.reshape(2 * g, SUBLANES, tq)
        hi_ref[pl.ds(pl.multiple_of(jj * (2 * g16), 2 * g16), 2 * g16)] = (
            (key >> HALF_BITS).astype(jnp.int16).reshape(2 * g16, PACKED_ROWS, tq))
        return carry

    lax.fori_loop(0, (nkb + 1) // 2, score_pair, 0)

    def count_above_and_at(level):
        def total(hit):
            part = jnp.sum(hit.astype(jnp.int32).reshape(4, g // 4, SUBLANES, tq), axis=1)
            return jnp.sum(part, axis=0)

        def blk(j, cnts):
            kk = keys_ref[blk_rows(j)]
            return cnts[0] + total(kk > level), cnts[1] + total(kk >= level)
        zero = jnp.zeros((SUBLANES, tq), jnp.int32)
        cnts = lax.fori_loop(0, nkb, blk, (zero, zero))
        return tuple(jnp.broadcast_to(jnp.sum(c, axis=0, keepdims=True), (SUBLANES, tq))
                     for c in cnts)

    def count_half(half_ref, cand):
        cand16 = jnp.broadcast_to(cand, (PACKED_ROWS, tq)).astype(jnp.int16)

        def group(jj, cnt):
            n = COUNT_GROUP * g16
            hit = jnp.where(half_ref[pl.ds(pl.multiple_of(jj * n, n), n)] >= cand16,
                            jnp.bfloat16(1), jnp.bfloat16(0))
            parts = [hit[a] for a in range(n)]
            while len(parts) > 1:
                parts = [parts[a] + parts[a + 1] for a in range(0, len(parts), 2)]
            return cnt + parts[0].astype(jnp.float32)
        cnt = lax.fori_loop(0, (nkb + COUNT_GROUP - 1) // COUNT_GROUP, group,
                            jnp.zeros((PACKED_ROWS, tq), jnp.float32))
        return jnp.sum(cnt, axis=0, keepdims=True).astype(jnp.int32)

    def fill_pair_partner(half_ref):
        for extra in range(COUNT_GROUP - 1):
            half_ref[blk_halves(nkb + extra)] = jnp.full((g16, PACKED_ROWS, tq), -HALF_BIAS, jnp.int16)

    def kth_largest_half(half_ref, k):
        def step(b, ut):
            cand = ut | (jnp.int32(1) << (HALF_BITS - 1 - b))
            cnt = count_half(half_ref, cand - HALF_BIAS)
            return jnp.where(cnt >= k, cand, ut)
        return lax.fori_loop(0, HALF_BITS, step, jnp.zeros((1, tq), jnp.int32)) - HALF_BIAS

    fill_pair_partner(hi_ref)
    thr_hi = kth_largest_half(hi_ref, topk)
    above = jnp.where(thr_hi == HALF_BIAS - 1, 0,
                      count_half(hi_ref, jnp.minimum(thr_hi + 1, HALF_BIAS - 1)))

    def low_halves(j, carry):
        kk = keys_ref[blk_rows(j)].reshape(tk, tq)
        lo = jnp.where((kk >> HALF_BITS) == thr_hi, (kk & HALF_MASK) - HALF_BIAS, -HALF_BIAS)
        lo_ref[blk_halves(j)] = lo.astype(jnp.int16).reshape(g16, PACKED_ROWS, tq)
        return carry

    lax.fori_loop(0, nkb, low_halves, 0)
    fill_pair_partner(lo_ref)
    thr_lo = kth_largest_half(lo_ref, topk - above)
    thr = jnp.broadcast_to((thr_hi << HALF_BITS) | (thr_lo + HALF_BIAS), (SUBLANES, tq))
    cnt_gt, cnt_ge = count_above_and_at(thr)
    excess = (cnt_ge > topk) & (thr != KEY_NEG_INF)
    any_excess = jnp.max(excess.astype(jnp.int32))
    thr_row = thr[0:1]
    may_keep = (topk - cnt_gt[0:1]).astype(jnp.float32)

    def store_mask(j, sel):
        sel = sel & (keys_ref[blk_rows(j)].reshape(tk, tq) != KEY_NEG_INF)
        madd_ref[0, 0, blk_keys(j), :] = jnp.where(sel, 0.0, NEG_BIG).astype(madd_ref.dtype)

    @pl.when(any_excess > 0)
    def _mask_tied():
        earlier = (lax.broadcasted_iota(jnp.int32, (tk, tk), 1)
                   < lax.broadcasted_iota(jnp.int32, (tk, tk), 0)).astype(jnp.bfloat16)

        def block(j, seen):
            kk = keys_ref[blk_rows(j)].reshape(tk, tq)
            tie = jnp.where(kk == thr_row, 1.0, 0.0)
            before = seen + jnp.dot(earlier, tie.astype(jnp.bfloat16),
                                    preferred_element_type=jnp.float32)
            store_mask(j, (kk > thr_row) | ((kk == thr_row) & (before < may_keep)))
            return before[tk - 1:tk] + tie[tk - 1:tk]

        lax.fori_loop(0, nkb, block, jnp.zeros((1, tq), jnp.float32))

    @pl.when(any_excess == 0)
    def _mask_plain():
        def block(j, carry):
            store_mask(j, keys_ref[blk_rows(j)].reshape(tk, tq) >= thr_row)
            return carry

        lax.fori_loop(0, nkb, block, 0)

    def mask_tail(j, carry):
        madd_ref[0, 0, blk_keys(j), :] = jnp.full((tk, tq), NEG_BIG, madd_ref.dtype)
        return carry

    lax.fori_loop(nkb, madd_ref.shape[2] // tk, mask_tail, 0)


def _dsa_attend_kernel(qt_ref, k_ref, vt_ref, madd_ref, bias_ref, o_ref, s0_ref, s1_ref, p1_ref,
                       *, tq, tk, q_off, n_lead, l_valid):
    i = pl.program_id(2)
    qpos0 = q_off + i * tq
    nkb = _key_blocks(i, tq, tk, q_off, n_lead, l_valid)
    wide = ATTN_GROUP * tk
    n_trips = (nkb + ATTN_GROUP - 1) // ATTN_GROUP
    jq = qpos0 // tk
    s_refs = (s0_ref, s1_ref)
    heads = [slice(hh * HEAD_DIM, (hh + 1) * HEAD_DIM) for hh in range(HEADS_PER_STEP)]

    def keys_of(t):
        return pl.ds(pl.multiple_of(t * wide, wide), wide)

    def scores(hh, t):
        s_refs[hh][...] = jnp.dot(k_ref[0, keys_of(t), heads[hh]], qt_ref[0, heads[hh], :],
                                  preferred_element_type=jnp.float32)

    def softmax(hh, t, m, near):
        parts = []
        for u in range(ATTN_GROUP):
            j = t * ATTN_GROUP + u
            madd = madd_ref[0, 0, pl.ds(pl.multiple_of(j * tk, tk), tk), :]
            part = s_refs[hh][u * tk:(u + 1) * tk] + madd.astype(jnp.float32)
            if near:
                part = part + bias_ref[jnp.clip(j - jq, -2, 1) + 2, hh]
            parts.append(part)
        m_new = m
        for part in parts:
            m_new = jnp.maximum(m_new, jnp.max(part, axis=0, keepdims=True))
        alpha = jnp.exp(m - m_new)
        pb = jnp.concatenate([jnp.exp(part - m_new).astype(jnp.bfloat16) for part in parts], axis=0)
        return m_new, alpha, pb

    ones = jnp.ones((SUBLANES, wide), jnp.bfloat16)

    def weighted_values(hh, t, pb):
        num = jnp.dot(vt_ref[0, heads[hh], keys_of(t)], pb, preferred_element_type=jnp.float32)
        den = jnp.dot(ones, pb, preferred_element_type=jnp.float32)
        return jnp.concatenate([num, den], axis=0)

    def trip(t, carries, near, last=False):
        (m0, acc0), (m1, acc1) = carries
        acc1 = acc1 + weighted_values(1, jnp.maximum(t - 1, 0), p1_ref[...])
        scores(1, t)
        m0, alpha0, pb0 = softmax(0, t, m0, near)
        acc0 = alpha0 * acc0 + weighted_values(0, t, pb0)
        if not last:
            scores(0, t + 1)
        m1, alpha1, pb1 = softmax(1, t, m1, near)
        if last:
            return (m0, acc0), (m1, alpha1 * acc1 + weighted_values(1, t, pb1))
        p1_ref[...] = pb1
        return (m0, acc0), (m1, alpha1 * acc1)

    scores(0, 0)
    p1_ref[...] = jnp.zeros(p1_ref.shape, p1_ref.dtype)
    carries = tuple((jnp.full((1, tq), NEG_BIG, jnp.float32),
                     jnp.zeros((HEAD_DIM + SUBLANES, tq), jnp.float32)) for _ in range(2))
    n_far = jnp.clip((jq - 1) // ATTN_GROUP, 0, n_trips - 1)
    carries = lax.fori_loop(0, n_far, functools.partial(trip, near=False), carries)
    carries = lax.fori_loop(n_far, n_trips - 1, functools.partial(trip, near=True), carries)
    (_, acc0), (_, acc1) = trip(n_trips - 1, carries, near=True, last=True)
    for hs, acc in zip(heads, (acc0, acc1)):
        o_ref[0, :, hs] = (acc[:HEAD_DIM] / acc[HEAD_DIM:HEAD_DIM + 1]).T.astype(o_ref.dtype)


def _attn_a(qt, qit, kwt, k, vt, kib, bias, *, tq, tk, q_off, n_lead, l_valid, topk):
    b, _, tqp = qt.shape
    lp = k.shape[1]
    assert lp % (ATTN_GROUP * tk) == 0 and tk % (4 * SUBLANES) == 0 and tk >= topk
    assert tqp % tq == 0 and q_off % tk == 0 and HEADS_PER_STEP == 2
    n_q = tqp // tq
    hw = HEADS_PER_STEP * HEAD_DIM
    geom = dict(tq=tq, tk=tk, q_off=q_off, n_lead=n_lead, l_valid=l_valid)
    madd = pl.pallas_call(
        functools.partial(_dsa_select_kernel, topk=topk, **geom),
        grid=(b, n_q),
        in_specs=[
            pl.BlockSpec((1, IDX_QW, tq), lambda bi, i: (bi, 0, i)),
            pl.BlockSpec((1, LANES, tq), lambda bi, i: (bi, 0, i)),
            pl.BlockSpec((1, lp, IDX_DIM), lambda bi, i: (bi, 0, 0)),
        ],
        out_specs=pl.BlockSpec((1, 1, lp, tq), lambda bi, i: (bi, i, 0, 0)),
        out_shape=jax.ShapeDtypeStruct((b, n_q, lp, tq), jnp.bfloat16),
        scratch_shapes=[
            pltpu.VMEM((lp // SUBLANES, SUBLANES, tq), jnp.int32),
            pltpu.VMEM(((lp + COUNT_GROUP * tk) // PACKED_ROWS, PACKED_ROWS, tq), jnp.int16),
            pltpu.VMEM(((lp + COUNT_GROUP * tk) // PACKED_ROWS, PACKED_ROWS, tq), jnp.int16),
        ],
        compiler_params=pltpu.CompilerParams(
            dimension_semantics=("parallel", "parallel"), vmem_limit_bytes=VMEM_LIMIT),
        name="dsa_select",
    )(qit, kwt, kib)
    return pl.pallas_call(
        functools.partial(_dsa_attend_kernel, **geom),
        grid=(b, N_HEADS // HEADS_PER_STEP, n_q),
        in_specs=[
            pl.BlockSpec((1, hw, tq), lambda bi, h, i: (bi, h, i)),
            pl.BlockSpec((1, lp, hw), lambda bi, h, i: (bi, 0, h)),
            pl.BlockSpec((1, hw, lp), lambda bi, h, i: (bi, h, 0)),
            pl.BlockSpec((1, 1, lp, tq), lambda bi, h, i: (bi, i, 0, 0)),
            pl.BlockSpec((4, HEADS_PER_STEP, tk, tq), lambda bi, h, i: (0, h, 0, 0)),
        ],
        out_specs=pl.BlockSpec((1, tq, hw), lambda bi, h, i: (bi, i, h)),
        out_shape=jax.ShapeDtypeStruct((b, tqp, HD), jnp.bfloat16),
        scratch_shapes=[
            pltpu.VMEM((ATTN_GROUP * tk, tq), jnp.float32),
            pltpu.VMEM((ATTN_GROUP * tk, tq), jnp.float32),
            pltpu.VMEM((ATTN_GROUP * tk, tq), jnp.bfloat16),
        ],
        compiler_params=pltpu.CompilerParams(
            dimension_semantics=("parallel", "parallel", "parallel"),
            vmem_limit_bytes=VMEM_LIMIT),
        name="dsa_attend",
    )(qt, k, vt, madd, bias)


def _rel_bucket(rel):
    nb = N_BUCKETS // 2
    max_exact = nb // 2
    ret = jnp.where(rel > 0, nb, 0)
    n = jnp.abs(rel)
    nf = jnp.maximum(n, 1).astype(jnp.float32)
    large = max_exact + (jnp.log(nf / max_exact) / math.log(MAX_DISTANCE / max_exact)
                         * (nb - max_exact)).astype(jnp.int32)
    large = jnp.minimum(large, nb - 1)
    return ret + jnp.where(n < max_exact, n, large)


def _bias_tiles(rel_bias, tq, tk):
    w = tk + tq
    tiles = []
    for d in range(-2, 2):
        rel = d * tk + tk - 1 - jnp.arange(w, dtype=jnp.int32)
        vec = rel_bias.astype(jnp.float32)[_rel_bucket(rel)].T
        vec = jnp.roll(vec, -(tk - 1), axis=1)
        skew = jnp.tile(vec, (1, tk))[:, :tk * (w - 1)].reshape(-1, tk, w - 1)
        tiles.append(skew[:, :, :tq])
    tiles = jnp.stack(tiles)
    return tiles - tiles[0, :, :1, :1]


def _attn_b_kernel(q_ref, k_ref, v_ref, tri_ref, o_ref, *, tq, tk, q_off):
    i = pl.program_id(2)
    qpos0 = q_off + i * tq
    heads = [slice(hh * HEAD_DIM, (hh + 1) * HEAD_DIM) for hh in range(SB_HEADS_PER_STEP)]
    tri = tri_ref[...]

    def block(j, diagonal, runs, accs):
        ds = pl.ds(pl.multiple_of(j * tk, tk), tk)
        if diagonal:
            causal = ((j * tk + lax.broadcasted_iota(jnp.int32, (tq, tk), 1))
                      < (qpos0 + lax.broadcasted_iota(jnp.int32, (tq, tk), 0)))
        zs = [lax.dot_general(q_ref[0, :, hs], k_ref[0, ds, hs], _NT,
                              preferred_element_type=jnp.float32) for hs in heads]
        tails, log_1ms, withins = [], [], []
        for z in zs:
            tail = jnp.log(1.0 + jnp.exp(-jnp.abs(z)))
            log_1m = -jnp.maximum(z, 0.0) - tail
            if diagonal:
                log_1m = jnp.where(causal, log_1m, 0.0)
            hi = log_1m.astype(jnp.bfloat16)
            lo = (log_1m - hi.astype(jnp.float32)).astype(jnp.bfloat16)
            withins.append(jnp.dot(hi, tri, preferred_element_type=jnp.float32)
                           + jnp.dot(lo, tri, preferred_element_type=jnp.float32))
            tails.append(tail)
            log_1ms.append(log_1m)
        new_runs, new_accs = [], []
        for hh, hs in enumerate(heads):
            a = jnp.exp(jnp.minimum(zs[hh], 0.0) - tails[hh] + withins[hh] + runs[hh])
            if diagonal:
                a = jnp.where(causal, a, 0.0)
            new_accs.append(accs[hh] + jnp.dot(a.astype(jnp.bfloat16), v_ref[0, ds, hs],
                                               preferred_element_type=jnp.float32))
            new_runs.append(runs[hh] + withins[hh][:, 0:1] + log_1ms[hh][:, 0:1])
        return tuple(new_runs), tuple(new_accs)

    def cond(carry):
        j, runs, _ = carry
        return (j >= 0) & (jnp.max(functools.reduce(jnp.maximum, runs)) > EXP_ZERO_BELOW)

    def body(carry):
        j, runs, accs = carry
        runs, accs = block(j, False, runs, accs)
        return j - 1, runs, accs

    j0 = (qpos0 + tq - 2) // tk
    runs = tuple(jnp.zeros((tq, 1), jnp.float32) for _ in heads)
    accs = tuple(jnp.zeros((tq, HEAD_DIM), jnp.float32) for _ in heads)
    runs, accs = block(j0, True, runs, accs)
    _, _, accs = lax.while_loop(cond, body, (j0 - 1, runs, accs))
    for hs, acc in zip(heads, accs):
        o_ref[0, :, hs] = acc.astype(o_ref.dtype)


def _attn_b(q, k, v, *, tq, tk, q_off):
    b, tqp, _ = q.shape
    lp = k.shape[1]
    tri = (jnp.arange(tk)[:, None] > jnp.arange(tk)[None, :]).astype(jnp.bfloat16)
    assert q_off % tk == 0 and tq <= tk and (tq == tk or tqp == tq)
    hw = SB_HEADS_PER_STEP * HEAD_DIM
    kern = functools.partial(_attn_b_kernel, tq=tq, tk=tk, q_off=q_off)
    return pl.pallas_call(
        kern, grid=(b, N_HEADS // SB_HEADS_PER_STEP, tqp // tq),
        in_specs=[
            pl.BlockSpec((1, tq, hw), lambda bi, h, i: (bi, i, h)),
            pl.BlockSpec((1, lp, hw), lambda bi, h, i: (bi, 0, h)),
            pl.BlockSpec((1, lp, hw), lambda bi, h, i: (bi, 0, h)),
            _const_spec((tk, tk)),
        ],
        out_specs=pl.BlockSpec((1, tq, hw), lambda bi, h, i: (bi, i, h)),
        out_shape=jax.ShapeDtypeStruct((b, tqp, HD), jnp.bfloat16),
        compiler_params=pltpu.CompilerParams(
            dimension_semantics=("parallel", "parallel", "parallel"),
            vmem_limit_bytes=VMEM_LIMIT),
        name="stick_breaking_attention",
    )(q, k, v, tri)


def _post_kernel(x_ref, o_ref, left_ref, wo_ref, g1_ref, b1_ref, wg_ref, wu_ref, cw_ref, cb_ref,
                 wd_ref, g2_ref, b2_ref, y_ref, conv_ref, carry_ref,
                 *, tm, d_ff, alpha, r_last):
    t = pl.program_id(1)

    @pl.when(t == 0)
    def _init():
        carry_ref[...] = left_ref[0]

    mix = jnp.dot(o_ref[0], wo_ref[...], preferred_element_type=jnp.float32)
    x1 = _layer_norm(alpha * x_ref[0] + mix, g1_ref[...], b1_ref[...])
    xb = x1.astype(jnp.bfloat16)
    row = lax.broadcasted_iota(jnp.int32, (tm, FF_CHUNK), 0)
    acc = jnp.zeros((tm, D_MODEL), jnp.float32)
    n_chunks = d_ff // FF_CHUNK

    def gate_up(c):
        cs = slice(c * FF_CHUNK, (c + 1) * FF_CHUNK)
        return (jnp.dot(xb, wg_ref[:, cs], preferred_element_type=jnp.float32),
                jnp.dot(xb, wu_ref[:, cs], preferred_element_type=jnp.float32))

    nxt = gate_up(0)
    for c in range(n_chunks):
        cs = slice(c * FF_CHUNK, (c + 1) * FF_CHUNK)
        g, u = nxt
        if c + 1 < n_chunks:
            nxt = gate_up(c + 1)
        prev = carry_ref[:, cs]
        g1 = jnp.where(row == 0, prev[7:8], pltpu.roll(g, 1, 0))
        g2 = jnp.where(row == 0, prev[6:7], jnp.where(row == 1, prev[7:8], pltpu.roll(g, 2, 0)))
        gc = cb_ref[:, cs] + cw_ref[0:1, cs] * g2
        gc = gc + cw_ref[1:2, cs] * g1
        gc = gc + cw_ref[2:3, cs] * g
        hid = jax.nn.gelu(gc) * u
        acc = acc + jnp.dot(hid.astype(jnp.bfloat16), wd_ref[cs, :],
                            preferred_element_type=jnp.float32)
        carry_ref[:, cs] = g[tm - SUBLANES:tm]
        conv_ref[0, :, cs] = g[r_last:r_last + SUBLANES]

    y_ref[0] = _layer_norm(alpha * x1 + acc, g2_ref[...], b2_ref[...])


def _post(x, o, left8, t_valid, tm, w, alpha):
    b, tp, _ = x.shape
    d_ff = w["wg"].shape[1]
    assert d_ff % FF_CHUNK == 0 and t_valid % SUBLANES == 0 and t_valid >= SUBLANES
    assert (t_valid - 1) // tm == tp // tm - 1
    r_last = (t_valid - SUBLANES) % tm
    row = lambda width: pl.BlockSpec((1, tm, width), lambda i, j: (i, j, 0))
    vec = lambda a: a.reshape(1, -1).astype(jnp.float32)
    consts = [w["wo"], vec(w["g1"]), vec(w["b1"]), w["wg"], w["wu"], w["cw"].astype(jnp.float32),
              vec(w["cb"]), w["wd"], vec(w["g2"]), vec(w["b2"])]
    kern = functools.partial(_post_kernel, tm=tm, d_ff=d_ff, alpha=alpha, r_last=r_last)
    return pl.pallas_call(
        kern, grid=(b, tp // tm),
        in_specs=[row(D_MODEL), row(HD), pl.BlockSpec((1, SUBLANES, d_ff), lambda i, j: (i, 0, 0))]
                 + [_const_spec(c.shape) for c in consts],
        out_specs=[row(D_MODEL), pl.BlockSpec((1, SUBLANES, d_ff), lambda i, j: (i, 0, 0))],
        out_shape=[jax.ShapeDtypeStruct((b, tp, D_MODEL), jnp.float32),
                   jax.ShapeDtypeStruct((b, SUBLANES, d_ff), jnp.float32)],
        scratch_shapes=[pltpu.VMEM((SUBLANES, d_ff), jnp.float32)],
        compiler_params=pltpu.CompilerParams(
            dimension_semantics=("parallel", "arbitrary"), vmem_limit_bytes=VMEM_LIMIT),
        name="outproj_ln_convffn_ln",
    )(x, o, left8, *consts)


def _split_a(w_in):
    bf = jnp.bfloat16
    off_qi = 3 * HD
    off_ki = off_qi + IDX_QW
    wq, wk, wv = w_in[:, :HD], w_in[:, HD:2 * HD], w_in[:, 2 * HD:3 * HD]
    wqi = w_in[:, off_qi:off_ki]
    wkw = jnp.pad(w_in[:, off_ki:], ((0, 0), (0, LANES - IDX_DIM - IDX_HEADS)))
    row_scale = jnp.concatenate([
        jnp.ones((IDX_DIM,), jnp.float32),
        jnp.full((IDX_HEADS,), IDX_QW ** -0.5, jnp.float32),
        jnp.zeros((LANES - IDX_DIM - IDX_HEADS,), jnp.float32)]).reshape(LANES, 1)
    return [wq.T.astype(bf), wk.astype(bf), wv.astype(bf), wv.T.astype(bf), wqi.T.astype(bf),
            wkw.astype(bf), wkw.T.astype(bf), row_scale]


def _split_b(w_in):
    bf = jnp.bfloat16
    return [w_in[:, :HD].astype(bf), w_in[:, HD:2 * HD].astype(bf), w_in[:, 2 * HD:].astype(bf)]


def _with_cache(cache, new_bf, lp):
    b, p = cache.shape[:2]
    flat = cache.reshape(b, p, -1).astype(jnp.bfloat16)
    t = new_bf.shape[1]
    return jnp.pad(jnp.concatenate([flat, new_bf], axis=1), ((0, 0), (0, lp - p - t), (0, 0)))


def kernel(x_prompt, x_sample, cache_a_k, cache_a_v, cache_a_idx_k, cache_b_k, cache_b_v,
           state_ffn_conv, meta_tokens, rel_bias, w_a_in, w_a_out, w_b_in, w_b_out,
           ln1_g, ln1_b, ln2_g, ln2_b, w_ffn_gate, w_ffn_up, ffn_conv_w, ffn_conv_b, w_ffn_down):
    bp, seq, _ = x_prompt.shape
    bs, ts, _ = x_sample.shape
    depth = ln1_g.shape[0]
    d_ff = w_ffn_gate.shape[2]
    past = cache_a_k.shape[2]
    alpha = (2 * depth) ** 0.25
    t_p = N_META + seq
    tp_pad = _round_up(t_p, ROW_TILE)
    ls = past + ts
    ls_pad = _round_up(ls, ATTN_GROUP * KEY_TILE)
    assert past % KEY_TILE == 0 and ts % SUBLANES == 0 and ts <= SAMPLE_TILE
    topk_p = min(TOPK_MAX, seq // 4)
    topk_s = min(TOPK_MAX, ls // 4)
    bf = jnp.bfloat16

    meta = jnp.broadcast_to(meta_tokens.astype(x_prompt.dtype)[None], (bp, N_META, D_MODEL))
    xp = jnp.pad(jnp.concatenate([meta, x_prompt], axis=1), ((0, 0), (0, tp_pad - t_p), (0, 0)))
    xs = x_sample
    bias_p = _bias_tiles(rel_bias, ROW_TILE, KEY_TILE)
    bias_s = _bias_tiles(rel_bias, SAMPLE_TILE, KEY_TILE)
    left_p = jnp.zeros((bp, SUBLANES, d_ff), jnp.float32)
    wg_all, wu_all, wd_all = w_ffn_gate.astype(bf), w_ffn_up.astype(bf), w_ffn_down.astype(bf)
    wo_a, wo_b = w_a_out.astype(bf), w_b_out.astype(bf)

    heads = lambda a: a.reshape(a.shape[0], a.shape[1], N_HEADS, HEAD_DIM)
    outs = {k: [] for k in ("akp", "avp", "aip", "aks", "avs", "ais",
                            "bkp", "bvp", "bks", "bvs", "cp", "cs")}
    for i in range(depth):
        j = i // N_MIXERS
        if i % N_MIXERS == 0:
            wts = _split_a(w_a_in[j])
            qt, kc, kb, vc, vt, qit, kic, kib, kwt = _project(
                xp, t_p, ROW_TILE, wts, True, key_rows=_round_up(tp_pad, ATTN_GROUP * KEY_TILE))
            op = _attn_a(qt, qit, kwt, kb, vt, kib, bias_p,
                         tq=ROW_TILE, tk=KEY_TILE, q_off=0, n_lead=N_META, l_valid=t_p, topk=topk_p)
            outs["akp"].append(heads(kc)); outs["avp"].append(heads(vc)); outs["aip"].append(kic)
            xs_pad = jnp.pad(xs, ((0, 0), (0, SAMPLE_TILE - ts), (0, 0)))
            qt, kc, kb, vc, vt, qit, kic, kib, kwt = _project(
                xs_pad, ts, SAMPLE_TILE, wts, True, key_rows=SAMPLE_TILE)
            cache_vt = jnp.swapaxes(cache_a_v[j].reshape(bs, past, HD).astype(bf), 1, 2)
            vt_all = jnp.pad(jnp.concatenate([cache_vt, vt[:, :, :ts]], axis=2),
                             ((0, 0), (0, 0), (0, ls_pad - ls)))
            os_ = _attn_a(qt, qit, kwt, _with_cache(cache_a_k[j], kb[:, :ts], ls_pad), vt_all,
                          _with_cache(cache_a_idx_k[j], kib[:, :ts], ls_pad), bias_s,
                          tq=SAMPLE_TILE, tk=KEY_TILE, q_off=past, n_lead=0, l_valid=ls,
                          topk=topk_s)[:, :ts]
            outs["aks"].append(heads(kc)); outs["avs"].append(heads(vc)); outs["ais"].append(kic)
            w_out = wo_a[j]
        else:
            wts = _split_b(w_b_in[j])
            q, kc, kb, vc, vb = _project(xp, t_p, ROW_TILE, wts, False)
            op = _attn_b(q, kb, vb, tq=ROW_TILE, tk=KEY_TILE, q_off=0)
            outs["bkp"].append(heads(kc)); outs["bvp"].append(heads(vc))
            q, kc, kb, vc, vb = _project(xs, ts, ts, wts, False)
            lb_pad = _round_up(ls, KEY_TILE)
            os_ = _attn_b(q, _with_cache(cache_b_k[j], kb, lb_pad),
                          _with_cache(cache_b_v[j], vb, lb_pad), tq=ts, tk=KEY_TILE, q_off=past)
            outs["bks"].append(heads(kc)); outs["bvs"].append(heads(vc))
            w_out = wo_b[j]
        w = dict(wo=w_out, g1=ln1_g[i], b1=ln1_b[i], wg=wg_all[i], wu=wu_all[i], cw=ffn_conv_w[i],
                 cb=ffn_conv_b[i], wd=wd_all[i], g2=ln2_g[i], b2=ln2_b[i])
        xp, conv_p = _post(xp, op, left_p, t_p, ROW_TILE, w, alpha)
        left_s = jnp.pad(state_ffn_conv[i].astype(jnp.float32),
                         ((0, 0), (SUBLANES - (CONV_W - 1), 0), (0, 0)))
        xs, conv_s = _post(xs, os_, left_s, ts, ts, w, alpha)
        outs["cp"].append(conv_p[:, SUBLANES - (CONV_W - 1):])
        outs["cs"].append(conv_s[:, SUBLANES - (CONV_W - 1):])

    st = jnp.stack
    return (xp[:, N_META:t_p], xs,
            st(outs["akp"]), st(outs["avp"]), st(outs["aip"]),
            st(outs["bkp"]), st(outs["bvp"]), st(outs["cp"]),
            st(outs["aks"]), st(outs["avs"]), st(outs["ais"]),
            st(outs["bks"]), st(outs["bvs"]), st(outs["cs"]))
```

```python
import functools
import math

import jax
import jax.numpy as jnp
from jax import lax
from jax.experimental import pallas as pl
from jax.experimental.pallas import tpu as pltpu

D_MODEL = 1024
N_HEADS = 8
HEAD_DIM = 128
HD = N_HEADS * HEAD_DIM
IDX_HEADS = 8
IDX_DIM = 64
IDX_QW = IDX_HEADS * IDX_DIM
CHUNK = 64
CHUNK_SHIFT = 6
N_META = 16
N_MIXERS = 2
TOPK_MAX = 256
N_BUCKETS = 32
MAX_DISTANCE = 128
CONV_W = 3
LN_EPS = 1e-5

LANES = 128
SUBLANES = 8
ROW_TILE = 256
KEY_TILE = 256
FF_CHUNK = 256
ATTN_GROUP = 4
HEADS_PER_STEP = 2
SB_HEADS_PER_STEP = 4
COUNT_GROUP = 2
SAMPLE_TILE = 128
VMEM_LIMIT = 56 * 1024 * 1024

NEG_BIG = -1e30
INT_MIN = -2 ** 31
KEY_NEG_INF = -2139095041
HALF_BITS = 16
HALF_BIAS = 1 << (HALF_BITS - 1)
HALF_MASK = (1 << HALF_BITS) - 1
PACKED_ROWS = 16
EXP_ZERO_BELOW = -104.0

_NT = (((1,), (1,)), ((), ()))


def _round_up(n, m):
    return -(-n // m) * m


def _layer_norm(x, g, b):
    mu = jnp.mean(x, -1, keepdims=True)
    xc = x - mu
    var = jnp.mean(xc * xc, -1, keepdims=True)
    return xc * lax.rsqrt(var + LN_EPS) * g + b


def _proj_a_kernel(x_ref, wqt_ref, wk_ref, wv_ref, wqit_ref, wkw_ref, wkwt_ref, rs_ref,
                   qt_ref, kc_ref, kb_ref, vc_ref, vt_ref, qit_ref, kic_ref, kib_ref, kwt_ref,
                   *, n_tiles):
    live = pl.program_id(1) < n_tiles
    xb = x_ref[0].astype(jnp.bfloat16)

    def cols(wt_ref):
        return lax.dot_general(wt_ref[...], xb, _NT, preferred_element_type=jnp.float32)

    qt_ref[0] = (cols(wqt_ref) * (HEAD_DIM ** -0.5)).astype(jnp.bfloat16)
    k = jnp.dot(xb, wk_ref[...], preferred_element_type=jnp.float32)
    kc_ref[0] = k
    kb_ref[0] = jnp.where(live, k, 0.0).astype(jnp.bfloat16)
    v = jnp.dot(xb, wv_ref[...], preferred_element_type=jnp.float32)
    vc_ref[0] = v
    vt_ref[0] = jnp.where(live, v.T, 0.0).astype(jnp.bfloat16)
    qit_ref[0] = cols(wqit_ref).astype(jnp.bfloat16)
    kw = jnp.dot(xb, wkw_ref[...], preferred_element_type=jnp.float32)
    kic_ref[0] = kw[:, :IDX_DIM]
    kib_ref[0] = jnp.where(live, kw[:, :IDX_DIM], 0.0).astype(jnp.bfloat16)
    kwt_ref[0] = cols(wkwt_ref) * rs_ref[...]


def _proj_b_kernel(x_ref, wq_ref, wk_ref, wv_ref, q_ref, kc_ref, kb_ref, vc_ref, vb_ref):
    xb = x_ref[0].astype(jnp.bfloat16)
    q = jnp.dot(xb, wq_ref[...], preferred_element_type=jnp.float32)
    q_ref[0] = (q * (HEAD_DIM ** -0.5)).astype(jnp.bfloat16)
    k = jnp.dot(xb, wk_ref[...], preferred_element_type=jnp.float32)
    kc_ref[0] = k
    kb_ref[0] = k.astype(jnp.bfloat16)
    v = jnp.dot(xb, wv_ref[...], preferred_element_type=jnp.float32)
    vc_ref[0] = v
    vb_ref[0] = v.astype(jnp.bfloat16)


def _const_spec(shape):
    return pl.BlockSpec(shape, lambda *_: (0,) * len(shape))


def _project(x, t_valid, tm, weights, is_a, key_rows=None):
    b, tp, _ = x.shape
    n_tiles = tp // tm
    grid = (b, n_tiles)
    row = lambda w: pl.BlockSpec((1, tm, w), lambda i, j: (i, j, 0))
    bf = jnp.bfloat16
    f32 = jnp.float32
    pad = lambda w, dt: jax.ShapeDtypeStruct((b, tp, w), dt)
    exact = lambda w, dt: jax.ShapeDtypeStruct((b, t_valid, w), dt)
    in_specs = [row(D_MODEL)] + [_const_spec(w.shape) for w in weights]
    if is_a:
        assert key_rows % tm == 0 and key_rows >= tp
        grid = (b, key_rows // tm)
        last = n_tiles - 1
        row = lambda w: pl.BlockSpec((1, tm, w), lambda i, j: (i, jnp.minimum(j, last), 0))
        col = lambda w: pl.BlockSpec((1, w, tm), lambda i, j: (i, 0, jnp.minimum(j, last)))
        key_row = lambda w: pl.BlockSpec((1, tm, w), lambda i, j: (i, j, 0))
        key_col = lambda w: pl.BlockSpec((1, w, tm), lambda i, j: (i, 0, j))
        padt = lambda w, dt: jax.ShapeDtypeStruct((b, w, tp), dt)
        out_shape = [padt(HD, bf), exact(HD, f32), jax.ShapeDtypeStruct((b, key_rows, HD), bf),
                     exact(HD, f32), jax.ShapeDtypeStruct((b, HD, key_rows), bf),
                     padt(IDX_QW, bf), exact(IDX_DIM, f32),
                     jax.ShapeDtypeStruct((b, key_rows, IDX_DIM), bf), padt(LANES, f32)]
        out_specs = [col(HD), row(HD), key_row(HD), row(HD), key_col(HD),
                     col(IDX_QW), row(IDX_DIM), key_row(IDX_DIM), col(LANES)]
        in_specs[0] = row(D_MODEL)
        body = functools.partial(_proj_a_kernel, n_tiles=n_tiles)
    else:
        out_shape = [pad(HD, bf), exact(HD, f32), pad(HD, bf), exact(HD, f32), pad(HD, bf)]
        out_specs = [row(HD)] * 5
        body = _proj_b_kernel
    return pl.pallas_call(
        body, grid=grid, in_specs=in_specs, out_specs=out_specs, out_shape=out_shape,
        compiler_params=pltpu.CompilerParams(
            dimension_semantics=("parallel", "arbitrary"), vmem_limit_bytes=VMEM_LIMIT),
        name="proj_a" if is_a else "proj_b",
    )(x, *weights)


def _vis_end(qpos, n_lead, l_valid):
    ve = jnp.where(qpos < n_lead, n_lead,
                   n_lead + ((((qpos - n_lead) >> CHUNK_SHIFT) + 1) << CHUNK_SHIFT))
    return jnp.minimum(ve, l_valid)


def _key_blocks(i, tq, tk, q_off, n_lead, l_valid):
    return (_vis_end(q_off + i * tq + tq - 1, n_lead, l_valid) + tk - 1) // tk


def _dsa_select_kernel(qit_ref, kwt_ref, kib_ref, madd_ref,
                       keys_ref, hi_ref, lo_ref, *, tq, tk, q_off, n_lead, l_valid, topk):
    i = pl.program_id(1)
    qpos0 = q_off + i * tq
    nkb = _key_blocks(i, tq, tk, q_off, n_lead, l_valid)
    g = tk // SUBLANES

    def blk_rows(j):
        return pl.ds(pl.multiple_of(j * g, g), g)

    def blk_keys(j):
        return pl.ds(pl.multiple_of(j * tk, tk), tk)

    g16 = tk // PACKED_ROWS

    def blk_halves(j):
        return pl.ds(pl.multiple_of(j * g16, g16), g16)

    qpos = qpos0 + lax.broadcasted_iota(jnp.int32, (1, tq), 1)
    vis_end = _vis_end(qpos, n_lead, l_valid)

    def score_pair(jj, carry):
        n = 2 * tk
        kib = kib_ref[0, pl.ds(pl.multiple_of(jj * n, n), n), :]
        sc = jnp.zeros((n, tq), jnp.float32)
        for ih in range(IDX_HEADS):
            s = jnp.dot(kib, qit_ref[0, ih * IDX_DIM:(ih + 1) * IDX_DIM, :],
                        preferred_element_type=jnp.float32)
            sc = sc + jnp.maximum(s, 0.0) * kwt_ref[0, IDX_DIM + ih:IDX_DIM + ih + 1, :]
        sc = jnp.where(sc == 0.0, 0.0, sc)
        bits = pltpu.bitcast(sc, jnp.int32)
        key = jnp.where(bits < 0, bits ^ 0x7FFFFFFF, bits)
        pos = jj * n + lax.broadcasted_iota(jnp.int32, (n, tq), 0)
        key = jnp.where(pos < vis_end, key, KEY_NEG_INF)
        keys_ref[pl.ds(pl.multiple_of(jj * (2 * g), 2 * g), 2 * g)] = key.reshape(2 * g, SUBLANES, tq)
        hi_ref[pl.ds(pl.multiple_of(jj * (2 * g16), 2 * g16), 2 * g16)] = (
            (key >> HALF_BITS).astype(jnp.int16).reshape(2 * g16, PACKED_ROWS, tq))
        return carry

    lax.fori_loop(0, (nkb + 1) // 2, score_pair, 0)

    def count_above_and_at(level):
        def total(hit):
            part = jnp.sum(hit.astype(jnp.int32).reshape(4, g // 4, SUBLANES, tq), axis=1)
            return jnp.sum(part, axis=0)

        def blk(j, cnts):
            kk = keys_ref[blk_rows(j)]
            return cnts[0] + total(kk > level), cnts[1] + total(kk >= level)
        zero = jnp.zeros((SUBLANES, tq), jnp.int32)
        cnts = lax.fori_loop(0, nkb, blk, (zero, zero))
        return tuple(jnp.broadcast_to(jnp.sum(c, axis=0, keepdims=True), (SUBLANES, tq))
                     for c in cnts)

    def count_half(half_ref, cand):
        cand16 = jnp.broadcast_to(cand, (PACKED_ROWS, tq)).astype(jnp.int16)

        def group(jj, cnt):
            n = COUNT_GROUP * g16
            hit = jnp.where(half_ref[pl.ds(pl.multiple_of(jj * n, n), n)] >= cand16,
                            jnp.bfloat16(1), jnp.bfloat16(0))
            parts = [hit[a] for a in range(n)]
            while len(parts) > 1:
                parts = [parts[a] + parts[a + 1] for a in range(0, len(parts), 2)]
            return cnt + parts[0].astype(jnp.float32)
        cnt = lax.fori_loop(0, (nkb + COUNT_GROUP - 1) // COUNT_GROUP, group,
                            jnp.zeros((PACKED_ROWS, tq), jnp.float32))
        return jnp.sum(cnt, axis=0, keepdims=True).astype(jnp.int32)

    def fill_pair_partner(half_ref):
        for extra in range(COUNT_GROUP - 1):
            half_ref[blk_halves(nkb + extra)] = jnp.full((g16, PACKED_ROWS, tq), -HALF_BIAS, jnp.int16)

    def kth_largest_half(half_ref, k):
        def step(b, ut):
            cand = ut | (jnp.int32(1) << (HALF_BITS - 1 - b))
            cnt = count_half(half_ref, cand - HALF_BIAS)
            return jnp.where(cnt >= k, cand, ut)
        return lax.fori_loop(0, HALF_BITS, step, jnp.zeros((1, tq), jnp.int32)) - HALF_BIAS

    fill_pair_partner(hi_ref)
    thr_hi = kth_largest_half(hi_ref, topk)
    above = jnp.where(thr_hi == HALF_BIAS - 1, 0,
                      count_half(hi_ref, jnp.minimum(thr_hi + 1, HALF_BIAS - 1)))

    def low_halves(j, carry):
        kk = keys_ref[blk_rows(j)].reshape(tk, tq)
        lo = jnp.where((kk >> HALF_BITS) == thr_hi, (kk & HALF_MASK) - HALF_BIAS, -HALF_BIAS)
        lo_ref[blk_halves(j)] = lo.astype(jnp.int16).reshape(g16, PACKED_ROWS, tq)
        return carry

    lax.fori_loop(0, nkb, low_halves, 0)
    fill_pair_partner(lo_ref)
    thr_lo = kth_largest_half(lo_ref, topk - above)
    thr = jnp.broadcast_to((thr_hi << HALF_BITS) | (thr_lo + HALF_BIAS), (SUBLANES, tq))
    cnt_gt, cnt_ge = count_above_and_at(thr)
    excess = (cnt_ge > topk) & (thr != KEY_NEG_INF)
    any_excess = jnp.max(excess.astype(jnp.int32))
    thr_row = thr[0:1]
    may_keep = (topk - cnt_gt[0:1]).astype(jnp.float32)

    def store_mask(j, sel):
        sel = sel & (keys_ref[blk_rows(j)].reshape(tk, tq) != KEY_NEG_INF)
        madd_ref[0, 0, blk_keys(j), :] = jnp.where(sel, 0.0, NEG_BIG).astype(madd_ref.dtype)

    @pl.when(any_excess > 0)
    def _mask_tied():
        earlier = (lax.broadcasted_iota(jnp.int32, (tk, tk), 1)
                   < lax.broadcasted_iota(jnp.int32, (tk, tk), 0)).astype(jnp.bfloat16)

        def block(j, seen):
            kk = keys_ref[blk_rows(j)].reshape(tk, tq)
            tie = jnp.where(kk == thr_row, 1.0, 0.0)
            before = seen + jnp.dot(earlier, tie.astype(jnp.bfloat16),
                                    preferred_element_type=jnp.float32)
            store_mask(j, (kk > thr_row) | ((kk == thr_row) & (before < may_keep)))
            return before[tk - 1:tk] + tie[tk - 1:tk]

        lax.fori_loop(0, nkb, block, jnp.zeros((1, tq), jnp.float32))

    @pl.when(any_excess == 0)
    def _mask_plain():
        def block(j, carry):
            store_mask(j, keys_ref[blk_rows(j)].reshape(tk, tq) >= thr_row)
            return carry

        lax.fori_loop(0, nkb, block, 0)

    def mask_tail(j, carry):
        madd_ref[0, 0, blk_keys(j), :] = jnp.full((tk, tq), NEG_BIG, madd_ref.dtype)
        return carry

    lax.fori_loop(nkb, madd_ref.shape[2] // tk, mask_tail, 0)


def _dsa_attend_kernel(qt_ref, k_ref, vt_ref, madd_ref, bias_ref, o_ref, s0_ref, s1_ref, p1_ref,
                       *, tq, tk, q_off, n_lead, l_valid):
    i = pl.program_id(2)
    qpos0 = q_off + i * tq
    nkb = _key_blocks(i, tq, tk, q_off, n_lead, l_valid)
    wide = ATTN_GROUP * tk
    n_trips = (nkb + ATTN_GROUP - 1) // ATTN_GROUP
    jq = qpos0 // tk
    s_refs = (s0_ref, s1_ref)
    heads = [slice(hh * HEAD_DIM, (hh + 1) * HEAD_DIM) for hh in range(HEADS_PER_STEP)]

    def keys_of(t):
        return pl.ds(pl.multiple_of(t * wide, wide), wide)

    def scores(hh, t):
        s_refs[hh][...] = jnp.dot(k_ref[0, keys_of(t), heads[hh]], qt_ref[0, heads[hh], :],
                                  preferred_element_type=jnp.float32)

    def softmax(hh, t, m, near):
        parts = []
        for u in range(ATTN_GROUP):
            j = t * ATTN_GROUP + u
            madd = madd_ref[0, 0, pl.ds(pl.multiple_of(j * tk, tk), tk), :]
            part = s_refs[hh][u * tk:(u + 1) * tk] + madd.astype(jnp.float32)
            if near:
                part = part + bias_ref[jnp.clip(j - jq, -2, 1) + 2, hh]
            parts.append(part)
        m_new = m
        for part in parts:
            m_new = jnp.maximum(m_new, jnp.max(part, axis=0, keepdims=True))
        alpha = jnp.exp(m - m_new)
        pb = jnp.concatenate([jnp.exp(part - m_new).astype(jnp.bfloat16) for part in parts], axis=0)
        return m_new, alpha, pb

    ones = jnp.ones((SUBLANES, wide), jnp.bfloat16)

    def weighted_values(hh, t, pb):
        num = jnp.dot(vt_ref[0, heads[hh], keys_of(t)], pb, preferred_element_type=jnp.float32)
        den = jnp.dot(ones, pb, preferred_element_type=jnp.float32)
        return jnp.concatenate([num, den], axis=0)

    def trip(t, carries, near, last=False):
        (m0, acc0), (m1, acc1) = carries
        acc1 = acc1 + weighted_values(1, jnp.maximum(t - 1, 0), p1_ref[...])
        scores(1, t)
        m0, alpha0, pb0 = softmax(0, t, m0, near)
        acc0 = alpha0 * acc0 + weighted_values(0, t, pb0)
        if not last:
            scores(0, t + 1)
        m1, alpha1, pb1 = softmax(1, t, m1, near)
        if last:
            return (m0, acc0), (m1, alpha1 * acc1 + weighted_values(1, t, pb1))
        p1_ref[...] = pb1
        return (m0, acc0), (m1, alpha1 * acc1)

    scores(0, 0)
    p1_ref[...] = jnp.zeros(p1_ref.shape, p1_ref.dtype)
    carries = tuple((jnp.full((1, tq), NEG_BIG, jnp.float32),
                     jnp.zeros((HEAD_DIM + SUBLANES, tq), jnp.float32)) for _ in range(2))
    n_far = jnp.clip((jq - 1) // ATTN_GROUP, 0, n_trips - 1)
    carries = lax.fori_loop(0, n_far, functools.partial(trip, near=False), carries)
    carries = lax.fori_loop(n_far, n_trips - 1, functools.partial(trip, near=True), carries)
    (_, acc0), (_, acc1) = trip(n_trips - 1, carries, near=True, last=True)
    for hs, acc in zip(heads, (acc0, acc1)):
        o_ref[0, :, hs] = (acc[:HEAD_DIM] / acc[HEAD_DIM:HEAD_DIM + 1]).T.astype(o_ref.dtype)


def _attn_a(qt, qit, kwt, k, vt, kib, bias, *, tq, tk, q_off, n_lead, l_valid, topk):
    b, _, tqp = qt.shape
    lp = k.shape[1]
    assert lp % (ATTN_GROUP * tk) == 0 and tk % (4 * SUBLANES) == 0 and tk >= topk
    assert tqp % tq == 0 and q_off % tk == 0 and HEADS_PER_STEP == 2
    n_q = tqp // tq
    hw = HEADS_PER_STEP * HEAD_DIM
    geom = dict(tq=tq, tk=tk, q_off=q_off, n_lead=n_lead, l_valid=l_valid)
    madd = pl.pallas_call(
        functools.partial(_dsa_select_kernel, topk=topk, **geom),
        grid=(b, n_q),
        in_specs=[
            pl.BlockSpec((1, IDX_QW, tq), lambda bi, i: (bi, 0, i)),
            pl.BlockSpec((1, LANES, tq), lambda bi, i: (bi, 0, i)),
            pl.BlockSpec((1, lp, IDX_DIM), lambda bi, i: (bi, 0, 0)),
        ],
        out_specs=pl.BlockSpec((1, 1, lp, tq), lambda bi, i: (bi, i, 0, 0)),
        out_shape=jax.ShapeDtypeStruct((b, n_q, lp, tq), jnp.bfloat16),
        scratch_shapes=[
            pltpu.VMEM((lp // SUBLANES, SUBLANES, tq), jnp.int32),
            pltpu.VMEM(((lp + COUNT_GROUP * tk) // PACKED_ROWS, PACKED_ROWS, tq), jnp.int16),
            pltpu.VMEM(((lp + COUNT_GROUP * tk) // PACKED_ROWS, PACKED_ROWS, tq), jnp.int16),
        ],
        compiler_params=pltpu.CompilerParams(
            dimension_semantics=("parallel", "parallel"), vmem_limit_bytes=VMEM_LIMIT),
        name="dsa_select",
    )(qit, kwt, kib)
    return pl.pallas_call(
        functools.partial(_dsa_attend_kernel, **geom),
        grid=(b, N_HEADS // HEADS_PER_STEP, n_q),
        in_specs=[
            pl.BlockSpec((1, hw, tq), lambda bi, h, i: (bi, h, i)),
            pl.BlockSpec((1, lp, hw), lambda bi, h, i: (bi, 0, h)),
            pl.BlockSpec((1, hw, lp), lambda bi, h, i: (bi, h, 0)),
            pl.BlockSpec((1, 1, lp, tq), lambda bi, h, i: (bi, i, 0, 0)),
            pl.BlockSpec((4, HEADS_PER_STEP, tk, tq), lambda bi, h, i: (0, h, 0, 0)),
        ],
        out_specs=pl.BlockSpec((1, tq, hw), lambda bi, h, i: (bi, i, h)),
        out_shape=jax.ShapeDtypeStruct((b, tqp, HD), jnp.bfloat16),
        scratch_shapes=[
            pltpu.VMEM((ATTN_GROUP * tk, tq), jnp.float32),
            pltpu.VMEM((ATTN_GROUP * tk, tq), jnp.float32),
            pltpu.VMEM((ATTN_GROUP * tk, tq), jnp.bfloat16),
        ],
        compiler_params=pltpu.CompilerParams(
            dimension_semantics=("parallel", "parallel", "parallel"),
            vmem_limit_bytes=VMEM_LIMIT),
        name="dsa_attend",
    )(qt, k, vt, madd, bias)


def _rel_bucket(rel):
    nb = N_BUCKETS // 2
    max_exact = nb // 2
    ret = jnp.where(rel > 0, nb, 0)
    n = jnp.abs(rel)
    nf = jnp.maximum(n, 1).astype(jnp.float32)
    large = max_exact + (jnp.log(nf / max_exact) / math.log(MAX_DISTANCE / max_exact)
                         * (nb - max_exact)).astype(jnp.int32)
    large = jnp.minimum(large, nb - 1)
    return ret + jnp.where(n < max_exact, n, large)


def _bias_tiles(rel_bias, tq, tk):
    w = tk + tq
    tiles = []
    for d in range(-2, 2):
        rel = d * tk + tk - 1 - jnp.arange(w, dtype=jnp.int32)
        vec = rel_bias.astype(jnp.float32)[_rel_bucket(rel)].T
        vec = jnp.roll(vec, -(tk - 1), axis=1)
        skew = jnp.tile(vec, (1, tk))[:, :tk * (w - 1)].reshape(-1, tk, w - 1)
        tiles.append(skew[:, :, :tq])
    tiles = jnp.stack(tiles)
    return tiles - tiles[0, :, :1, :1]


def _attn_b_kernel(q_ref, k_ref, v_ref, tri_ref, o_ref, *, tq, tk, q_off):
    i = pl.program_id(2)
    qpos0 = q_off + i * tq
    heads = [slice(hh * HEAD_DIM, (hh + 1) * HEAD_DIM) for hh in range(SB_HEADS_PER_STEP)]
    tri = tri_ref[...]

    def block(j, diagonal, runs, accs):
        ds = pl.ds(pl.multiple_of(j * tk, tk), tk)
        if diagonal:
            causal = ((j * tk + lax.broadcasted_iota(jnp.int32, (tq, tk), 1))
                      < (qpos0 + lax.broadcasted_iota(jnp.int32, (tq, tk), 0)))
        zs = [lax.dot_general(q_ref[0, :, hs], k_ref[0, ds, hs], _NT,
                              preferred_element_type=jnp.float32) for hs in heads]
        tails, log_1ms, withins = [], [], []
        for z in zs:
            tail = jnp.log(1.0 + jnp.exp(-jnp.abs(z)))
            log_1m = -jnp.maximum(z, 0.0) - tail
            if diagonal:
                log_1m = jnp.where(causal, log_1m, 0.0)
            hi = log_1m.astype(jnp.bfloat16)
            lo = (log_1m - hi.astype(jnp.float32)).astype(jnp.bfloat16)
            withins.append(jnp.dot(hi, tri, preferred_element_type=jnp.float32)
                           + jnp.dot(lo, tri, preferred_element_type=jnp.float32))
            tails.append(tail)
            log_1ms.append(log_1m)
        new_runs, new_accs = [], []
        for hh, hs in enumerate(heads):
            a = jnp.exp(jnp.minimum(zs[hh], 0.0) - tails[hh] + withins[hh] + runs[hh])
            if diagonal:
                a = jnp.where(causal, a, 0.0)
            new_accs.append(accs[hh] + jnp.dot(a.astype(jnp.bfloat16), v_ref[0, ds, hs],
                                               preferred_element_type=jnp.float32))
            new_runs.append(runs[hh] + withins[hh][:, 0:1] + log_1ms[hh][:, 0:1])
        return tuple(new_runs), tuple(new_accs)

    def cond(carry):
        j, runs, _ = carry
        return (j >= 0) & (jnp.max(functools.reduce(jnp.maximum, runs)) > EXP_ZERO_BELOW)

    def body(carry):
        j, runs, accs = carry
        runs, accs = block(j, False, runs, accs)
        return j - 1, runs, accs

    j0 = (qpos0 + tq - 2) // tk
    runs = tuple(jnp.zeros((tq, 1), jnp.float32) for _ in heads)
    accs = tuple(jnp.zeros((tq, HEAD_DIM), jnp.float32) for _ in heads)
    runs, accs = block(j0, True, runs, accs)
    _, _, accs = lax.while_loop(cond, body, (j0 - 1, runs, accs))
    for hs, acc in zip(heads, accs):
        o_ref[0, :, hs] = acc.astype(o_ref.dtype)


def _attn_b(q, k, v, *, tq, tk, q_off):
    b, tqp, _ = q.shape
    lp = k.shape[1]
    tri = (jnp.arange(tk)[:, None] > jnp.arange(tk)[None, :]).astype(jnp.bfloat16)
    assert q_off % tk == 0 and tq <= tk and (tq == tk or tqp == tq)
    hw = SB_HEADS_PER_STEP * HEAD_DIM
    kern = functools.partial(_attn_b_kernel, tq=tq, tk=tk, q_off=q_off)
    return pl.pallas_call(
        kern, grid=(b, N_HEADS // SB_HEADS_PER_STEP, tqp // tq),
        in_specs=[
            pl.BlockSpec((1, tq, hw), lambda bi, h, i: (bi, i, h)),
            pl.BlockSpec((1, lp, hw), lambda bi, h, i: (bi, 0, h)),
            pl.BlockSpec((1, lp, hw), lambda bi, h, i: (bi, 0, h)),
            _const_spec((tk, tk)),
        ],
        out_specs=pl.BlockSpec((1, tq, hw), lambda bi, h, i: (bi, i, h)),
        out_shape=jax.ShapeDtypeStruct((b, tqp, HD), jnp.bfloat16),
        compiler_params=pltpu.CompilerParams(
            dimension_semantics=("parallel", "parallel", "parallel"),
            vmem_limit_bytes=VMEM_LIMIT),
        name="stick_breaking_attention",
    )(q, k, v, tri)


def _post_kernel(x_ref, o_ref, left_ref, wo_ref, g1_ref, b1_ref, wg_ref, wu_ref, cw_ref, cb_ref,
                 wd_ref, g2_ref, b2_ref, y_ref, conv_ref, carry_ref,
                 *, tm, d_ff, alpha, r_last):
    t = pl.program_id(1)

    @pl.when(t == 0)
    def _init():
        carry_ref[...] = left_ref[0]

    mix = jnp.dot(o_ref[0], wo_ref[...], preferred_element_type=jnp.float32)
    x1 = _layer_norm(alpha * x_ref[0] + mix, g1_ref[...], b1_ref[...])
    xb = x1.astype(jnp.bfloat16)
    row = lax.broadcasted_iota(jnp.int32, (tm, FF_CHUNK), 0)
    acc = jnp.zeros((tm, D_MODEL), jnp.float32)
    n_chunks = d_ff // FF_CHUNK

    def gate_up(c):
        cs = slice(c * FF_CHUNK, (c + 1) * FF_CHUNK)
        return (jnp.dot(xb, wg_ref[:, cs], preferred_element_type=jnp.float32),
                jnp.dot(xb, wu_ref[:, cs], preferred_element_type=jnp.float32))

    nxt = gate_up(0)
    for c in range(n_chunks):
        cs = slice(c * FF_CHUNK, (c + 1) * FF_CHUNK)
        g, u = nxt
        if c + 1 < n_chunks:
            nxt = gate_up(c + 1)
        prev = carry_ref[:, cs]
        g1 = jnp.where(row == 0, prev[7:8], pltpu.roll(g, 1, 0))
        g2 = jnp.where(row == 0, prev[6:7], jnp.where(row == 1, prev[7:8], pltpu.roll(g, 2, 0)))
        gc = cb_ref[:, cs] + cw_ref[0:1, cs] * g2
        gc = gc + cw_ref[1:2, cs] * g1
        gc = gc + cw_ref[2:3, cs] * g
        hid = jax.nn.gelu(gc) * u
        acc = acc + jnp.dot(hid.astype(jnp.bfloat16), wd_ref[cs, :],
                            preferred_element_type=jnp.float32)
        carry_ref[:, cs] = g[tm - SUBLANES:tm]
        conv_ref[0, :, cs] = g[r_last:r_last + SUBLANES]

    y_ref[0] = _layer_norm(alpha * x1 + acc, g2_ref[...], b2_ref[...])


def _post(x, o, left8, t_valid, tm, w, alpha):
    b, tp, _ = x.shape
    d_ff = w["wg"].shape[1]
    assert d_ff % FF_CHUNK == 0 and t_valid % SUBLANES == 0 and t_valid >= SUBLANES
    assert (t_valid - 1) // tm == tp // tm - 1
    r_last = (t_valid - SUBLANES) % tm
    row = lambda width: pl.BlockSpec((1, tm, width), lambda i, j: (i, j, 0))
    vec = lambda a: a.reshape(1, -1).astype(jnp.float32)
    consts = [w["wo"], vec(w["g1"]), vec(w["b1"]), w["wg"], w["wu"], w["cw"].astype(jnp.float32),
              vec(w["cb"]), w["wd"], vec(w["g2"]), vec(w["b2"])]
    kern = functools.partial(_post_kernel, tm=tm, d_ff=d_ff, alpha=alpha, r_last=r_last)
    return pl.pallas_call(
        kern, grid=(b, tp // tm),
        in_specs=[row(D_MODEL), row(HD), pl.BlockSpec((1, SUBLANES, d_ff), lambda i, j: (i, 0, 0))]
                 + [_const_spec(c.shape) for c in consts],
        out_specs=[row(D_MODEL), pl.BlockSpec((1, SUBLANES, d_ff), lambda i, j: (i, 0, 0))],
        out_shape=[jax.ShapeDtypeStruct((b, tp, D_MODEL), jnp.float32),
                   jax.ShapeDtypeStruct((b, SUBLANES, d_ff), jnp.float32)],
        scratch_shapes=[pltpu.VMEM((SUBLANES, d_ff), jnp.float32)],
        compiler_params=pltpu.CompilerParams(
            dimension_semantics=("parallel", "arbitrary"), vmem_limit_bytes=VMEM_LIMIT),
        name="outproj_ln_convffn_ln",
    )(x, o, left8, *consts)


def _split_a(w_in):
    bf = jnp.bfloat16
    off_qi = 3 * HD
    off_ki = off_qi + IDX_QW
    wq, wk, wv = w_in[:, :HD], w_in[:, HD:2 * HD], w_in[:, 2 * HD:3 * HD]
    wqi = w_in[:, off_qi:off_ki]
    wkw = jnp.pad(w_in[:, off_ki:], ((0, 0), (0, LANES - IDX_DIM - IDX_HEADS)))
    row_scale = jnp.concatenate([
        jnp.ones((IDX_DIM,), jnp.float32),
        jnp.full((IDX_HEADS,), IDX_QW ** -0.5, jnp.float32),
        jnp.zeros((LANES - IDX_DIM - IDX_HEADS,), jnp.float32)]).reshape(LANES, 1)
    return [wq.T.astype(bf), wk.astype(bf), wv.astype(bf), wqi.T.astype(bf),
            wkw.astype(bf), wkw.T.astype(bf), row_scale]


def _split_b(w_in):
    bf = jnp.bfloat16
    return [w_in[:, :HD].astype(bf), w_in[:, HD:2 * HD].astype(bf), w_in[:, 2 * HD:].astype(bf)]


def _with_cache(cache, new_bf, lp):
    b, p = cache.shape[:2]
    flat = cache.reshape(b, p, -1).astype(jnp.bfloat16)
    t = new_bf.shape[1]
    return jnp.pad(jnp.concatenate([flat, new_bf], axis=1), ((0, 0), (0, lp - p - t), (0, 0)))


def kernel(x_prompt, x_sample, cache_a_k, cache_a_v, cache_a_idx_k, cache_b_k, cache_b_v,
           state_ffn_conv, meta_tokens, rel_bias, w_a_in, w_a_out, w_b_in, w_b_out,
           ln1_g, ln1_b, ln2_g, ln2_b, w_ffn_gate, w_ffn_up, ffn_conv_w, ffn_conv_b, w_ffn_down):
    bp, seq, _ = x_prompt.shape
    bs, ts, _ = x_sample.shape
    depth = ln1_g.shape[0]
    d_ff = w_ffn_gate.shape[2]
    past = cache_a_k.shape[2]
    alpha = (2 * depth) ** 0.25
    t_p = N_META + seq
    tp_pad = _round_up(t_p, ROW_TILE)
    ls = past + ts
    ls_pad = _round_up(ls, ATTN_GROUP * KEY_TILE)
    assert past % KEY_TILE == 0 and ts % SUBLANES == 0 and ts <= SAMPLE_TILE
    topk_p = min(TOPK_MAX, seq // 4)
    topk_s = min(TOPK_MAX, ls // 4)
    bf = jnp.bfloat16

    meta = jnp.broadcast_to(meta_tokens.astype(x_prompt.dtype)[None], (bp, N_META, D_MODEL))
    xp = jnp.pad(jnp.concatenate([meta, x_prompt], axis=1), ((0, 0), (0, tp_pad - t_p), (0, 0)))
    xs = x_sample
    bias_p = _bias_tiles(rel_bias, ROW_TILE, KEY_TILE)
    bias_s = _bias_tiles(rel_bias, SAMPLE_TILE, KEY_TILE)
    left_p = jnp.zeros((bp, SUBLANES, d_ff), jnp.float32)
    wg_all, wu_all, wd_all = w_ffn_gate.astype(bf), w_ffn_up.astype(bf), w_ffn_down.astype(bf)
    wo_a, wo_b = w_a_out.astype(bf), w_b_out.astype(bf)

    heads = lambda a: a.reshape(a.shape[0], a.shape[1], N_HEADS, HEAD_DIM)
    outs = {k: [] for k in ("akp", "avp", "aip", "aks", "avs", "ais",
                            "bkp", "bvp", "bks", "bvs", "cp", "cs")}
    for i in range(depth):
        j = i // N_MIXERS
        if i % N_MIXERS == 0:
            wts = _split_a(w_a_in[j])
            qt, kc, kb, vc, vt, qit, kic, kib, kwt = _project(
                xp, t_p, ROW_TILE, wts, True, key_rows=_round_up(tp_pad, ATTN_GROUP * KEY_TILE))
            op = _attn_a(qt, qit, kwt, kb, vt, kib, bias_p,
                         tq=ROW_TILE, tk=KEY_TILE, q_off=0, n_lead=N_META, l_valid=t_p, topk=topk_p)
            outs["akp"].append(heads(kc)); outs["avp"].append(heads(vc)); outs["aip"].append(kic)
            xs_pad = jnp.pad(xs, ((0, 0), (0, SAMPLE_TILE - ts), (0, 0)))
            qt, kc, kb, vc, vt, qit, kic, kib, kwt = _project(
                xs_pad, ts, SAMPLE_TILE, wts, True, key_rows=SAMPLE_TILE)
            cache_vt = jnp.swapaxes(cache_a_v[j].reshape(bs, past, HD).astype(bf), 1, 2)
            vt_all = jnp.pad(jnp.concatenate([cache_vt, vt[:, :, :ts]], axis=2),
                             ((0, 0), (0, 0), (0, ls_pad - ls)))
            os_ = _attn_a(qt, qit, kwt, _with_cache(cache_a_k[j], kb[:, :ts], ls_pad), vt_all,
                          _with_cache(cache_a_idx_k[j], kib[:, :ts], ls_pad), bias_s,
                          tq=SAMPLE_TILE, tk=KEY_TILE, q_off=past, n_lead=0, l_valid=ls,
                          topk=topk_s)[:, :ts]
            outs["aks"].append(heads(kc)); outs["avs"].append(heads(vc)); outs["ais"].append(kic)
            w_out = wo_a[j]
        else:
            wts = _split_b(w_b_in[j])
            q, kc, kb, vc, vb = _project(xp, t_p, ROW_TILE, wts, False)
            op = _attn_b(q, kb, vb, tq=ROW_TILE, tk=KEY_TILE, q_off=0)
            outs["bkp"].append(heads(kc)); outs["bvp"].append(heads(vc))
            q, kc, kb, vc, vb = _project(xs, ts, ts, wts, False)
            lb_pad = _round_up(ls, KEY_TILE)
            os_ = _attn_b(q, _with_cache(cache_b_k[j], kb, lb_pad),
                          _with_cache(cache_b_v[j], vb, lb_pad), tq=ts, tk=KEY_TILE, q_off=past)
            outs["bks"].append(heads(kc)); outs["bvs"].append(heads(vc))
            w_out = wo_b[j]
        w = dict(wo=w_out, g1=ln1_g[i], b1=ln1_b[i], wg=wg_all[i], wu=wu_all[i], cw=ffn_conv_w[i],
                 cb=ffn_conv_b[i], wd=wd_all[i], g2=ln2_g[i], b2=ln2_b[i])
        xp, conv_p = _post(xp, op, left_p, t_p, ROW_TILE, w, alpha)
        left_s = jnp.pad(state_ffn_conv[i].astype(jnp.float32),
                         ((0, 0), (SUBLANES - (CONV_W - 1), 0), (0, 0)))
        xs, conv_s = _post(xs, os_, left_s, ts, ts, w, alpha)
        outs["cp"].append(conv_p[:, SUBLANES - (CONV_W - 1):])
        outs["cs"].append(conv_s[:, SUBLANES - (CONV_W - 1):])

    st = jnp.stack
    return (xp[:, N_META:t_p], xs,
            st(outs["akp"]), st(outs["avp"]), st(outs["aip"]),
            st(outs["bkp"]), st(outs["bvp"]), st(outs["cp"]),
            st(outs["aks"]), st(outs["avs"]), st(outs["ais"]),
            st(outs["bks"]), st(outs["bvs"]), st(outs["cs"]))
```
